```python
import math
import jax, jax.numpy as jnp
from jax import lax
import numpy as np

D_MODEL = 1024
BATCH = 8
SEQ = 2048
DEPTH = 1

MIX_WIDTH = D_MODEL
MLA_HEADS = 8
MLA_NOPE_DIM = 64
MLA_ROPE_DIM = 32
MLA_QK_DIM = MLA_NOPE_DIM + MLA_ROPE_DIM
MLA_V_DIM = 64
MLA_Q_RANK = 256
MLA_KV_RANK = 128
MLA_WIDTH = MLA_HEADS * MLA_V_DIM
CONV_CH = MIX_WIDTH - MLA_WIDTH
CONV_WIDTH = 31
CONV_PAD = (CONV_WIDTH - 1) // 2
ROPE_BASE = 10000.0
MEM_LEN = 256
MEM_HEADS = 4
MEM_HEAD_DIM = D_MODEL // MEM_HEADS
N_EXPERTS = 16
EXPERT_FF = 2048
CAPACITY_FACTOR = 2
Q_BLOCK = 128
NORM_EPS = 1e-5
DEEPNORM_ALPHA = (2.0 * DEPTH) ** 0.25
DEEPNORM_BETA = (8.0 * DEPTH) ** -0.25
SPLIT_Q = MLA_Q_RANK
SPLIT_KV = SPLIT_Q + MLA_KV_RANK
SPLIT_KR = SPLIT_KV + MLA_ROPE_DIM
IN_PROJ_WIDTH = SPLIT_KR + 2 * CONV_CH

kernel_name = "hybrid_mla_conformer_ecmoe_encoder"


def layer_norm(x, g, b):
    xf = x.astype(jnp.float32)
    mu = jnp.mean(xf, axis=-1, keepdims=True)
    var = jnp.mean(jnp.square(xf - mu), axis=-1, keepdims=True)
    y = (xf - mu) * lax.rsqrt(var + NORM_EPS) * g.astype(jnp.float32) + b.astype(jnp.float32)
    return y.astype(x.dtype)


def rms_norm(x, g):
    xf = x.astype(jnp.float32)
    y = xf * lax.rsqrt(jnp.mean(jnp.square(xf), axis=-1, keepdims=True) + NORM_EPS) * g.astype(jnp.float32)
    return y.astype(x.dtype)


def rotary(x, cos, sin):
    x1, x2 = jnp.split(x, 2, axis=-1)
    return jnp.concatenate([x1 * cos - x2 * sin, x2 * cos + x1 * sin], axis=-1)


def blocked_attention(q, k, v, scale):
    b, s, h, dq = q.shape
    nb = s // Q_BLOCK
    qb = q.reshape(b, nb, Q_BLOCK, h, dq).transpose(1, 0, 2, 3, 4)

    def one_block(qblk):
        sc = jnp.einsum('bqhd,bkhd->bhqk', qblk, k).astype(jnp.float32) * scale
        p = jax.nn.softmax(sc, axis=-1).astype(v.dtype)
        return jnp.einsum('bhqk,bkhd->bqhd', p, v)

    o = lax.map(one_block, qb)
    return o.transpose(1, 0, 2, 3, 4).reshape(b, s, h * v.shape[-1])


def parallel_mixer(x, positions, w_in, q_norm_g, w_uq, kv_norm_g, w_uk, w_uv,
                   conv_w, conv_b, conv_ln_g, conv_ln_b, w_o):
    b, s, _ = x.shape
    hcat = x @ w_in
    c_q = hcat[..., :SPLIT_Q]
    c_kv = hcat[..., SPLIT_Q:SPLIT_KV]
    k_rope = hcat[..., SPLIT_KV:SPLIT_KR]
    conv_in = hcat[..., SPLIT_KR:]

    q = (rms_norm(c_q, q_norm_g) @ w_uq).reshape(b, s, MLA_HEADS, MLA_QK_DIM)
    q_nope, q_rope = q[..., :MLA_NOPE_DIM], q[..., MLA_NOPE_DIM:]
    ckv = rms_norm(c_kv, kv_norm_g)
    k_nope = (ckv @ w_uk).reshape(b, s, MLA_HEADS, MLA_NOPE_DIM)
    v = (ckv @ w_uv).reshape(b, s, MLA_HEADS, MLA_V_DIM)
    inv_freq = ROPE_BASE ** (-jnp.arange(0, MLA_ROPE_DIM, 2, dtype=jnp.float32) / MLA_ROPE_DIM)
    ang = positions.astype(jnp.float32)[..., None] * inv_freq
    cos = jnp.cos(ang).astype(x.dtype)
    sin = jnp.sin(ang).astype(x.dtype)
    q_rope = rotary(q_rope, cos[:, :, None, :], sin[:, :, None, :])
    k_rope = rotary(k_rope, cos, sin)
    q_full = jnp.concatenate([q_nope, q_rope], axis=-1)
    k_full = jnp.concatenate(
        [k_nope, jnp.broadcast_to(k_rope[:, :, None, :], (b, s, MLA_HEADS, MLA_ROPE_DIM))], axis=-1)
    attn_out = blocked_attention(q_full, k_full, v, 1.0 / math.sqrt(MLA_QK_DIM))

    a, g = conv_in[..., :CONV_CH], conv_in[..., CONV_CH:]
    u = a * jax.nn.sigmoid(g)
    u = lax.conv_general_dilated(
        u, conv_w[:, None, :], window_strides=(1,), padding=[(CONV_PAD, CONV_PAD)],
        dimension_numbers=('NWC', 'WIO', 'NWC'), feature_group_count=CONV_CH) + conv_b
    u = jax.nn.silu(layer_norm(u, conv_ln_g, conv_ln_b))

    return jnp.concatenate([attn_out, u], axis=-1) @ w_o


def memory_cross_attention(x, mem, w_q, w_k, w_v, w_o):
    b, s, _ = x.shape
    m = mem.shape[1]
    q = (x @ w_q).reshape(b, s, MEM_HEADS, MEM_HEAD_DIM)
    k = (mem @ w_k).reshape(b, m, MEM_HEADS, MEM_HEAD_DIM)
    v = (mem @ w_v).reshape(b, m, MEM_HEADS, MEM_HEAD_DIM)
    sc = jnp.einsum('bshd,bmhd->bhsm', q, k).astype(jnp.float32) / math.sqrt(MEM_HEAD_DIM)
    p = jax.nn.softmax(sc, axis=-1).astype(v.dtype)
    o = jnp.einsum('bhsm,bmhd->bshd', p, v).reshape(b, s, D_MODEL)
    return o @ w_o


def expert_choice_moe(x, w_router, w_gate, w_up, w_down):
    b, s, d = x.shape
    cap = CAPACITY_FACTOR * s // N_EXPERTS
    affinity = jax.nn.softmax((x @ w_router).astype(jnp.float32), axis=-1)
    gate, idx = lax.top_k(affinity.transpose(0, 2, 1), cap)
    bidx = jnp.arange(b)[:, None, None]
    xs = x[bidx, idx]
    hid = jax.nn.silu(jnp.einsum('becd,edf->becf', xs, w_gate)) * jnp.einsum('becd,edf->becf', xs, w_up)
    y = jnp.einsum('becf,efd->becd', hid, w_down).astype(jnp.float32) * gate[..., None]
    out = jnp.zeros((b, s, d), jnp.float32).at[bidx, idx].add(y)
    return out.astype(x.dtype)


def setup_inputs(seed: int = 0) -> dict:
    key = jax.random.key(seed)
    ks = jax.random.split(key, 32)
    f32 = jnp.float32

    def nrm(k, shape, fan_in, scale=1.0):
        return jax.random.normal(k, shape, f32) * (scale * fan_in ** -0.5)

    def gain(k, shape):
        return 1.0 + 0.02 * jax.random.normal(k, shape, f32)

    def bias(k, shape):
        return 0.02 * jax.random.normal(k, shape, f32)

    L = DEPTH
    beta = DEEPNORM_BETA
    x = jax.random.normal(ks[0], (BATCH, SEQ, D_MODEL), f32)
    mem = jax.random.normal(ks[1], (BATCH, MEM_LEN, D_MODEL), f32)
    offs = jax.random.randint(ks[2], (BATCH, 1), 0, 4096)
    positions = (jnp.arange(SEQ, dtype=jnp.int32)[None, :] + offs).astype(jnp.int32)
    return {
        "x": x,
        "mem": mem,
        "positions": positions,
        "w_in": nrm(ks[3], (L, D_MODEL, IN_PROJ_WIDTH), D_MODEL),
        "q_norm_g": gain(ks[4], (L, MLA_Q_RANK)),
        "w_uq": nrm(ks[5], (L, MLA_Q_RANK, MLA_HEADS * MLA_QK_DIM), MLA_Q_RANK),
        "kv_norm_g": gain(ks[6], (L, MLA_KV_RANK)),
        "w_uk": nrm(ks[7], (L, MLA_KV_RANK, MLA_HEADS * MLA_NOPE_DIM), MLA_KV_RANK),
        "w_uv": nrm(ks[8], (L, MLA_KV_RANK, MLA_HEADS * MLA_V_DIM), MLA_KV_RANK, beta),
        "conv_w": nrm(ks[9], (L, CONV_WIDTH, CONV_CH), CONV_WIDTH),
        "conv_b": bias(ks[10], (L, CONV_CH)),
        "conv_ln_g": gain(ks[11], (L, CONV_CH)),
        "conv_ln_b": bias(ks[12], (L, CONV_CH)),
        "w_o": nrm(ks[13], (L, MIX_WIDTH, D_MODEL), MIX_WIDTH, beta),
        "ln1_g": gain(ks[14], (L, D_MODEL)),
        "ln1_b": bias(ks[15], (L, D_MODEL)),
        "xa_w_q": nrm(ks[16], (L, D_MODEL, D_MODEL), D_MODEL),
        "xa_w_k": nrm(ks[17], (L, D_MODEL, D_MODEL), D_MODEL),
        "xa_w_v": nrm(ks[18], (L, D_MODEL, D_MODEL), D_MODEL, beta),
        "xa_w_o": nrm(ks[19], (L, D_MODEL, D_MODEL), D_MODEL, beta),
        "ln2_g": gain(ks[20], (L, D_MODEL)),
        "ln2_b": bias(ks[21], (L, D_MODEL)),
        "w_router": nrm(ks[22], (L, D_MODEL, N_EXPERTS), D_MODEL),
        "w_gate": nrm(ks[23], (L, N_EXPERTS, D_MODEL, EXPERT_FF), D_MODEL),
        "w_up": nrm(ks[24], (L, N_EXPERTS, D_MODEL, EXPERT_FF), D_MODEL),
        "w_down": nrm(ks[25], (L, N_EXPERTS, EXPERT_FF, D_MODEL), EXPERT_FF, beta),
        "ln3_g": gain(ks[26], (L, D_MODEL)),
        "ln3_b": bias(ks[27], (L, D_MODEL)),
    }


def reference(x, mem, positions, w_in, q_norm_g, w_uq, kv_norm_g, w_uk, w_uv,
              conv_w, conv_b, conv_ln_g, conv_ln_b, w_o, ln1_g, ln1_b,
              xa_w_q, xa_w_k, xa_w_v, xa_w_o, ln2_g, ln2_b,
              w_router, w_gate, w_up, w_down, ln3_g, ln3_b):
    h = x
    for l in range(DEPTH):
        mix = parallel_mixer(h, positions, w_in[l], q_norm_g[l], w_uq[l], kv_norm_g[l], w_uk[l], w_uv[l],
                             conv_w[l], conv_b[l], conv_ln_g[l], conv_ln_b[l], w_o[l])
        h = layer_norm(DEEPNORM_ALPHA * h + mix, ln1_g[l], ln1_b[l])
        xa = memory_cross_attention(h, mem, xa_w_q[l], xa_w_k[l], xa_w_v[l], xa_w_o[l])
        h = layer_norm(DEEPNORM_ALPHA * h + xa, ln2_g[l], ln2_b[l])
        ff = expert_choice_moe(h, w_router[l], w_gate[l], w_up[l], w_down[l])
        h = layer_norm(DEEPNORM_ALPHA * h + ff, ln3_g[l], ln3_b[l])
    return h
```

```python
import functools
import math

import jax
import jax.numpy as jnp
from jax import lax
from jax.experimental import pallas as pl
from jax.experimental.pallas import tpu as pltpu

F32 = jnp.float32
BF16 = jnp.bfloat16

D_MODEL = 1024
MLA_HEADS = 8
MLA_NOPE_DIM = 64
MLA_ROPE_DIM = 32
MLA_QK_DIM = MLA_NOPE_DIM + MLA_ROPE_DIM
MLA_V_DIM = 64
MLA_Q_RANK = 256
MLA_KV_RANK = 128
MLA_WIDTH = MLA_HEADS * MLA_V_DIM
CONV_CH = D_MODEL - MLA_WIDTH
CONV_WIDTH = 31
CONV_PAD = (CONV_WIDTH - 1) // 2
ROPE_BASE = 10000.0
MEM_HEADS = 4
MEM_HEAD_DIM = D_MODEL // MEM_HEADS
N_EXPERTS = 16
EXPERT_FF = 2048
CAPACITY_FACTOR = 2
NORM_EPS = 1e-5
DEPTH = 1
DEEPNORM_ALPHA = (2.0 * DEPTH) ** 0.25
SPLIT_Q = MLA_Q_RANK
SPLIT_KV = SPLIT_Q + MLA_KV_RANK
SPLIT_KR = SPLIT_KV + MLA_ROPE_DIM

HEAD_PAD = 128
QK_PAD = MLA_HEADS * HEAD_PAD
OFF_CQ = 0
OFF_CKV = OFF_CQ + MLA_Q_RANK
OFF_KR = OFF_CKV + MLA_KV_RANK
OFF_KRS = OFF_KR + HEAD_PAD
OFF_A = OFF_KRS + HEAD_PAD
OFF_G = OFF_A + CONV_CH
IN_EXT = OFF_G + CONV_CH

HALO = 16
VMEM_LIMIT = 56 * 1024 * 1024


def _cparams(sem):
    return pltpu.CompilerParams(dimension_semantics=sem, vmem_limit_bytes=VMEM_LIMIT)


def _layer_norm(v, g, b):
    mu = jnp.mean(v, axis=-1, keepdims=True)
    d = v - mu
    var = jnp.mean(d * d, axis=-1, keepdims=True)
    return d * lax.rsqrt(var + NORM_EPS) * g + b


def _rms_norm(v, g):
    return v * lax.rsqrt(jnp.mean(v * v, axis=-1, keepdims=True) + NORM_EPS) * g


def _sigmoid(v):
    return 1.0 / (1.0 + jnp.exp(-v))


def _dot(a, b):
    return jnp.dot(a, b, preferred_element_type=F32)


def _dot_nt(a, b, precision=None):
    return lax.dot_general(a, b, (((1,), (1,)), ((), ())), preferred_element_type=F32, precision=precision)


def _front_kernel(x_ref, pos_ref, win_ref, qg_ref, wq_ref, kvg_ref, wkv_ref, invf_ref,
                  q_out, k_out, v_out, u_out):
    x = x_ref[0].astype(BF16)
    hc = _dot(x, win_ref[...])
    cqn = _rms_norm(hc[:, OFF_CQ:OFF_CKV], qg_ref[...])
    qq = _dot(cqn.astype(BF16), wq_ref[...])
    ckvn = _rms_norm(hc[:, OFF_CKV:OFF_KR], kvg_ref[...])
    kv = _dot(ckvn.astype(BF16), wkv_ref[...])
    ang = pos_ref[0] * invf_ref[...]
    cos = jnp.cos(ang)
    sin = jnp.sin(ang)
    krot = hc[:, OFF_KR:OFF_KRS] * cos + hc[:, OFF_KRS:OFF_A] * sin
    scale = 1.0 / math.sqrt(MLA_QK_DIM)
    for h in range(MLA_HEADS):
        lo, hi = h * HEAD_PAD, (h + 1) * HEAD_PAD
        qh = (qq[:, lo:hi] * cos + qq[:, QK_PAD + lo:QK_PAD + hi] * sin) * scale
        q_out[0, :, lo:hi] = qh.astype(BF16)
        k_out[0, :, lo:hi] = (kv[:, lo:hi] + krot).astype(BF16)
    v_out[0] = kv[:, QK_PAD:].astype(BF16)
    u_out[0] = hc[:, OFF_A:OFF_G] * _sigmoid(hc[:, OFF_G:IN_EXT])


def _front(x, posf, w_in_ext, qg, wq_ext, kvg, wkv_ext, invf, tm):
    b, s, d = x.shape
    const = lambda bi, i: (0, 0)
    return pl.pallas_call(
        _front_kernel,
        grid=(b, s // tm),
        in_specs=[
            pl.BlockSpec((1, tm, d), lambda bi, i: (bi, i, 0)),
            pl.BlockSpec((1, tm, 1), lambda bi, i: (bi, i, 0)),
            pl.BlockSpec(w_in_ext.shape, const),
            pl.BlockSpec(qg.shape, const),
            pl.BlockSpec(wq_ext.shape, const),
            pl.BlockSpec(kvg.shape, const),
            pl.BlockSpec(wkv_ext.shape, const),
            pl.BlockSpec(invf.shape, const),
        ],
        out_specs=[
            pl.BlockSpec((1, tm, QK_PAD), lambda bi, i: (bi, i, 0)),
            pl.BlockSpec((1, tm, QK_PAD), lambda bi, i: (bi, i, 0)),
            pl.BlockSpec((1, tm, MLA_WIDTH), lambda bi, i: (bi, i, 0)),
            pl.BlockSpec((1, tm, CONV_CH), lambda bi, i: (bi, i, 0)),
        ],
        out_shape=[
            jax.ShapeDtypeStruct((b, s, QK_PAD), BF16),
            jax.ShapeDtypeStruct((b, s, QK_PAD), BF16),
            jax.ShapeDtypeStruct((b, s, MLA_WIDTH), BF16),
            jax.ShapeDtypeStruct((b, s, CONV_CH), F32),
        ],
        compiler_params=_cparams(("parallel", "parallel")),
        name="front",
    )(x, posf, w_in_ext, qg, wq_ext, kvg, wkv_ext, invf)


def _attn_kernel(q_ref, k_ref, v_ref, o_ref):
    for h in range(MLA_HEADS):
        qh = q_ref[0, :, h * HEAD_PAD:(h + 1) * HEAD_PAD]
        kh = k_ref[0, :, h * HEAD_PAD:(h + 1) * HEAD_PAD]
        sc = _dot_nt(qh, kh)
        m = jnp.max(sc, axis=-1, keepdims=True)
        p = jnp.exp(sc - m)
        l = jnp.sum(p, axis=-1, keepdims=True)
        vh = v_ref[0, :, h * MLA_V_DIM:(h + 1) * MLA_V_DIM]
        o = _dot(p.astype(BF16), vh) / l
        o_ref[0, :, h * MLA_V_DIM:(h + 1) * MLA_V_DIM] = o.astype(BF16)


def _attention(q, k, v, tq):
    b, s, _ = q.shape
    return pl.pallas_call(
        _attn_kernel,
        grid=(b, s // tq),
        in_specs=[
            pl.BlockSpec((1, tq, QK_PAD), lambda bi, i: (bi, i, 0)),
            pl.BlockSpec((1, s, QK_PAD), lambda bi, i: (bi, 0, 0)),
            pl.BlockSpec((1, s, MLA_WIDTH), lambda bi, i: (bi, 0, 0)),
        ],
        out_specs=pl.BlockSpec((1, tq, MLA_WIDTH), lambda bi, i: (bi, i, 0)),
        out_shape=jax.ShapeDtypeStruct((b, s, MLA_WIDTH), BF16),
        compiler_params=_cparams(("parallel", "parallel")),
        name="attn",
    )(q, k, v)


def _mem_kv_kernel(mem_ref, wk_ref, wv_ref, k_out, v_out):
    m = mem_ref[0].astype(BF16)
    k_out[0] = _dot(m, wk_ref[...]).astype(BF16)
    v_out[0] = _dot(m, wv_ref[...]).astype(BF16)


def _mem_kv(mem, wk, wv):
    b, m, d = mem.shape
    const = lambda bi: (0, 0)
    return pl.pallas_call(
        _mem_kv_kernel,
        grid=(b,),
        in_specs=[
            pl.BlockSpec((1, m, d), lambda bi: (bi, 0, 0)),
            pl.BlockSpec(wk.shape, const),
            pl.BlockSpec(wv.shape, const),
        ],
        out_specs=[pl.BlockSpec((1, m, d), lambda bi: (bi, 0, 0))] * 2,
        out_shape=[jax.ShapeDtypeStruct((b, m, d), BF16)] * 2,
        compiler_params=_cparams(("parallel",)),
        name="mem_kv",
    )(mem, wk, wv)


CONV_SUB = 64


def _mix_out_kernel(ucur_ref, uprev_ref, unext_ref, attn_ref, x_ref, cw_ref, cb_ref, cg_ref, cbeta_ref,
                    wo_ref, g1_ref, b1_ref, h1_out, win_ref, u_scr):
    tm = ucur_ref.shape[1]
    i = pl.program_id(1)
    last = pl.num_programs(1) - 1
    win_ref[0:HALO, :] = jnp.where(i > 0, uprev_ref[0], 0.0)
    win_ref[HALO:HALO + tm, :] = ucur_ref[0]
    win_ref[HALO + tm:2 * HALO + tm, :] = jnp.where(i < last, unext_ref[0], 0.0)
    for r in range(tm // CONV_SUB):
        base = r * CONV_SUB + HALO - CONV_PAD
        acc = jnp.broadcast_to(cb_ref[...], (CONV_SUB, CONV_CH))
        for t in range(CONV_WIDTH):
            acc = acc + win_ref[base + t:base + t + CONV_SUB, :] * cw_ref[t:t + 1, :]
        y = _layer_norm(acc, cg_ref[...], cbeta_ref[...])
        u_scr[r * CONV_SUB:(r + 1) * CONV_SUB, :] = (y * _sigmoid(y)).astype(BF16)
    mix = _dot(attn_ref[0], wo_ref[0:MLA_WIDTH, :]) + _dot(u_scr[...], wo_ref[MLA_WIDTH:, :])
    h1_out[0] = _layer_norm(DEEPNORM_ALPHA * x_ref[0] + mix, g1_ref[...], b1_ref[...])


def _mix_out(u_pre, attn, x, conv_w, conv_b, conv_g, conv_beta, w_o, g1, b1, tm):
    b, s, d = x.shape
    nh = tm // HALO
    const = lambda bi, i: (0, 0)
    return pl.pallas_call(
        _mix_out_kernel,
        grid=(b, s // tm),
        in_specs=[
            pl.BlockSpec((1, tm, CONV_CH), lambda bi, i: (bi, i, 0)),
            pl.BlockSpec((1, HALO, CONV_CH), lambda bi, i: (bi, jnp.maximum(i * nh - 1, 0), 0)),
            pl.BlockSpec((1, HALO, CONV_CH), lambda bi, i: (bi, jnp.minimum((i + 1) * nh, s // HALO - 1), 0)),
            pl.BlockSpec((1, tm, MLA_WIDTH), lambda bi, i: (bi, i, 0)),
            pl.BlockSpec((1, tm, d), lambda bi, i: (bi, i, 0)),
            pl.BlockSpec(conv_w.shape, const),
            pl.BlockSpec(conv_b.shape, const),
            pl.BlockSpec(conv_g.shape, const),
            pl.BlockSpec(conv_beta.shape, const),
            pl.BlockSpec(w_o.shape, const),
            pl.BlockSpec(g1.shape, const),
            pl.BlockSpec(b1.shape, const),
        ],
        out_specs=pl.BlockSpec((1, tm, d), lambda bi, i: (bi, i, 0)),
        out_shape=jax.ShapeDtypeStruct((b, s, d), F32),
        scratch_shapes=[
            pltpu.VMEM((tm + 2 * HALO, CONV_CH), F32),
            pltpu.VMEM((tm, CONV_CH), BF16),
        ],
        compiler_params=_cparams(("parallel", "parallel")),
        name="mix_out",
    )(u_pre, u_pre, u_pre, attn, x, conv_w, conv_b, conv_g, conv_beta, w_o, g1, b1)


def _xattn_kernel(h1_ref, wq_ref, kx_ref, vx_ref, wo_ref, g2_ref, b2_ref, wr_ref,
                  h2_out, h2b_out, aff_out, o_scr):
    h1 = h1_ref[0]
    q = (_dot(h1.astype(BF16), wq_ref[...]) * (1.0 / math.sqrt(MEM_HEAD_DIM))).astype(BF16)
    for h in range(MEM_HEADS):
        lo, hi = h * MEM_HEAD_DIM, (h + 1) * MEM_HEAD_DIM
        sc = _dot_nt(q[:, lo:hi], kx_ref[0, :, lo:hi])
        m = jnp.max(sc, axis=-1, keepdims=True)
        p = jnp.exp(sc - m)
        l = jnp.sum(p, axis=-1, keepdims=True)
        o_scr[:, lo:hi] = (_dot(p.astype(BF16), vx_ref[0, :, lo:hi]) / l).astype(BF16)
    xa = _dot(o_scr[...], wo_ref[...])
    h2 = _layer_norm(DEEPNORM_ALPHA * h1 + xa, g2_ref[...], b2_ref[...])
    h2_out[0] = h2
    h2b_out[0] = h2.astype(BF16)
    logits = _dot_nt(wr_ref[...], h2, precision=lax.Precision.HIGHEST)
    e = jnp.exp(logits - jnp.max(logits, axis=0, keepdims=True))
    aff_out[0] = e / jnp.sum(e, axis=0, keepdims=True)


def _xattn(h1, wq, kx, vx, wo, g2, b2, w_router_t, tm):
    b, s, d = h1.shape
    m = kx.shape[1]
    const = lambda bi, i: (0, 0)
    return pl.pallas_call(
        _xattn_kernel,
        grid=(b, s // tm),
        in_specs=[
            pl.BlockSpec((1, tm, d), lambda bi, i: (bi, i, 0)),
            pl.BlockSpec(wq.shape, const),
            pl.BlockSpec((1, m, d), lambda bi, i: (bi, 0, 0)),
            pl.BlockSpec((1, m, d), lambda bi, i: (bi, 0, 0)),
            pl.BlockSpec(wo.shape, const),
            pl.BlockSpec(g2.shape, const),
            pl.BlockSpec(b2.shape, const),
            pl.BlockSpec(w_router_t.shape, const),
        ],
        out_specs=[
            pl.BlockSpec((1, tm, d), lambda bi, i: (bi, i, 0)),
            pl.BlockSpec((1, tm, d), lambda bi, i: (bi, i, 0)),
            pl.BlockSpec((1, N_EXPERTS, tm), lambda bi, i: (bi, 0, i)),
        ],
        out_shape=[
            jax.ShapeDtypeStruct((b, s, d), F32),
            jax.ShapeDtypeStruct((b, s, d), BF16),
            jax.ShapeDtypeStruct((b, N_EXPERTS, s), F32),
        ],
        scratch_shapes=[pltpu.VMEM((tm, d), BF16)],
        compiler_params=_cparams(("parallel", "parallel")),
        name="xattn",
    )(h1, wq, kx, vx, wo, g2, b2, w_router_t)


def _topk_kernel(aff_ref, slot_out, slot_t_out, *, cap):
    aff = aff_ref[...]
    rows, s = aff.shape
    capf = jnp.float32(cap)

    def not_done(carry):
        return carry[2] > 0

    def bisect(carry):
        lo, hi, _ = carry
        mid = 0.5 * (lo + hi)
        take = jnp.sum(jnp.where(aff >= mid, 1.0, 0.0), axis=1, keepdims=True) >= capf
        lo = jnp.where(take, mid, lo)
        hi = jnp.where(take, hi, mid)
        smallest_in = jnp.min(jnp.where(aff >= lo, aff, jnp.inf), axis=1, keepdims=True)
        largest_in = jnp.max(jnp.where(aff < hi, aff, -jnp.inf), axis=1, keepdims=True)
        open_rows = jnp.sum(jnp.where(smallest_in == largest_in, 0.0, 1.0))
        return lo, hi, open_rows.astype(jnp.int32)

    _, hi, _ = lax.while_loop(not_done, bisect,
                              (jnp.zeros((rows, 1), F32), jnp.full((rows, 1), 2.0, F32), jnp.int32(1)))
    th = jnp.max(jnp.where(aff < hi, aff, -jnp.inf), axis=1, keepdims=True)
    gt = aff > th
    eq = aff == th
    n_gt = jnp.sum(jnp.where(gt, 1.0, 0.0), axis=1, keepdims=True)
    tri = jnp.where(lax.broadcasted_iota(jnp.int32, (s, s), 0) < lax.broadcasted_iota(jnp.int32, (s, s), 1),
                    1.0, 0.0).astype(BF16)
    tie_rank = _dot(jnp.where(eq, 1.0, 0.0).astype(BF16), tri)
    sel = jnp.logical_or(gt, jnp.logical_and(eq, tie_rank < (capf - n_gt)))
    pos = _dot(jnp.where(sel, 1.0, 0.0).astype(BF16), tri)
    slot = jnp.where(sel, pos, -1.0)
    slot_out[...] = slot
    slot_t_out[...] = slot.T


def _topk(aff2d, cap):
    rows, s = aff2d.shape
    return pl.pallas_call(
        functools.partial(_topk_kernel, cap=cap),
        out_shape=[
            jax.ShapeDtypeStruct((rows, s), F32),
            jax.ShapeDtypeStruct((s, rows), F32),
        ],
        compiler_params=pltpu.CompilerParams(vmem_limit_bytes=VMEM_LIMIT),
        name="topk",
    )(aff2d)


def _dispatch_kernel(h2b_ref, slot_ref, aff_ref, xs_out, gate_out, *, cap):
    slot = slot_ref[0, 0]
    s = slot.shape[1]
    hit = slot == lax.broadcasted_iota(jnp.int32, (cap, s), 0).astype(F32)
    xs_out[0, 0] = _dot(jnp.where(hit, 1.0, 0.0).astype(BF16), h2b_ref[0]).astype(BF16)
    gate_out[0, 0] = jnp.sum(jnp.where(hit, aff_ref[0, 0], 0.0), axis=1, keepdims=True)


def _dispatch(h2b, slot4, aff4, cap):
    b, s, d = h2b.shape
    e = slot4.shape[1]
    return pl.pallas_call(
        functools.partial(_dispatch_kernel, cap=cap),
        grid=(b, e),
        in_specs=[
            pl.BlockSpec((1, s, d), lambda bi, ei: (bi, 0, 0)),
            pl.BlockSpec((1, 1, 1, s), lambda bi, ei: (bi, ei, 0, 0)),
            pl.BlockSpec((1, 1, 1, s), lambda bi, ei: (bi, ei, 0, 0)),
        ],
        out_specs=[
            pl.BlockSpec((1, 1, cap, d), lambda bi, ei: (ei, bi, 0, 0)),
            pl.BlockSpec((1, 1, cap, 1), lambda bi, ei: (ei, bi, 0, 0)),
        ],
        out_shape=[
            jax.ShapeDtypeStruct((e, b, cap, d), BF16),
            jax.ShapeDtypeStruct((e, b, cap, 1), F32),
        ],
        compiler_params=_cparams(("parallel", "parallel")),
        name="dispatch",
    )(h2b, slot4, aff4)


FFN_ROWS = 512


def _experts_kernel(xs_ref, gate_ref, wg_ref, wu_ref, wd_ref, y_out, acc_ref, wg_s, wu_s, wd_s):
    f = pl.program_id(1)
    wg_s[...] = wg_ref[0].astype(BF16)
    wu_s[...] = wu_ref[0].astype(BF16)
    wd_s[...] = wd_ref[0].astype(BF16)
    n_rows = xs_ref.shape[1]

    def body(mi, carry):
        rows = pl.ds(pl.multiple_of(mi * FFN_ROWS, FFN_ROWS), FFN_ROWS)
        xs = xs_ref[0, rows, :]
        g = _dot(xs, wg_s[...])
        u = _dot(xs, wu_s[...])
        hid = (g * _sigmoid(g) * u).astype(BF16)
        part = _dot(hid, wd_s[...])

        @pl.when(f == 0)
        def _():
            acc_ref[rows, :] = part

        @pl.when(f > 0)
        def _():
            acc_ref[rows, :] += part

        return carry

    lax.fori_loop(0, n_rows // FFN_ROWS, body, 0)

    @pl.when(f == pl.num_programs(1) - 1)
    def _():
        y_out[0] = (acc_ref[...] * gate_ref[0]).astype(BF16)


def _experts(xs, gate, w_gate, w_up, w_down, fc):
    e, n, d = xs.shape
    ff = w_gate.shape[2]
    return pl.pallas_call(
        _experts_kernel,
        grid=(e, ff // fc),
        in_specs=[
            pl.BlockSpec((1, n, d), lambda ei, fi: (ei, 0, 0)),
            pl.BlockSpec((1, n, 1), lambda ei, fi: (ei, 0, 0)),
            pl.BlockSpec((1, d, fc), lambda ei, fi: (ei, 0, fi)),
            pl.BlockSpec((1, d, fc), lambda ei, fi: (ei, 0, fi)),
            pl.BlockSpec((1, fc, d), lambda ei, fi: (ei, fi, 0)),
        ],
        out_specs=pl.BlockSpec((1, n, d), lambda ei, fi: (ei, 0, 0)),
        out_shape=jax.ShapeDtypeStruct((e, n, d), BF16),
        scratch_shapes=[
            pltpu.VMEM((n, d), F32),
            pltpu.VMEM((d, fc), BF16),
            pltpu.VMEM((d, fc), BF16),
            pltpu.VMEM((fc, d), BF16),
        ],
        compiler_params=_cparams(("parallel", "arbitrary")),
        name="experts",
    )(xs, gate, w_gate, w_up, w_down)


def _combine_kernel(slot_t_ref, y_ref, h2_ref, g3_ref, b3_ref, out_ref, *, cap):
    bi = pl.program_id(0)
    n_e = y_ref.shape[0]
    rows = slot_t_ref.shape[1]
    width = n_e * cap
    col = lax.broadcasted_iota(jnp.int32, (rows, width), 1) // cap
    spread = jnp.where(lax.broadcasted_iota(jnp.int32, (rows, width), 0) == bi * n_e + col, 1.0, 0.0).astype(BF16)
    slot_wide = _dot(slot_t_ref[...].astype(BF16), spread)
    lane_c = (lax.broadcasted_iota(jnp.int32, (1, width), 1) % cap).astype(F32)
    onehot = jnp.where(slot_wide == lane_c, 1.0, 0.0).astype(BF16)
    ff = _dot(onehot, y_ref[:, 0].reshape(width, y_ref.shape[3]))
    out_ref[0] = _layer_norm(DEEPNORM_ALPHA * h2_ref[0] + ff, g3_ref[...], b3_ref[...])


def _combine(slot_t, y4, h2, g3, b3, cap, ts):
    b, s, d = h2.shape
    e = y4.shape[0]
    const = lambda bi, i: (0, 0)
    return pl.pallas_call(
        functools.partial(_combine_kernel, cap=cap),
        grid=(b, s // ts),
        in_specs=[
            pl.BlockSpec((ts, slot_t.shape[1]), lambda bi, i: (i, 0)),
            pl.BlockSpec((e, 1, cap, d), lambda bi, i: (0, bi, 0, 0)),
            pl.BlockSpec((1, ts, d), lambda bi, i: (bi, i, 0)),
            pl.BlockSpec(g3.shape, const),
            pl.BlockSpec(b3.shape, const),
        ],
        out_specs=pl.BlockSpec((1, ts, d), lambda bi, i: (bi, i, 0)),
        out_shape=jax.ShapeDtypeStruct((b, s, d), F32),
        compiler_params=_cparams(("parallel", "parallel")),
        name="combine",
    )(slot_t, y4, h2, g3, b3)


def _extend_weights(w_in, w_uq, w_uk, w_uv):
    half = MLA_ROPE_DIM // 2
    d = w_in.shape[0]
    kr = w_in[:, SPLIT_KV:SPLIT_KR]
    z = lambda n: jnp.zeros((d, n), w_in.dtype)
    kr_full = jnp.concatenate([z(MLA_NOPE_DIM), kr, z(HEAD_PAD - MLA_QK_DIM)], axis=1)
    kr_swap = jnp.concatenate([z(MLA_NOPE_DIM), -kr[:, half:], kr[:, :half], z(HEAD_PAD - MLA_QK_DIM)], axis=1)
    w_in_ext = jnp.concatenate([w_in[:, :SPLIT_KV], kr_full, kr_swap, w_in[:, SPLIT_KR:]], axis=1)

    wq = w_uq.reshape(MLA_Q_RANK, MLA_HEADS, MLA_QK_DIM)
    zq = lambda n: jnp.zeros((MLA_Q_RANK, MLA_HEADS, n), w_uq.dtype)
    wq_full = jnp.concatenate([wq, zq(HEAD_PAD - MLA_QK_DIM)], axis=2)
    wq_swap = jnp.concatenate([zq(MLA_NOPE_DIM), -wq[:, :, MLA_NOPE_DIM + half:],
                               wq[:, :, MLA_NOPE_DIM:MLA_NOPE_DIM + half], zq(HEAD_PAD - MLA_QK_DIM)], axis=2)
    wq_ext = jnp.concatenate([wq_full.reshape(MLA_Q_RANK, QK_PAD), wq_swap.reshape(MLA_Q_RANK, QK_PAD)], axis=1)

    wk = w_uk.reshape(MLA_KV_RANK, MLA_HEADS, MLA_NOPE_DIM)
    wk_full = jnp.concatenate([wk, jnp.zeros((MLA_KV_RANK, MLA_HEADS, HEAD_PAD - MLA_NOPE_DIM), w_uk.dtype)], axis=2)
    wkv_ext = jnp.concatenate([wk_full.reshape(MLA_KV_RANK, QK_PAD), w_uv], axis=1)
    return w_in_ext.astype(BF16), wq_ext.astype(BF16), wkv_ext.astype(BF16)


def _rope_freq_lanes():
    inv_freq = ROPE_BASE ** (-jnp.arange(0, MLA_ROPE_DIM, 2, dtype=F32) / MLA_ROPE_DIM)
    z = lambda n: jnp.zeros((n,), F32)
    return jnp.concatenate([z(MLA_NOPE_DIM), inv_freq, inv_freq, z(HEAD_PAD - MLA_QK_DIM)])[None, :]


def kernel(x, mem, positions, w_in, q_norm_g, w_uq, kv_norm_g, w_uk, w_uv, conv_w, conv_b, conv_ln_g, conv_ln_b,
           w_o, ln1_g, ln1_b, xa_w_q, xa_w_k, xa_w_v, xa_w_o, ln2_g, ln2_b, w_router, w_gate, w_up, w_down,
           ln3_g, ln3_b):
    assert w_in.shape[0] == DEPTH == 1
    b, s, d = x.shape
    cap = CAPACITY_FACTOR * s // N_EXPERTS
    tm = 512

    w_in_ext, wq_ext, wkv_ext = _extend_weights(w_in[0], w_uq[0], w_uk[0], w_uv[0])
    posf = positions.astype(F32)[..., None]
    q, k, v, u_pre = _front(x, posf, w_in_ext, q_norm_g, wq_ext, kv_norm_g, wkv_ext, _rope_freq_lanes(), tm)
    attn = _attention(q, k, v, tm)
    h1 = _mix_out(u_pre, attn, x, conv_w[0], conv_b, conv_ln_g, conv_ln_b, w_o[0].astype(BF16), ln1_g, ln1_b, tm)

    kx, vx = _mem_kv(mem, xa_w_k[0].astype(BF16), xa_w_v[0].astype(BF16))
    h2, h2b, aff = _xattn(h1, xa_w_q[0].astype(BF16), kx, vx, xa_w_o[0].astype(BF16), ln2_g, ln2_b,
                          w_router[0].T, tm)

    slot, slot_t = _topk(aff.reshape(b * N_EXPERTS, s), cap)
    xs, gate = _dispatch(h2b, slot.reshape(b, N_EXPERTS, 1, s), aff.reshape(b, N_EXPERTS, 1, s), cap)
    y = _experts(xs.reshape(N_EXPERTS, b * cap, d), gate.reshape(N_EXPERTS, b * cap, 1),
                 w_gate[0], w_up[0], w_down[0], 512)
    return _combine(slot_t, y.reshape(N_EXPERTS, b, cap, d), h2, ln3_g, ln3_b, cap, tm)
```

```python
import functools
import math

import jax
import jax.numpy as jnp
from jax import lax
from jax.experimental import pallas as pl
from jax.experimental.pallas import tpu as pltpu

F32 = jnp.float32
BF16 = jnp.bfloat16

D_MODEL = 1024
MLA_HEADS = 8
MLA_NOPE_DIM = 64
MLA_ROPE_DIM = 32
MLA_QK_DIM = MLA_NOPE_DIM + MLA_ROPE_DIM
MLA_V_DIM = 64
MLA_Q_RANK = 256
MLA_KV_RANK = 128
MLA_WIDTH = MLA_HEADS * MLA_V_DIM
CONV_CH = D_MODEL - MLA_WIDTH
CONV_WIDTH = 31
CONV_PAD = (CONV_WIDTH - 1) // 2
ROPE_BASE = 10000.0
MEM_HEADS = 4
MEM_HEAD_DIM = D_MODEL // MEM_HEADS
N_EXPERTS = 16
EXPERT_FF = 2048
CAPACITY_FACTOR = 2
NORM_EPS = 1e-5
DEPTH = 1
DEEPNORM_ALPHA = (2.0 * DEPTH) ** 0.25
SPLIT_Q = MLA_Q_RANK
SPLIT_KV = SPLIT_Q + MLA_KV_RANK
SPLIT_KR = SPLIT_KV + MLA_ROPE_DIM

HEAD_PAD = 128
QK_PAD = MLA_HEADS * HEAD_PAD
OFF_CQ = 0
OFF_CKV = OFF_CQ + MLA_Q_RANK
OFF_KR = OFF_CKV + MLA_KV_RANK
OFF_KRS = OFF_KR + HEAD_PAD
OFF_A = OFF_KRS + HEAD_PAD
OFF_G = OFF_A + CONV_CH
IN_EXT = OFF_G + CONV_CH

SUBLANES = 8
HALO = 16
VMEM_LIMIT = 56 * 1024 * 1024


def _cparams(sem):
    return pltpu.CompilerParams(dimension_semantics=sem, vmem_limit_bytes=VMEM_LIMIT)


def _layer_norm(v, g, b):
    mu = jnp.mean(v, axis=-1, keepdims=True)
    d = v - mu
    var = jnp.mean(d * d, axis=-1, keepdims=True)
    return d * lax.rsqrt(var + NORM_EPS) * g + b


def _rms_norm(v, g):
    return v * lax.rsqrt(jnp.mean(v * v, axis=-1, keepdims=True) + NORM_EPS) * g


def _sigmoid(v):
    return 1.0 / (1.0 + jnp.exp(-v))


def _dot(a, b):
    return jnp.dot(a, b, preferred_element_type=F32)


def _dot_nt(a, b, precision=None):
    return lax.dot_general(a, b, (((1,), (1,)), ((), ())), preferred_element_type=F32, precision=precision)


def _front_kernel(x_ref, pos_ref, win_ref, qg_ref, wq_ref, kvg_ref, wkv_ref, invf_ref, vone_ref,
                  q_out, k_out, v_out, u_out):
    x = x_ref[0].astype(BF16)
    hc = _dot(x, win_ref[...])
    cqn = _rms_norm(hc[:, OFF_CQ:OFF_CKV], qg_ref[...])
    qq = _dot(cqn.astype(BF16), wq_ref[...])
    ckvn = _rms_norm(hc[:, OFF_CKV:OFF_KR], kvg_ref[...])
    kv = _dot(ckvn.astype(BF16), wkv_ref[...])
    ang = pos_ref[0] * invf_ref[...]
    cos = jnp.cos(ang)
    sin = jnp.sin(ang)
    krot = hc[:, OFF_KR:OFF_KRS] * cos + hc[:, OFF_KRS:OFF_A] * sin
    scale = math.log2(math.e) / math.sqrt(MLA_QK_DIM)
    for h in range(MLA_HEADS):
        lo, hi = h * HEAD_PAD, (h + 1) * HEAD_PAD
        qh = (qq[:, lo:hi] * cos + qq[:, QK_PAD + lo:QK_PAD + hi] * sin) * scale
        q_out[0, :, lo:hi] = qh.astype(BF16)
        k_out[0, :, lo:hi] = (kv[:, lo:hi] + krot).astype(BF16)
    v_out[0] = (kv[:, QK_PAD:] + vone_ref[...]).astype(BF16)
    u_out[0] = hc[:, OFF_A:OFF_G] * _sigmoid(hc[:, OFF_G:IN_EXT])


def _front(x, posf, w_in_ext, qg, wq_ext, kvg, wkv_ext, invf, vone, tm):
    b, s, d = x.shape
    const = lambda bi, i: (0, 0)
    return pl.pallas_call(
        _front_kernel,
        grid=(b, s // tm),
        in_specs=[
            pl.BlockSpec((1, tm, d), lambda bi, i: (bi, i, 0)),
            pl.BlockSpec((1, tm, 1), lambda bi, i: (bi, i, 0)),
            pl.BlockSpec(w_in_ext.shape, const),
            pl.BlockSpec(qg.shape, const),
            pl.BlockSpec(wq_ext.shape, const),
            pl.BlockSpec(kvg.shape, const),
            pl.BlockSpec(wkv_ext.shape, const),
            pl.BlockSpec(invf.shape, const),
            pl.BlockSpec(vone.shape, const),
        ],
        out_specs=[
            pl.BlockSpec((1, tm, QK_PAD), lambda bi, i: (bi, i, 0)),
            pl.BlockSpec((1, tm, QK_PAD), lambda bi, i: (bi, i, 0)),
            pl.BlockSpec((1, tm, QK_PAD), lambda bi, i: (bi, i, 0)),
            pl.BlockSpec((1, tm, CONV_CH), lambda bi, i: (bi, i, 0)),
        ],
        out_shape=[
            jax.ShapeDtypeStruct((b, s, QK_PAD), BF16),
            jax.ShapeDtypeStruct((b, s, QK_PAD), BF16),
            jax.ShapeDtypeStruct((b, s, QK_PAD), BF16),
            jax.ShapeDtypeStruct((b, s, CONV_CH), F32),
        ],
        compiler_params=_cparams(("parallel", "parallel")),
        name="front",
    )(x, posf, w_in_ext, qg, wq_ext, kvg, wkv_ext, invf, vone)


def _attn_kernel(q_ref, k_ref, v_ref, o_ref):
    for h in range(MLA_HEADS):
        lo, hi = h * HEAD_PAD, (h + 1) * HEAD_PAD
        sc = _dot_nt(q_ref[0, :, lo:hi], k_ref[0, :, lo:hi])
        p = jnp.exp2(sc - jnp.max(sc, axis=-1, keepdims=True))
        pv = _dot(p.astype(BF16), v_ref[0, :, lo:hi])
        o = pv[:, :MLA_V_DIM] / pv[:, MLA_V_DIM:MLA_V_DIM + 1]
        o_ref[0, :, h * MLA_V_DIM:(h + 1) * MLA_V_DIM] = o.astype(BF16)


def _attention(q, k, v, tq):
    b, s, _ = q.shape
    return pl.pallas_call(
        _attn_kernel,
        grid=(b, s // tq),
        in_specs=[
            pl.BlockSpec((1, tq, QK_PAD), lambda bi, i: (bi, i, 0)),
            pl.BlockSpec((1, s, QK_PAD), lambda bi, i: (bi, 0, 0)),
            pl.BlockSpec((1, s, QK_PAD), lambda bi, i: (bi, 0, 0)),
        ],
        out_specs=pl.BlockSpec((1, tq, MLA_WIDTH), lambda bi, i: (bi, i, 0)),
        out_shape=jax.ShapeDtypeStruct((b, s, MLA_WIDTH), BF16),
        compiler_params=_cparams(("parallel", "parallel")),
        name="attn",
    )(q, k, v)


def _mem_kv_kernel(mem_ref, wk_ref, wv_ref, k_out, v_out):
    m = mem_ref[0].astype(BF16)
    k_out[0] = _dot(m, wk_ref[...]).astype(BF16)
    v_out[0] = _dot(m, wv_ref[...]).astype(BF16)


def _mem_kv(mem, wk, wv):
    b, m, d = mem.shape
    const = lambda bi: (0, 0)
    return pl.pallas_call(
        _mem_kv_kernel,
        grid=(b,),
        in_specs=[
            pl.BlockSpec((1, m, d), lambda bi: (bi, 0, 0)),
            pl.BlockSpec(wk.shape, const),
            pl.BlockSpec(wv.shape, const),
        ],
        out_specs=[pl.BlockSpec((1, m, d), lambda bi: (bi, 0, 0))] * 2,
        out_shape=[jax.ShapeDtypeStruct((b, m, d), BF16)] * 2,
        compiler_params=_cparams(("parallel",)),
        name="mem_kv",
    )(mem, wk, wv)


CONV_SUB = 64


def _mix_out_kernel(ucur_ref, uprev_ref, unext_ref, attn_ref, x_ref, cw_ref, cb_ref, cg_ref, cbeta_ref,
                    wo_ref, g1_ref, b1_ref, h1_out, win_ref, shift_ref, u_scr):
    tm = ucur_ref.shape[1]
    i = pl.program_id(1)
    last = pl.num_programs(1) - 1
    win_ref[0:HALO, :] = jnp.where(i > 0, uprev_ref[0], 0.0)
    win_ref[HALO:HALO + tm, :] = ucur_ref[0]
    win_ref[HALO + tm:2 * HALO + tm, :] = jnp.where(i < last, unext_ref[0], 0.0)
    span = tm + 2 * HALO - SUBLANES
    shift_ref[0] = win_ref[...]
    for j in range(1, SUBLANES):
        shift_ref[j, 0:span, :] = win_ref[j:j + span, :]
    for r in range(tm // CONV_SUB):
        acc = jnp.broadcast_to(cb_ref[...], (CONV_SUB, CONV_CH))
        for t in range(CONV_WIDTH):
            off = HALO - CONV_PAD + t
            row = r * CONV_SUB + (off // SUBLANES) * SUBLANES
            acc = acc + shift_ref[off % SUBLANES, row:row + CONV_SUB, :] * cw_ref[t:t + 1, :]
        y = _layer_norm(acc, cg_ref[...], cbeta_ref[...])
        u_scr[r * CONV_SUB:(r + 1) * CONV_SUB, :] = (y * _sigmoid(y)).astype(BF16)
    mix = _dot(attn_ref[0], wo_ref[0:MLA_WIDTH, :]) + _dot(u_scr[...], wo_ref[MLA_WIDTH:, :])
    h1_out[0] = _layer_norm(DEEPNORM_ALPHA * x_ref[0] + mix, g1_ref[...], b1_ref[...])


def _mix_out(u_pre, attn, x, conv_w, conv_b, conv_g, conv_beta, w_o, g1, b1, tm):
    b, s, d = x.shape
    nh = tm // HALO
    const = lambda bi, i: (0, 0)
    return pl.pallas_call(
        _mix_out_kernel,
        grid=(b, s // tm),
        in_specs=[
            pl.BlockSpec((1, tm, CONV_CH), lambda bi, i: (bi, i, 0)),
            pl.BlockSpec((1, HALO, CONV_CH), lambda bi, i: (bi, jnp.maximum(i * nh - 1, 0), 0)),
            pl.BlockSpec((1, HALO, CONV_CH), lambda bi, i: (bi, jnp.minimum((i + 1) * nh, s // HALO - 1), 0)),
            pl.BlockSpec((1, tm, MLA_WIDTH), lambda bi, i: (bi, i, 0)),
            pl.BlockSpec((1, tm, d), lambda bi, i: (bi, i, 0)),
            pl.BlockSpec(conv_w.shape, const),
            pl.BlockSpec(conv_b.shape, const),
            pl.BlockSpec(conv_g.shape, const),
            pl.BlockSpec(conv_beta.shape, const),
            pl.BlockSpec(w_o.shape, const),
            pl.BlockSpec(g1.shape, const),
            pl.BlockSpec(b1.shape, const),
        ],
        out_specs=pl.BlockSpec((1, tm, d), lambda bi, i: (bi, i, 0)),
        out_shape=jax.ShapeDtypeStruct((b, s, d), F32),
        scratch_shapes=[
            pltpu.VMEM((tm + 2 * HALO, CONV_CH), F32),
            pltpu.VMEM((SUBLANES, tm + 2 * HALO, CONV_CH), F32),
            pltpu.VMEM((tm, CONV_CH), BF16),
        ],
        compiler_params=_cparams(("parallel", "parallel")),
        name="mix_out",
    )(u_pre, u_pre, u_pre, attn, x, conv_w, conv_b, conv_g, conv_beta, w_o, g1, b1)


def _xattn_kernel(h1_ref, wq_ref, kx_ref, vx_ref, wo_ref, g2_ref, b2_ref, wr_ref,
                  h2_out, h2b_out, aff_out, o_scr):
    h1 = h1_ref[0]
    q = (_dot(h1.astype(BF16), wq_ref[...]) * (1.0 / math.sqrt(MEM_HEAD_DIM))).astype(BF16)
    for h in range(MEM_HEADS):
        lo, hi = h * MEM_HEAD_DIM, (h + 1) * MEM_HEAD_DIM
        sc = _dot_nt(q[:, lo:hi], kx_ref[0, :, lo:hi])
        m = jnp.max(sc, axis=-1, keepdims=True)
        p = jnp.exp(sc - m)
        l = jnp.sum(p, axis=-1, keepdims=True)
        o_scr[:, lo:hi] = (_dot(p.astype(BF16), vx_ref[0, :, lo:hi]) / l).astype(BF16)
    xa = _dot(o_scr[...], wo_ref[...])
    h2 = _layer_norm(DEEPNORM_ALPHA * h1 + xa, g2_ref[...], b2_ref[...])
    h2_out[0] = h2
    h2b_out[0] = h2.astype(BF16)
    logits = _dot_nt(wr_ref[...], h2, precision=lax.Precision.HIGHEST)
    e = jnp.exp(logits - jnp.max(logits, axis=0, keepdims=True))
    aff_out[0] = e / jnp.sum(e, axis=0, keepdims=True)


def _xattn(h1, wq, kx, vx, wo, g2, b2, w_router_t, tm):
    b, s, d = h1.shape
    m = kx.shape[1]
    const = lambda bi, i: (0, 0)
    return pl.pallas_call(
        _xattn_kernel,
        grid=(b, s // tm),
        in_specs=[
            pl.BlockSpec((1, tm, d), lambda bi, i: (bi, i, 0)),
            pl.BlockSpec(wq.shape, const),
            pl.BlockSpec((1, m, d), lambda bi, i: (bi, 0, 0)),
            pl.BlockSpec((1, m, d), lambda bi, i: (bi, 0, 0)),
            pl.BlockSpec(wo.shape, const),
            pl.BlockSpec(g2.shape, const),
            pl.BlockSpec(b2.shape, const),
            pl.BlockSpec(w_router_t.shape, const),
        ],
        out_specs=[
            pl.BlockSpec((1, tm, d), lambda bi, i: (bi, i, 0)),
            pl.BlockSpec((1, tm, d), lambda bi, i: (bi, i, 0)),
            pl.BlockSpec((1, N_EXPERTS, tm), lambda bi, i: (bi, 0, i)),
        ],
        out_shape=[
            jax.ShapeDtypeStruct((b, s, d), F32),
            jax.ShapeDtypeStruct((b, s, d), BF16),
            jax.ShapeDtypeStruct((b, N_EXPERTS, s), F32),
        ],
        scratch_shapes=[pltpu.VMEM((tm, d), BF16)],
        compiler_params=_cparams(("parallel", "parallel")),
        name="xattn",
    )(h1, wq, kx, vx, wo, g2, b2, w_router_t)


def _topk_kernel(aff_ref, slot_out, slot_t_out, *, cap):
    aff = aff_ref[...]
    rows, s = aff.shape
    capf = jnp.float32(cap)

    def not_done(carry):
        return carry[2] > 0

    def bisect(carry):
        lo, hi, _ = carry
        mid = 0.5 * (lo + hi)
        take = jnp.sum(jnp.where(aff >= mid, 1.0, 0.0), axis=1, keepdims=True) >= capf
        lo = jnp.where(take, mid, lo)
        hi = jnp.where(take, hi, mid)
        smallest_in = jnp.min(jnp.where(aff >= lo, aff, jnp.inf), axis=1, keepdims=True)
        largest_in = jnp.max(jnp.where(aff < hi, aff, -jnp.inf), axis=1, keepdims=True)
        open_rows = jnp.sum(jnp.where(smallest_in == largest_in, 0.0, 1.0))
        return lo, hi, open_rows.astype(jnp.int32)

    _, hi, _ = lax.while_loop(not_done, bisect,
                              (jnp.zeros((rows, 1), F32), jnp.full((rows, 1), 2.0, F32), jnp.int32(1)))
    th = jnp.max(jnp.where(aff < hi, aff, -jnp.inf), axis=1, keepdims=True)
    gt = aff > th
    eq = aff == th
    n_gt = jnp.sum(jnp.where(gt, 1.0, 0.0), axis=1, keepdims=True)
    tri = jnp.where(lax.broadcasted_iota(jnp.int32, (s, s), 0) < lax.broadcasted_iota(jnp.int32, (s, s), 1),
                    1.0, 0.0).astype(BF16)
    tie_rank = _dot(jnp.where(eq, 1.0, 0.0).astype(BF16), tri)
    sel = jnp.logical_or(gt, jnp.logical_and(eq, tie_rank < (capf - n_gt)))
    pos = _dot(jnp.where(sel, 1.0, 0.0).astype(BF16), tri)
    slot = jnp.where(sel, pos, -1.0)
    slot_out[...] = slot
    slot_t_out[...] = slot.T


def _topk(aff2d, cap):
    rows, s = aff2d.shape
    return pl.pallas_call(
        functools.partial(_topk_kernel, cap=cap),
        out_shape=[
            jax.ShapeDtypeStruct((rows, s), F32),
            jax.ShapeDtypeStruct((s, rows), F32),
        ],
        compiler_params=pltpu.CompilerParams(vmem_limit_bytes=VMEM_LIMIT),
        name="topk",
    )(aff2d)


def _dispatch_kernel(h2b_ref, slot_ref, aff_ref, xs_out, gate_out, *, cap):
    slot = slot_ref[0, 0]
    s = slot.shape[1]
    hit = slot == lax.broadcasted_iota(jnp.int32, (cap, s), 0).astype(F32)
    xs_out[0, 0] = _dot(jnp.where(hit, 1.0, 0.0).astype(BF16), h2b_ref[0]).astype(BF16)
    gate_out[0, 0] = jnp.sum(jnp.where(hit, aff_ref[0, 0], 0.0), axis=1, keepdims=True)


def _dispatch(h2b, slot4, aff4, cap):
    b, s, d = h2b.shape
    e = slot4.shape[1]
    return pl.pallas_call(
        functools.partial(_dispatch_kernel, cap=cap),
        grid=(b, e),
        in_specs=[
            pl.BlockSpec((1, s, d), lambda bi, ei: (bi, 0, 0)),
            pl.BlockSpec((1, 1, 1, s), lambda bi, ei: (bi, ei, 0, 0)),
            pl.BlockSpec((1, 1, 1, s), lambda bi, ei: (bi, ei, 0, 0)),
        ],
        out_specs=[
            pl.BlockSpec((1, 1, cap, d), lambda bi, ei: (ei, bi, 0, 0)),
            pl.BlockSpec((1, 1, cap, 1), lambda bi, ei: (ei, bi, 0, 0)),
        ],
        out_shape=[
            jax.ShapeDtypeStruct((e, b, cap, d), BF16),
            jax.ShapeDtypeStruct((e, b, cap, 1), F32),
        ],
        compiler_params=_cparams(("parallel", "parallel")),
        name="dispatch",
    )(h2b, slot4, aff4)


FFN_ROWS = 512
FFN_CHUNK = 512


def _experts_kernel(xs_ref, gate_ref, wg_ref, wu_ref, wd_ref, y_out, wg_s, wu_s, wd_s, hid_s):
    e = pl.program_id(0)
    f = pl.program_id(1)
    n_chunks = wg_s.shape[1]

    def stage():
        slot = e % 2
        wg_s[slot, f] = wg_ref[0].astype(BF16)
        wu_s[slot, f] = wu_ref[0].astype(BF16)
        wd_s[slot, f] = wd_ref[0].astype(BF16)

    @pl.when(e == 0)
    def _():
        stage()
        y_out[0] = jnp.zeros(y_out.shape[1:], y_out.dtype)

    @pl.when(e > 0)
    def _():
        stage()
        slot = (e - 1) % 2
        xs = xs_ref[0]
        for c in range(n_chunks):
            g = _dot(xs, wg_s[slot, c])
            u = _dot(xs, wu_s[slot, c])
            hid_s[:, c * FFN_CHUNK:(c + 1) * FFN_CHUNK] = (g * _sigmoid(g) * u).astype(BF16)
        y = _dot(hid_s[:, 0:FFN_CHUNK], wd_s[slot, 0])
        for c in range(1, n_chunks):
            y = y + _dot(hid_s[:, c * FFN_CHUNK:(c + 1) * FFN_CHUNK], wd_s[slot, c])
        y_out[0] = (y * gate_ref[0]).astype(BF16)


def _experts(xs, gate, w_gate, w_up, w_down):
    e, n, d = xs.shape
    ff = w_gate.shape[2]
    n_chunks = ff // FFN_CHUNK
    assert n // FFN_ROWS == n_chunks
    prev = lambda ei, fi: (jnp.maximum(ei - 1, 0), fi, 0)
    cur = lambda ei: jnp.minimum(ei, e - 1)
    return pl.pallas_call(
        _experts_kernel,
        grid=(e + 1, n_chunks),
        in_specs=[
            pl.BlockSpec((1, FFN_ROWS, d), prev),
            pl.BlockSpec((1, FFN_ROWS, 1), prev),
            pl.BlockSpec((1, d, FFN_CHUNK), lambda ei, fi: (cur(ei), 0, fi)),
            pl.BlockSpec((1, d, FFN_CHUNK), lambda ei, fi: (cur(ei), 0, fi)),
            pl.BlockSpec((1, FFN_CHUNK, d), lambda ei, fi: (cur(ei), fi, 0)),
        ],
        out_specs=pl.BlockSpec((1, FFN_ROWS, d), lambda ei, fi: (jnp.where(ei == 0, e, ei - 1), fi, 0)),
        out_shape=jax.ShapeDtypeStruct((e + 1, n, d), BF16),
        scratch_shapes=[
            pltpu.VMEM((2, n_chunks, d, FFN_CHUNK), BF16),
            pltpu.VMEM((2, n_chunks, d, FFN_CHUNK), BF16),
            pltpu.VMEM((2, n_chunks, FFN_CHUNK, d), BF16),
            pltpu.VMEM((FFN_ROWS, ff), BF16),
        ],
        compiler_params=_cparams(("arbitrary", "arbitrary")),
        name="experts",
    )(xs, gate, w_gate, w_up, w_down)


def _combine_kernel(slot_t_ref, y_ref, h2_ref, g3_ref, b3_ref, out_ref, *, cap):
    bi = pl.program_id(0)
    n_e = y_ref.shape[0]
    rows = slot_t_ref.shape[1]
    width = n_e * cap
    col = lax.broadcasted_iota(jnp.int32, (rows, width), 1) // cap
    spread = jnp.where(lax.broadcasted_iota(jnp.int32, (rows, width), 0) == bi * n_e + col, 1.0, 0.0).astype(BF16)
    slot_wide = _dot(slot_t_ref[...].astype(BF16), spread)
    lane_c = (lax.broadcasted_iota(jnp.int32, (1, width), 1) % cap).astype(F32)
    onehot = jnp.where(slot_wide == lane_c, 1.0, 0.0).astype(BF16)
    ff = _dot(onehot, y_ref[:, 0].reshape(width, y_ref.shape[3]))
    out_ref[0] = _layer_norm(DEEPNORM_ALPHA * h2_ref[0] + ff, g3_ref[...], b3_ref[...])


def _combine(slot_t, y4, h2, g3, b3, cap, ts):
    b, s, d = h2.shape
    e = y4.shape[0] - 1
    const = lambda bi, i: (0, 0)
    return pl.pallas_call(
        functools.partial(_combine_kernel, cap=cap),
        grid=(b, s // ts),
        in_specs=[
            pl.BlockSpec((ts, slot_t.shape[1]), lambda bi, i: (i, 0)),
            pl.BlockSpec((e, 1, cap, d), lambda bi, i: (0, bi, 0, 0)),
            pl.BlockSpec((1, ts, d), lambda bi, i: (bi, i, 0)),
            pl.BlockSpec(g3.shape, const),
            pl.BlockSpec(b3.shape, const),
        ],
        out_specs=pl.BlockSpec((1, ts, d), lambda bi, i: (bi, i, 0)),
        out_shape=jax.ShapeDtypeStruct((b, s, d), F32),
        compiler_params=_cparams(("parallel", "parallel")),
        name="combine",
    )(slot_t, y4, h2, g3, b3)


def _extend_weights(w_in, w_uq, w_uk, w_uv):
    half = MLA_ROPE_DIM // 2
    d = w_in.shape[0]
    kr = w_in[:, SPLIT_KV:SPLIT_KR]
    z = lambda n: jnp.zeros((d, n), w_in.dtype)
    kr_full = jnp.concatenate([z(MLA_NOPE_DIM), kr, z(HEAD_PAD - MLA_QK_DIM)], axis=1)
    kr_swap = jnp.concatenate([z(MLA_NOPE_DIM), -kr[:, half:], kr[:, :half], z(HEAD_PAD - MLA_QK_DIM)], axis=1)
    w_in_ext = jnp.concatenate([w_in[:, :SPLIT_KV], kr_full, kr_swap, w_in[:, SPLIT_KR:]], axis=1)

    wq = w_uq.reshape(MLA_Q_RANK, MLA_HEADS, MLA_QK_DIM)
    zq = lambda n: jnp.zeros((MLA_Q_RANK, MLA_HEADS, n), w_uq.dtype)
    wq_full = jnp.concatenate([wq, zq(HEAD_PAD - MLA_QK_DIM)], axis=2)
    wq_swap = jnp.concatenate([zq(MLA_NOPE_DIM), -wq[:, :, MLA_NOPE_DIM + half:],
                               wq[:, :, MLA_NOPE_DIM:MLA_NOPE_DIM + half], zq(HEAD_PAD - MLA_QK_DIM)], axis=2)
    wq_ext = jnp.concatenate([wq_full.reshape(MLA_Q_RANK, QK_PAD), wq_swap.reshape(MLA_Q_RANK, QK_PAD)], axis=1)

    wk = w_uk.reshape(MLA_KV_RANK, MLA_HEADS, MLA_NOPE_DIM)
    wk_full = jnp.concatenate([wk, jnp.zeros((MLA_KV_RANK, MLA_HEADS, HEAD_PAD - MLA_NOPE_DIM), w_uk.dtype)], axis=2)
    wv = w_uv.reshape(MLA_KV_RANK, MLA_HEADS, MLA_V_DIM)
    wv_full = jnp.concatenate([wv, jnp.zeros((MLA_KV_RANK, MLA_HEADS, HEAD_PAD - MLA_V_DIM), w_uv.dtype)], axis=2)
    wkv_ext = jnp.concatenate([wk_full.reshape(MLA_KV_RANK, QK_PAD), wv_full.reshape(MLA_KV_RANK, QK_PAD)], axis=1)
    return w_in_ext.astype(BF16), wq_ext.astype(BF16), wkv_ext.astype(BF16)


def _rope_freq_lanes():
    inv_freq = ROPE_BASE ** (-jnp.arange(0, MLA_ROPE_DIM, 2, dtype=F32) / MLA_ROPE_DIM)
    z = lambda n: jnp.zeros((n,), F32)
    return jnp.concatenate([z(MLA_NOPE_DIM), inv_freq, inv_freq, z(HEAD_PAD - MLA_QK_DIM)])[None, :]


def _value_one_lanes():
    one_hot = (jnp.arange(HEAD_PAD) == MLA_V_DIM).astype(F32)
    return jnp.tile(one_hot, MLA_HEADS)[None, :]


def kernel(x, mem, positions, w_in, q_norm_g, w_uq, kv_norm_g, w_uk, w_uv, conv_w, conv_b, conv_ln_g, conv_ln_b,
           w_o, ln1_g, ln1_b, xa_w_q, xa_w_k, xa_w_v, xa_w_o, ln2_g, ln2_b, w_router, w_gate, w_up, w_down,
           ln3_g, ln3_b):
    assert w_in.shape[0] == DEPTH == 1
    b, s, d = x.shape
    cap = CAPACITY_FACTOR * s // N_EXPERTS
    tm = 512

    w_in_ext, wq_ext, wkv_ext = _extend_weights(w_in[0], w_uq[0], w_uk[0], w_uv[0])
    posf = positions.astype(F32)[..., None]
    q, k, v, u_pre = _front(x, posf, w_in_ext, q_norm_g, wq_ext, kv_norm_g, wkv_ext, _rope_freq_lanes(),
                             _value_one_lanes(), tm)
    attn = _attention(q, k, v, tm)
    h1 = _mix_out(u_pre, attn, x, conv_w[0], conv_b, conv_ln_g, conv_ln_b, w_o[0].astype(BF16), ln1_g, ln1_b, tm)

    kx, vx = _mem_kv(mem, xa_w_k[0].astype(BF16), xa_w_v[0].astype(BF16))
    h2, h2b, aff = _xattn(h1, xa_w_q[0].astype(BF16), kx, vx, xa_w_o[0].astype(BF16), ln2_g, ln2_b,
                          w_router[0].T, tm)

    slot, slot_t = _topk(aff.reshape(b * N_EXPERTS, s), cap)
    xs, gate = _dispatch(h2b, slot.reshape(b, N_EXPERTS, 1, s), aff.reshape(b, N_EXPERTS, 1, s), cap)
    y = _experts(xs.reshape(N_EXPERTS, b * cap, d), gate.reshape(N_EXPERTS, b * cap, 1),
                 w_gate[0], w_up[0], w_down[0])
    return _combine(slot_t, y.reshape(N_EXPERTS + 1, b, cap, d), h2, ln3_g, ln3_b, cap, tm)
```

```python
import functools
import math

import jax
import jax.numpy as jnp
from jax import lax
from jax.experimental import pallas as pl
from jax.experimental.pallas import tpu as pltpu

F32 = jnp.float32
BF16 = jnp.bfloat16

D_MODEL = 1024
MLA_HEADS = 8
MLA_NOPE_DIM = 64
MLA_ROPE_DIM = 32
MLA_QK_DIM = MLA_NOPE_DIM + MLA_ROPE_DIM
MLA_V_DIM = 64
MLA_Q_RANK = 256
MLA_KV_RANK = 128
MLA_WIDTH = MLA_HEADS * MLA_V_DIM
CONV_CH = D_MODEL - MLA_WIDTH
CONV_WIDTH = 31
CONV_PAD = (CONV_WIDTH - 1) // 2
ROPE_BASE = 10000.0
MEM_HEADS = 4
MEM_HEAD_DIM = D_MODEL // MEM_HEADS
N_EXPERTS = 16
EXPERT_FF = 2048
CAPACITY_FACTOR = 2
NORM_EPS = 1e-5
DEPTH = 1
DEEPNORM_ALPHA = (2.0 * DEPTH) ** 0.25
SPLIT_Q = MLA_Q_RANK
SPLIT_KV = SPLIT_Q + MLA_KV_RANK
SPLIT_KR = SPLIT_KV + MLA_ROPE_DIM

HEAD_PAD = 128
QK_PAD = MLA_HEADS * HEAD_PAD
OFF_CQ = 0
OFF_CKV = OFF_CQ + MLA_Q_RANK
OFF_KR = OFF_CKV + MLA_KV_RANK
OFF_KRS = OFF_KR + HEAD_PAD
OFF_A = OFF_KRS + HEAD_PAD
OFF_G = OFF_A + CONV_CH
IN_EXT = OFF_G + CONV_CH

SUBLANES = 8
LANES = 128
BF16_ROWS = 16
ROUTE_CHUNK = 256
ROUTE_WIN = 64
HALO = 16
VMEM_LIMIT = 56 * 1024 * 1024


def _cparams(sem):
    return pltpu.CompilerParams(dimension_semantics=sem, vmem_limit_bytes=VMEM_LIMIT)


def _layer_norm(v, g, b):
    mu = jnp.mean(v, axis=-1, keepdims=True)
    d = v - mu
    var = jnp.mean(d * d, axis=-1, keepdims=True)
    return d * lax.rsqrt(var + NORM_EPS) * g + b


def _rms_norm(v, g):
    return v * lax.rsqrt(jnp.mean(v * v, axis=-1, keepdims=True) + NORM_EPS) * g


def _sigmoid(v):
    return 1.0 / (1.0 + jnp.exp(-v))


def _dot(a, b):
    return jnp.dot(a, b, preferred_element_type=F32)


def _dot_nt(a, b, precision=None):
    return lax.dot_general(a, b, (((1,), (1,)), ((), ())), preferred_element_type=F32, precision=precision)


def _front_kernel(x_ref, pos_ref, win_ref, qg_ref, wq_ref, kvg_ref, wkv_ref, invf_ref, vone_ref,
                  q_out, k_out, v_out, u_out):
    x = x_ref[0].astype(BF16)
    hc = _dot(x, win_ref[...])
    cqn = _rms_norm(hc[:, OFF_CQ:OFF_CKV], qg_ref[...])
    qq = _dot(cqn.astype(BF16), wq_ref[...])
    ckvn = _rms_norm(hc[:, OFF_CKV:OFF_KR], kvg_ref[...])
    kv = _dot(ckvn.astype(BF16), wkv_ref[...])
    ang = pos_ref[0] * invf_ref[...]
    cos = jnp.cos(ang)
    sin = jnp.sin(ang)
    krot = hc[:, OFF_KR:OFF_KRS] * cos + hc[:, OFF_KRS:OFF_A] * sin
    scale = math.log2(math.e) / math.sqrt(MLA_QK_DIM)
    for h in range(MLA_HEADS):
        lo, hi = h * HEAD_PAD, (h + 1) * HEAD_PAD
        qh = (qq[:, lo:hi] * cos + qq[:, QK_PAD + lo:QK_PAD + hi] * sin) * scale
        q_out[0, :, lo:hi] = qh.astype(BF16)
        k_out[0, :, lo:hi] = (kv[:, lo:hi] + krot).astype(BF16)
    v_out[0] = (kv[:, QK_PAD:] + vone_ref[...]).astype(BF16)
    u_out[0] = hc[:, OFF_A:OFF_G] * _sigmoid(hc[:, OFF_G:IN_EXT])


def _front(x, posf, w_in_ext, qg, wq_ext, kvg, wkv_ext, invf, vone, tm):
    b, s, d = x.shape
    const = lambda bi, i: (0, 0)
    return pl.pallas_call(
        _front_kernel,
        grid=(b, s // tm),
        in_specs=[
            pl.BlockSpec((1, tm, d), lambda bi, i: (bi, i, 0)),
            pl.BlockSpec((1, tm, 1), lambda bi, i: (bi, i, 0)),
            pl.BlockSpec(w_in_ext.shape, const),
            pl.BlockSpec(qg.shape, const),
            pl.BlockSpec(wq_ext.shape, const),
            pl.BlockSpec(kvg.shape, const),
            pl.BlockSpec(wkv_ext.shape, const),
            pl.BlockSpec(invf.shape, const),
            pl.BlockSpec(vone.shape, const),
        ],
        out_specs=[
            pl.BlockSpec((1, tm, QK_PAD), lambda bi, i: (bi, i, 0)),
            pl.BlockSpec((1, tm, QK_PAD), lambda bi, i: (bi, i, 0)),
            pl.BlockSpec((1, tm, QK_PAD), lambda bi, i: (bi, i, 0)),
            pl.BlockSpec((1, tm, CONV_CH), lambda bi, i: (bi, i, 0)),
        ],
        out_shape=[
            jax.ShapeDtypeStruct((b, s, QK_PAD), BF16),
            jax.ShapeDtypeStruct((b, s, QK_PAD), BF16),
            jax.ShapeDtypeStruct((b, s, QK_PAD), BF16),
            jax.ShapeDtypeStruct((b, s, CONV_CH), F32),
        ],
        compiler_params=_cparams(("parallel", "parallel")),
        name="front",
    )(x, posf, w_in_ext, qg, wq_ext, kvg, wkv_ext, invf, vone)


def _attn_kernel(q_ref, k_ref, v_ref, o_ref):
    for h in range(MLA_HEADS):
        lo, hi = h * HEAD_PAD, (h + 1) * HEAD_PAD
        sc = _dot_nt(q_ref[0, :, lo:hi], k_ref[0, :, lo:hi])
        p = jnp.exp2(sc - jnp.max(sc, axis=-1, keepdims=True))
        pv = _dot(p.astype(BF16), v_ref[0, :, lo:hi])
        o = pv[:, :MLA_V_DIM] / pv[:, MLA_V_DIM:MLA_V_DIM + 1]
        o_ref[0, :, h * MLA_V_DIM:(h + 1) * MLA_V_DIM] = o.astype(BF16)


def _attention(q, k, v, tq):
    b, s, _ = q.shape
    return pl.pallas_call(
        _attn_kernel,
        grid=(b, s // tq),
        in_specs=[
            pl.BlockSpec((1, tq, QK_PAD), lambda bi, i: (bi, i, 0)),
            pl.BlockSpec((1, s, QK_PAD), lambda bi, i: (bi, 0, 0)),
            pl.BlockSpec((1, s, QK_PAD), lambda bi, i: (bi, 0, 0)),
        ],
        out_specs=pl.BlockSpec((1, tq, MLA_WIDTH), lambda bi, i: (bi, i, 0)),
        out_shape=jax.ShapeDtypeStruct((b, s, MLA_WIDTH), BF16),
        compiler_params=_cparams(("parallel", "parallel")),
        name="attn",
    )(q, k, v)


def _mem_kv_kernel(mem_ref, wk_ref, wv_ref, k_out, v_out):
    m = mem_ref[0].astype(BF16)
    k_out[0] = _dot(m, wk_ref[...]).astype(BF16)
    v_out[0] = _dot(m, wv_ref[...]).astype(BF16)


def _mem_kv(mem, wk, wv):
    b, m, d = mem.shape
    const = lambda bi: (0, 0)
    return pl.pallas_call(
        _mem_kv_kernel,
        grid=(b,),
        in_specs=[
            pl.BlockSpec((1, m, d), lambda bi: (bi, 0, 0)),
            pl.BlockSpec(wk.shape, const),
            pl.BlockSpec(wv.shape, const),
        ],
        out_specs=[pl.BlockSpec((1, m, d), lambda bi: (bi, 0, 0))] * 2,
        out_shape=[jax.ShapeDtypeStruct((b, m, d), BF16)] * 2,
        compiler_params=_cparams(("parallel",)),
        name="mem_kv",
    )(mem, wk, wv)


CONV_SUB = 64


def _mix_out_kernel(ucur_ref, uprev_ref, unext_ref, attn_ref, x_ref, cw_ref, cb_ref, cg_ref, cbeta_ref,
                    wo_ref, g1_ref, b1_ref, h1_out, win_ref, shift_ref, u_scr):
    tm = ucur_ref.shape[1]
    i = pl.program_id(1)
    last = pl.num_programs(1) - 1
    win_ref[0:HALO, :] = jnp.where(i > 0, uprev_ref[0], 0.0)
    win_ref[HALO:HALO + tm, :] = ucur_ref[0]
    win_ref[HALO + tm:2 * HALO + tm, :] = jnp.where(i < last, unext_ref[0], 0.0)
    span = tm + 2 * HALO - SUBLANES
    shift_ref[0] = win_ref[...]
    for j in range(1, SUBLANES):
        shift_ref[j, 0:span, :] = win_ref[j:j + span, :]
    for r in range(tm // CONV_SUB):
        acc = jnp.broadcast_to(cb_ref[...], (CONV_SUB, CONV_CH))
        for t in range(CONV_WIDTH):
            off = HALO - CONV_PAD + t
            row = r * CONV_SUB + (off // SUBLANES) * SUBLANES
            acc = acc + shift_ref[off % SUBLANES, row:row + CONV_SUB, :] * cw_ref[t:t + 1, :]
        y = _layer_norm(acc, cg_ref[...], cbeta_ref[...])
        u_scr[r * CONV_SUB:(r + 1) * CONV_SUB, :] = (y * _sigmoid(y)).astype(BF16)
    mix = _dot(attn_ref[0], wo_ref[0:MLA_WIDTH, :]) + _dot(u_scr[...], wo_ref[MLA_WIDTH:, :])
    h1_out[0] = _layer_norm(DEEPNORM_ALPHA * x_ref[0] + mix, g1_ref[...], b1_ref[...])


def _mix_out(u_pre, attn, x, conv_w, conv_b, conv_g, conv_beta, w_o, g1, b1, tm):
    b, s, d = x.shape
    nh = tm // HALO
    const = lambda bi, i: (0, 0)
    return pl.pallas_call(
        _mix_out_kernel,
        grid=(b, s // tm),
        in_specs=[
            pl.BlockSpec((1, tm, CONV_CH), lambda bi, i: (bi, i, 0)),
            pl.BlockSpec((1, HALO, CONV_CH), lambda bi, i: (bi, jnp.maximum(i * nh - 1, 0), 0)),
            pl.BlockSpec((1, HALO, CONV_CH), lambda bi, i: (bi, jnp.minimum((i + 1) * nh, s // HALO - 1), 0)),
            pl.BlockSpec((1, tm, MLA_WIDTH), lambda bi, i: (bi, i, 0)),
            pl.BlockSpec((1, tm, d), lambda bi, i: (bi, i, 0)),
            pl.BlockSpec(conv_w.shape, const),
            pl.BlockSpec(conv_b.shape, const),
            pl.BlockSpec(conv_g.shape, const),
            pl.BlockSpec(conv_beta.shape, const),
            pl.BlockSpec(w_o.shape, const),
            pl.BlockSpec(g1.shape, const),
            pl.BlockSpec(b1.shape, const),
        ],
        out_specs=pl.BlockSpec((1, tm, d), lambda bi, i: (bi, i, 0)),
        out_shape=jax.ShapeDtypeStruct((b, s, d), F32),
        scratch_shapes=[
            pltpu.VMEM((tm + 2 * HALO, CONV_CH), F32),
            pltpu.VMEM((SUBLANES, tm + 2 * HALO, CONV_CH), F32),
            pltpu.VMEM((tm, CONV_CH), BF16),
        ],
        compiler_params=_cparams(("parallel", "parallel")),
        name="mix_out",
    )(u_pre, u_pre, u_pre, attn, x, conv_w, conv_b, conv_g, conv_beta, w_o, g1, b1)


def _xattn_kernel(h1_ref, wq_ref, kx_ref, vx_ref, wo_ref, g2_ref, b2_ref, wr_ref,
                  h2_out, h2b_out, aff_out, o_scr):
    h1 = h1_ref[0]
    q = (_dot(h1.astype(BF16), wq_ref[...]) * (1.0 / math.sqrt(MEM_HEAD_DIM))).astype(BF16)
    for h in range(MEM_HEADS):
        lo, hi = h * MEM_HEAD_DIM, (h + 1) * MEM_HEAD_DIM
        sc = _dot_nt(q[:, lo:hi], kx_ref[0, :, lo:hi])
        m = jnp.max(sc, axis=-1, keepdims=True)
        p = jnp.exp(sc - m)
        l = jnp.sum(p, axis=-1, keepdims=True)
        o_scr[:, lo:hi] = (_dot(p.astype(BF16), vx_ref[0, :, lo:hi]) / l).astype(BF16)
    xa = _dot(o_scr[...], wo_ref[...])
    h2 = _layer_norm(DEEPNORM_ALPHA * h1 + xa, g2_ref[...], b2_ref[...])
    h2_out[0] = h2
    h2b_out[0] = h2.astype(BF16)
    logits = _dot_nt(wr_ref[...], h2, precision=lax.Precision.HIGHEST)
    e = jnp.exp(logits - jnp.max(logits, axis=0, keepdims=True))
    aff_out[0] = e / jnp.sum(e, axis=0, keepdims=True)


def _xattn(h1, wq, kx, vx, wo, g2, b2, w_router_t, tm):
    b, s, d = h1.shape
    m = kx.shape[1]
    const = lambda bi, i: (0, 0)
    return pl.pallas_call(
        _xattn_kernel,
        grid=(b, s // tm),
        in_specs=[
            pl.BlockSpec((1, tm, d), lambda bi, i: (bi, i, 0)),
            pl.BlockSpec(wq.shape, const),
            pl.BlockSpec((1, m, d), lambda bi, i: (bi, 0, 0)),
            pl.BlockSpec((1, m, d), lambda bi, i: (bi, 0, 0)),
            pl.BlockSpec(wo.shape, const),
            pl.BlockSpec(g2.shape, const),
            pl.BlockSpec(b2.shape, const),
            pl.BlockSpec(w_router_t.shape, const),
        ],
        out_specs=[
            pl.BlockSpec((1, tm, d), lambda bi, i: (bi, i, 0)),
            pl.BlockSpec((1, tm, d), lambda bi, i: (bi, i, 0)),
            pl.BlockSpec((1, N_EXPERTS, tm), lambda bi, i: (bi, 0, i)),
        ],
        out_shape=[
            jax.ShapeDtypeStruct((b, s, d), F32),
            jax.ShapeDtypeStruct((b, s, d), BF16),
            jax.ShapeDtypeStruct((b, N_EXPERTS, s), F32),
        ],
        scratch_shapes=[pltpu.VMEM((tm, d), BF16)],
        compiler_params=_cparams(("parallel", "parallel")),
        name="xattn",
    )(h1, wq, kx, vx, wo, g2, b2, w_router_t)


def _topk_kernel(aff_ref, slot_out, slot_t_out, off_out, *, cap):
    aff = aff_ref[...]
    rows, s = aff.shape
    capf = jnp.float32(cap)

    def not_done(carry):
        return carry[2] > 0

    def bisect(carry):
        lo, hi, _ = carry
        mid = 0.5 * (lo + hi)
        take = jnp.sum(jnp.where(aff >= mid, 1.0, 0.0), axis=1, keepdims=True) >= capf
        lo = jnp.where(take, mid, lo)
        hi = jnp.where(take, hi, mid)
        smallest_in = jnp.min(jnp.where(aff >= lo, aff, jnp.inf), axis=1, keepdims=True)
        largest_in = jnp.max(jnp.where(aff < hi, aff, -jnp.inf), axis=1, keepdims=True)
        open_rows = jnp.sum(jnp.where(smallest_in == largest_in, 0.0, 1.0))
        return lo, hi, open_rows.astype(jnp.int32)

    _, hi, _ = lax.while_loop(not_done, bisect,
                              (jnp.zeros((rows, 1), F32), jnp.full((rows, 1), 2.0, F32), jnp.int32(1)))
    th = jnp.max(jnp.where(aff < hi, aff, -jnp.inf), axis=1, keepdims=True)
    gt = aff > th
    eq = aff == th
    n_gt = jnp.sum(jnp.where(gt, 1.0, 0.0), axis=1, keepdims=True)
    tri = jnp.where(lax.broadcasted_iota(jnp.int32, (s, s), 0) < lax.broadcasted_iota(jnp.int32, (s, s), 1),
                    1.0, 0.0).astype(BF16)
    tie_rank = _dot(jnp.where(eq, 1.0, 0.0).astype(BF16), tri)
    sel = jnp.logical_or(gt, jnp.logical_and(eq, tie_rank < (capf - n_gt)))
    pos = _dot(jnp.where(sel, 1.0, 0.0).astype(BF16), tri)
    slot = jnp.where(sel, pos, -1.0)
    slot_out[...] = slot
    slot_t_out[...] = slot.T
    lanes = off_out.shape[1]
    before = jnp.where(lax.broadcasted_iota(jnp.int32, (s, lanes), 0)
                       < ROUTE_CHUNK * lax.broadcasted_iota(jnp.int32, (s, lanes), 1), 1.0, 0.0).astype(BF16)
    off_out[...] = _dot(jnp.where(sel, 1.0, 0.0).astype(BF16), before)


def _topk(aff2d, cap):
    rows, s = aff2d.shape
    return pl.pallas_call(
        functools.partial(_topk_kernel, cap=cap),
        out_shape=[
            jax.ShapeDtypeStruct((rows, s), F32),
            jax.ShapeDtypeStruct((s, rows), F32),
            jax.ShapeDtypeStruct((rows, LANES), F32),
        ],
        compiler_params=pltpu.CompilerParams(vmem_limit_bytes=VMEM_LIMIT),
        name="topk",
    )(aff2d)


def _route_tables(off_tab, n_b, n_e, cap):
    n_j = off_tab.shape[1] - 1
    off = off_tab.astype(jnp.int32)
    base = jnp.minimum((off[:, :n_j] // BF16_ROWS) * BF16_ROWS, cap - ROUTE_WIN)
    over = jnp.any((off[:, 1:] - base > ROUTE_WIN).reshape(n_b, n_e, n_j), axis=1)
    return base, over.astype(jnp.int32)


def _dispatch_kernel(base_ref, over_ref, h2b_ref, slot_ref, aff_ref, xs_out, gate_out, p_scr, *, cap):
    bi = pl.program_id(0)
    j = pl.program_id(1)
    n_j = pl.num_programs(1)
    n_e, t = slot_ref.shape
    h = h2b_ref[0]

    @pl.when(j == 0)
    def _():
        xs_out[...] = jnp.zeros(xs_out.shape, xs_out.dtype)
        gate_out[...] = jnp.zeros(gate_out.shape, gate_out.dtype)

    @pl.when(over_ref[bi * n_j + j] == 0)
    def _():
        w_iota = lax.broadcasted_iota(jnp.int32, (ROUTE_WIN, t), 0).astype(F32)
        bases = []
        for e in range(n_e):
            base = pl.multiple_of(base_ref[(bi * n_e + e) * n_j + j], BF16_ROWS)
            hit = slot_ref[e:e + 1, :] == w_iota + base.astype(F32)
            p_scr[e * ROUTE_WIN:(e + 1) * ROUTE_WIN, :] = jnp.where(hit, 1.0, 0.0).astype(BF16)
            gate_out[e, 0, pl.ds(base, ROUTE_WIN), :] += jnp.sum(jnp.where(hit, aff_ref[e:e + 1, :], 0.0),
                                                                 axis=1, keepdims=True)
            bases.append(base)
        picked = _dot(p_scr[...], h)
        for e in range(n_e):
            rows = pl.ds(bases[e], ROUTE_WIN)
            xs_out[e, 0, rows, :] += picked[e * ROUTE_WIN:(e + 1) * ROUTE_WIN, :].astype(BF16)

    @pl.when(over_ref[bi * n_j + j] != 0)
    def _():
        c_iota = lax.broadcasted_iota(jnp.int32, (cap, t), 0).astype(F32)
        for e in range(n_e):
            hit = slot_ref[e:e + 1, :] == c_iota
            xs_out[e, 0] += _dot(jnp.where(hit, 1.0, 0.0).astype(BF16), h).astype(BF16)
            gate_out[e, 0] += jnp.sum(jnp.where(hit, aff_ref[e:e + 1, :], 0.0), axis=1, keepdims=True)


def _dispatch(base_flat, over_flat, h2b, slot2d, aff2d, n_e, cap):
    b, s, d = h2b.shape
    t = ROUTE_CHUNK
    grid_spec = pltpu.PrefetchScalarGridSpec(
        num_scalar_prefetch=2,
        grid=(b, s // t),
        in_specs=[
            pl.BlockSpec((1, t, d), lambda bi, j, *_: (bi, j, 0)),
            pl.BlockSpec((n_e, t), lambda bi, j, *_: (bi, j)),
            pl.BlockSpec((n_e, t), lambda bi, j, *_: (bi, j)),
        ],
        out_specs=[
            pl.BlockSpec((n_e, 1, cap, d), lambda bi, j, *_: (0, bi, 0, 0)),
            pl.BlockSpec((n_e, 1, cap, 1), lambda bi, j, *_: (0, bi, 0, 0)),
        ],
        scratch_shapes=[pltpu.VMEM((n_e * ROUTE_WIN, t), BF16)],
    )
    return pl.pallas_call(
        functools.partial(_dispatch_kernel, cap=cap),
        grid_spec=grid_spec,
        out_shape=[
            jax.ShapeDtypeStruct((n_e, b, cap, d), BF16),
            jax.ShapeDtypeStruct((n_e, b, cap, 1), F32),
        ],
        compiler_params=_cparams(("parallel", "arbitrary")),
        name="dispatch",
    )(base_flat, over_flat, h2b, slot2d, aff2d)


FFN_ROWS = 512
FFN_CHUNK = 512


def _experts_kernel(xs_ref, gate_ref, wg_ref, wu_ref, wd_ref, y_out, wg_s, wu_s, wd_s, hid_s):
    e = pl.program_id(0)
    f = pl.program_id(1)
    n_chunks = wg_s.shape[1]

    def stage():
        slot = e % 2
        wg_s[slot, f] = wg_ref[0].astype(BF16)
        wu_s[slot, f] = wu_ref[0].astype(BF16)
        wd_s[slot, f] = wd_ref[0].astype(BF16)

    @pl.when(e == 0)
    def _():
        stage()
        y_out[0] = jnp.zeros(y_out.shape[1:], y_out.dtype)

    @pl.when(e > 0)
    def _():
        stage()
        slot = (e - 1) % 2
        xs = xs_ref[0]
        for c in range(n_chunks):
            g = _dot(xs, wg_s[slot, c])
            u = _dot(xs, wu_s[slot, c])
            hid_s[:, c * FFN_CHUNK:(c + 1) * FFN_CHUNK] = (g * _sigmoid(g) * u).astype(BF16)
        y = _dot(hid_s[:, 0:FFN_CHUNK], wd_s[slot, 0])
        for c in range(1, n_chunks):
            y = y + _dot(hid_s[:, c * FFN_CHUNK:(c + 1) * FFN_CHUNK], wd_s[slot, c])
        y_out[0] = (y * gate_ref[0]).astype(BF16)


def _experts(xs, gate, w_gate, w_up, w_down):
    e, n, d = xs.shape
    ff = w_gate.shape[2]
    n_chunks = ff // FFN_CHUNK
    assert n // FFN_ROWS == n_chunks
    prev = lambda ei, fi: (jnp.maximum(ei - 1, 0), fi, 0)
    cur = lambda ei: jnp.minimum(ei, e - 1)
    return pl.pallas_call(
        _experts_kernel,
        grid=(e + 1, n_chunks),
        in_specs=[
            pl.BlockSpec((1, FFN_ROWS, d), prev),
            pl.BlockSpec((1, FFN_ROWS, 1), prev),
            pl.BlockSpec((1, d, FFN_CHUNK), lambda ei, fi: (cur(ei), 0, fi)),
            pl.BlockSpec((1, d, FFN_CHUNK), lambda ei, fi: (cur(ei), 0, fi)),
            pl.BlockSpec((1, FFN_CHUNK, d), lambda ei, fi: (cur(ei), fi, 0)),
        ],
        out_specs=pl.BlockSpec((1, FFN_ROWS, d), lambda ei, fi: (jnp.where(ei == 0, e, ei - 1), fi, 0)),
        out_shape=jax.ShapeDtypeStruct((e + 1, n, d), BF16),
        scratch_shapes=[
            pltpu.VMEM((2, n_chunks, d, FFN_CHUNK), BF16),
            pltpu.VMEM((2, n_chunks, d, FFN_CHUNK), BF16),
            pltpu.VMEM((2, n_chunks, FFN_CHUNK, d), BF16),
            pltpu.VMEM((FFN_ROWS, ff), BF16),
        ],
        compiler_params=_cparams(("arbitrary", "arbitrary")),
        name="experts",
    )(xs, gate, w_gate, w_up, w_down)


def _combine_kernel(base_ref, over_ref, slot_t_ref, base_t_ref, y_ref, h2_ref, g3_ref, b3_ref, out_ref, ywin_scr,
                    *, cap):
    bi = pl.program_id(0)
    j = pl.program_id(1)
    n_j = pl.num_programs(1)
    n_e = y_ref.shape[0]
    rows = slot_t_ref.shape[1]

    def spread_cols(per_expert):
        width = n_e * per_expert
        col = lax.broadcasted_iota(jnp.int32, (rows, width), 1) // per_expert
        return jnp.where(lax.broadcasted_iota(jnp.int32, (rows, width), 0) == bi * n_e + col, 1.0, 0.0).astype(BF16)

    def lane_in_group(per_expert):
        return (lax.broadcasted_iota(jnp.int32, (1, n_e * per_expert), 1) % per_expert).astype(F32)

    def finish(ff):
        out_ref[0] = _layer_norm(DEEPNORM_ALPHA * h2_ref[0] + ff, g3_ref[...], b3_ref[...])

    slot_bf = slot_t_ref[...].astype(BF16)

    @pl.when(over_ref[bi * n_j + j] == 0)
    def _():
        spread = spread_cols(ROUTE_WIN)
        slot_wide = _dot(slot_bf, spread)
        base_wide = _dot(jnp.broadcast_to(base_t_ref[0], (SUBLANES, rows)).astype(BF16), spread)[0:1, :]
        onehot = jnp.where(slot_wide - base_wide == lane_in_group(ROUTE_WIN), 1.0, 0.0).astype(BF16)
        for e in range(n_e):
            base = pl.multiple_of(base_ref[(bi * n_e + e) * n_j + j], BF16_ROWS)
            ywin_scr[e * ROUTE_WIN:(e + 1) * ROUTE_WIN, :] = y_ref[e, 0, pl.ds(base, ROUTE_WIN), :]
        finish(_dot(onehot, ywin_scr[...]))

    @pl.when(over_ref[bi * n_j + j] != 0)
    def _():
        slot_wide = _dot(slot_bf, spread_cols(cap))
        onehot = jnp.where(slot_wide == lane_in_group(cap), 1.0, 0.0).astype(BF16)
        finish(_dot(onehot, y_ref[:, 0].reshape(n_e * cap, y_ref.shape[3])))


def _combine(base_flat, over_flat, slot_t, base_t, y4, h2, g3, b3, cap):
    b, s, d = h2.shape
    e = y4.shape[0] - 1
    t = ROUTE_CHUNK
    const = lambda bi, j, *_: (0, 0)
    grid_spec = pltpu.PrefetchScalarGridSpec(
        num_scalar_prefetch=2,
        grid=(b, s // t),
        in_specs=[
            pl.BlockSpec((t, slot_t.shape[1]), lambda bi, j, *_: (j, 0)),
            pl.BlockSpec((1, 1, base_t.shape[2]), lambda bi, j, *_: (j, 0, 0)),
            pl.BlockSpec((e, 1, cap, d), lambda bi, j, *_: (0, bi, 0, 0)),
            pl.BlockSpec((1, t, d), lambda bi, j, *_: (bi, j, 0)),
            pl.BlockSpec(g3.shape, const),
            pl.BlockSpec(b3.shape, const),
        ],
        out_specs=pl.BlockSpec((1, t, d), lambda bi, j, *_: (bi, j, 0)),
        scratch_shapes=[pltpu.VMEM((e * ROUTE_WIN, d), BF16)],
    )
    return pl.pallas_call(
        functools.partial(_combine_kernel, cap=cap),
        grid_spec=grid_spec,
        out_shape=jax.ShapeDtypeStruct((b, s, d), F32),
        compiler_params=_cparams(("parallel", "parallel")),
        name="combine",
    )(base_flat, over_flat, slot_t, base_t, y4, h2, g3, b3)


def _extend_weights(w_in, w_uq, w_uk, w_uv):
    half = MLA_ROPE_DIM // 2
    d = w_in.shape[0]
    kr = w_in[:, SPLIT_KV:SPLIT_KR]
    z = lambda n: jnp.zeros((d, n), w_in.dtype)
    kr_full = jnp.concatenate([z(MLA_NOPE_DIM), kr, z(HEAD_PAD - MLA_QK_DIM)], axis=1)
    kr_swap = jnp.concatenate([z(MLA_NOPE_DIM), -kr[:, half:], kr[:, :half], z(HEAD_PAD - MLA_QK_DIM)], axis=1)
    w_in_ext = jnp.concatenate([w_in[:, :SPLIT_KV], kr_full, kr_swap, w_in[:, SPLIT_KR:]], axis=1)

    wq = w_uq.reshape(MLA_Q_RANK, MLA_HEADS, MLA_QK_DIM)
    zq = lambda n: jnp.zeros((MLA_Q_RANK, MLA_HEADS, n), w_uq.dtype)
    wq_full = jnp.concatenate([wq, zq(HEAD_PAD - MLA_QK_DIM)], axis=2)
    wq_swap = jnp.concatenate([zq(MLA_NOPE_DIM), -wq[:, :, MLA_NOPE_DIM + half:],
                               wq[:, :, MLA_NOPE_DIM:MLA_NOPE_DIM + half], zq(HEAD_PAD - MLA_QK_DIM)], axis=2)
    wq_ext = jnp.concatenate([wq_full.reshape(MLA_Q_RANK, QK_PAD), wq_swap.reshape(MLA_Q_RANK, QK_PAD)], axis=1)

    wk = w_uk.reshape(MLA_KV_RANK, MLA_HEADS, MLA_NOPE_DIM)
    wk_full = jnp.concatenate([wk, jnp.zeros((MLA_KV_RANK, MLA_HEADS, HEAD_PAD - MLA_NOPE_DIM), w_uk.dtype)], axis=2)
    wv = w_uv.reshape(MLA_KV_RANK, MLA_HEADS, MLA_V_DIM)
    wv_full = jnp.concatenate([wv, jnp.zeros((MLA_KV_RANK, MLA_HEADS, HEAD_PAD - MLA_V_DIM), w_uv.dtype)], axis=2)
    wkv_ext = jnp.concatenate([wk_full.reshape(MLA_KV_RANK, QK_PAD), wv_full.reshape(MLA_KV_RANK, QK_PAD)], axis=1)
    return w_in_ext.astype(BF16), wq_ext.astype(BF16), wkv_ext.astype(BF16)


def _rope_freq_lanes():
    inv_freq = ROPE_BASE ** (-jnp.arange(0, MLA_ROPE_DIM, 2, dtype=F32) / MLA_ROPE_DIM)
    z = lambda n: jnp.zeros((n,), F32)
    return jnp.concatenate([z(MLA_NOPE_DIM), inv_freq, inv_freq, z(HEAD_PAD - MLA_QK_DIM)])[None, :]


def _value_one_lanes():
    one_hot = (jnp.arange(HEAD_PAD) == MLA_V_DIM).astype(F32)
    return jnp.tile(one_hot, MLA_HEADS)[None, :]


def kernel(x, mem, positions, w_in, q_norm_g, w_uq, kv_norm_g, w_uk, w_uv, conv_w, conv_b, conv_ln_g, conv_ln_b,
           w_o, ln1_g, ln1_b, xa_w_q, xa_w_k, xa_w_v, xa_w_o, ln2_g, ln2_b, w_router, w_gate, w_up, w_down,
           ln3_g, ln3_b):
    assert w_in.shape[0] == DEPTH == 1
    b, s, d = x.shape
    cap = CAPACITY_FACTOR * s // N_EXPERTS
    tm = 512

    w_in_ext, wq_ext, wkv_ext = _extend_weights(w_in[0], w_uq[0], w_uk[0], w_uv[0])
    posf = positions.astype(F32)[..., None]
    q, k, v, u_pre = _front(x, posf, w_in_ext, q_norm_g, wq_ext, kv_norm_g, wkv_ext, _rope_freq_lanes(),
                             _value_one_lanes(), tm)
    attn = _attention(q, k, v, tm)
    h1 = _mix_out(u_pre, attn, x, conv_w[0], conv_b, conv_ln_g, conv_ln_b, w_o[0].astype(BF16), ln1_g, ln1_b, tm)

    kx, vx = _mem_kv(mem, xa_w_k[0].astype(BF16), xa_w_v[0].astype(BF16))
    h2, h2b, aff = _xattn(h1, xa_w_q[0].astype(BF16), kx, vx, xa_w_o[0].astype(BF16), ln2_g, ln2_b,
                          w_router[0].T, tm)

    aff2d = aff.reshape(b * N_EXPERTS, s)
    slot, slot_t, off_tab = _topk(aff2d, cap)
    n_j = s // ROUTE_CHUNK
    base, over = _route_tables(off_tab[:, :n_j + 1], b, N_EXPERTS, cap)
    base_flat, over_flat = base.reshape(-1), over.reshape(-1)
    xs, gate = _dispatch(base_flat, over_flat, h2b, slot, aff2d, N_EXPERTS, cap)
    y = _experts(xs.reshape(N_EXPERTS, b * cap, d), gate.reshape(N_EXPERTS, b * cap, 1),
                 w_gate[0], w_up[0], w_down[0])
    base_t = base.T.astype(F32).reshape(n_j, 1, b * N_EXPERTS)
    return _combine(base_flat, over_flat, slot_t, base_t, y.reshape(N_EXPERTS + 1, b, cap, d), h2, ln3_g, ln3_b, cap)
```

```python
import functools
import math

import jax
import jax.numpy as jnp
from jax import lax
from jax.experimental import pallas as pl
from jax.experimental.pallas import tpu as pltpu

F32 = jnp.float32
BF16 = jnp.bfloat16

D_MODEL = 1024
MLA_HEADS = 8
MLA_NOPE_DIM = 64
MLA_ROPE_DIM = 32
MLA_QK_DIM = MLA_NOPE_DIM + MLA_ROPE_DIM
MLA_V_DIM = 64
MLA_Q_RANK = 256
MLA_KV_RANK = 128
MLA_WIDTH = MLA_HEADS * MLA_V_DIM
CONV_CH = D_MODEL - MLA_WIDTH
CONV_WIDTH = 31
CONV_PAD = (CONV_WIDTH - 1) // 2
ROPE_BASE = 10000.0
MEM_HEADS = 4
MEM_HEAD_DIM = D_MODEL // MEM_HEADS
N_EXPERTS = 16
EXPERT_FF = 2048
CAPACITY_FACTOR = 2
NORM_EPS = 1e-5
DEPTH = 1
DEEPNORM_ALPHA = (2.0 * DEPTH) ** 0.25
SPLIT_Q = MLA_Q_RANK
SPLIT_KV = SPLIT_Q + MLA_KV_RANK
SPLIT_KR = SPLIT_KV + MLA_ROPE_DIM

HEAD_PAD = 128
QK_PAD = MLA_HEADS * HEAD_PAD
OFF_CQ = 0
OFF_CKV = OFF_CQ + MLA_Q_RANK
OFF_KR = OFF_CKV + MLA_KV_RANK
OFF_KRS = OFF_KR + HEAD_PAD
OFF_A = OFF_KRS + HEAD_PAD
OFF_G = OFF_A + CONV_CH
IN_EXT = OFF_G + CONV_CH

SUBLANES = 8
LANES = 128
BF16_ROWS = 16
ROUTE_CHUNK = 256
ROUTE_WIN = 64
HALO = 16
VMEM_LIMIT = 56 * 1024 * 1024


def _cparams(sem):
    return pltpu.CompilerParams(dimension_semantics=sem, vmem_limit_bytes=VMEM_LIMIT)


def _layer_norm(v, g, b):
    mu = jnp.mean(v, axis=-1, keepdims=True)
    d = v - mu
    var = jnp.mean(d * d, axis=-1, keepdims=True)
    return d * lax.rsqrt(var + NORM_EPS) * g + b


def _rms_norm(v, g):
    return v * lax.rsqrt(jnp.mean(v * v, axis=-1, keepdims=True) + NORM_EPS) * g


def _sigmoid(v):
    return 1.0 / (1.0 + jnp.exp(-v))


def _dot(a, b):
    return jnp.dot(a, b, preferred_element_type=F32)


def _dot_nt(a, b, precision=None):
    return lax.dot_general(a, b, (((1,), (1,)), ((), ())), preferred_element_type=F32, precision=precision)


def _front_kernel(x_ref, pos_ref, win_ref, qg_ref, wq_ref, kvg_ref, wkv_ref, invf_ref, vone_ref,
                  q_out, k_out, v_out, u_out):
    x = x_ref[0].astype(BF16)
    hc = _dot(x, win_ref[...])
    cqn = _rms_norm(hc[:, OFF_CQ:OFF_CKV], qg_ref[...])
    qq = _dot(cqn.astype(BF16), wq_ref[...])
    ckvn = _rms_norm(hc[:, OFF_CKV:OFF_KR], kvg_ref[...])
    kv = _dot(ckvn.astype(BF16), wkv_ref[...])
    ang = pos_ref[0] * invf_ref[...]
    cos = jnp.cos(ang)
    sin = jnp.sin(ang)
    krot = hc[:, OFF_KR:OFF_KRS] * cos + hc[:, OFF_KRS:OFF_A] * sin
    scale = math.log2(math.e) / math.sqrt(MLA_QK_DIM)
    for h in range(MLA_HEADS):
        lo, hi = h * HEAD_PAD, (h + 1) * HEAD_PAD
        qh = (qq[:, lo:hi] * cos + qq[:, QK_PAD + lo:QK_PAD + hi] * sin) * scale
        q_out[0, :, lo:hi] = qh.astype(BF16)
        k_out[0, :, lo:hi] = (kv[:, lo:hi] + krot).astype(BF16)
    v_out[0] = (kv[:, QK_PAD:] + vone_ref[...]).astype(BF16)
    u_out[0] = hc[:, OFF_A:OFF_G] * _sigmoid(hc[:, OFF_G:IN_EXT])


def _front(x, posf, w_in_ext, qg, wq_ext, kvg, wkv_ext, invf, vone, tm):
    b, s, d = x.shape
    const = lambda bi, i: (0, 0)
    return pl.pallas_call(
        _front_kernel,
        grid=(b, s // tm),
        in_specs=[
            pl.BlockSpec((1, tm, d), lambda bi, i: (bi, i, 0)),
            pl.BlockSpec((1, tm, 1), lambda bi, i: (bi, i, 0)),
            pl.BlockSpec(w_in_ext.shape, const),
            pl.BlockSpec(qg.shape, const),
            pl.BlockSpec(wq_ext.shape, const),
            pl.BlockSpec(kvg.shape, const),
            pl.BlockSpec(wkv_ext.shape, const),
            pl.BlockSpec(invf.shape, const),
            pl.BlockSpec(vone.shape, const),
        ],
        out_specs=[
            pl.BlockSpec((1, tm, QK_PAD), lambda bi, i: (bi, i, 0)),
            pl.BlockSpec((1, tm, QK_PAD), lambda bi, i: (bi, i, 0)),
            pl.BlockSpec((1, tm, QK_PAD), lambda bi, i: (bi, i, 0)),
            pl.BlockSpec((1, tm, CONV_CH), lambda bi, i: (bi, i, 0)),
        ],
        out_shape=[
            jax.ShapeDtypeStruct((b, s, QK_PAD), BF16),
            jax.ShapeDtypeStruct((b, s, QK_PAD), BF16),
            jax.ShapeDtypeStruct((b, s, QK_PAD), BF16),
            jax.ShapeDtypeStruct((b, s, CONV_CH), F32),
        ],
        compiler_params=_cparams(("parallel", "parallel")),
        name="front",
    )(x, posf, w_in_ext, qg, wq_ext, kvg, wkv_ext, invf, vone)


def _attn_kernel(q_ref, k_ref, v_ref, o_ref):
    for h in range(MLA_HEADS):
        lo, hi = h * HEAD_PAD, (h + 1) * HEAD_PAD
        sc = _dot_nt(q_ref[0, :, lo:hi], k_ref[0, :, lo:hi])
        p = jnp.exp2(sc - jnp.max(sc, axis=-1, keepdims=True))
        pv = _dot(p.astype(BF16), v_ref[0, :, lo:hi])
        o = pv[:, :MLA_V_DIM] / pv[:, MLA_V_DIM:MLA_V_DIM + 1]
        o_ref[0, :, h * MLA_V_DIM:(h + 1) * MLA_V_DIM] = o.astype(BF16)


def _attention(q, k, v, tq):
    b, s, _ = q.shape
    return pl.pallas_call(
        _attn_kernel,
        grid=(b, s // tq),
        in_specs=[
            pl.BlockSpec((1, tq, QK_PAD), lambda bi, i: (bi, i, 0)),
            pl.BlockSpec((1, s, QK_PAD), lambda bi, i: (bi, 0, 0)),
            pl.BlockSpec((1, s, QK_PAD), lambda bi, i: (bi, 0, 0)),
        ],
        out_specs=pl.BlockSpec((1, tq, MLA_WIDTH), lambda bi, i: (bi, i, 0)),
        out_shape=jax.ShapeDtypeStruct((b, s, MLA_WIDTH), BF16),
        compiler_params=_cparams(("parallel", "parallel")),
        name="attn",
    )(q, k, v)


def _mem_kv_kernel(mem_ref, wk_ref, wv_ref, k_out, v_out):
    m = mem_ref[0].astype(BF16)
    k_out[0] = _dot(m, wk_ref[...]).astype(BF16)
    v_out[0] = _dot(m, wv_ref[...]).astype(BF16)


def _mem_kv(mem, wk, wv):
    b, m, d = mem.shape
    const = lambda bi: (0, 0)
    return pl.pallas_call(
        _mem_kv_kernel,
        grid=(b,),
        in_specs=[
            pl.BlockSpec((1, m, d), lambda bi: (bi, 0, 0)),
            pl.BlockSpec(wk.shape, const),
            pl.BlockSpec(wv.shape, const),
        ],
        out_specs=[pl.BlockSpec((1, m, d), lambda bi: (bi, 0, 0))] * 2,
        out_shape=[jax.ShapeDtypeStruct((b, m, d), BF16)] * 2,
        compiler_params=_cparams(("parallel",)),
        name="mem_kv",
    )(mem, wk, wv)


CONV_SUB = 64


def _mid_kernel(ucur_ref, uprev_ref, unext_ref, attn_ref, x_ref, cw_ref, cb_ref, cg_ref, cbeta_ref, wo_ref,
                g1_ref, b1_ref, wq_ref, kx_ref, vx_ref, xwo_ref, g2_ref, b2_ref, wr_ref,
                h2_out, h2b_out, aff_out, win_ref, shift_ref, u_scr, o_scr):
    tm = ucur_ref.shape[1]
    i = pl.program_id(1)
    last = pl.num_programs(1) - 1
    win_ref[0:HALO, :] = jnp.where(i > 0, uprev_ref[0], 0.0)
    win_ref[HALO:HALO + tm, :] = ucur_ref[0]
    win_ref[HALO + tm:2 * HALO + tm, :] = jnp.where(i < last, unext_ref[0], 0.0)
    span = tm + 2 * HALO - SUBLANES
    shift_ref[0] = win_ref[...]
    for j in range(1, SUBLANES):
        shift_ref[j, 0:span, :] = win_ref[j:j + span, :]

    for r in range(tm // CONV_SUB):
        acc = jnp.broadcast_to(cb_ref[...], (CONV_SUB, CONV_CH))
        for t in range(CONV_WIDTH):
            off = HALO - CONV_PAD + t
            row = r * CONV_SUB + (off // SUBLANES) * SUBLANES
            acc = acc + shift_ref[off % SUBLANES, row:row + CONV_SUB, :] * cw_ref[t:t + 1, :]
        y = _layer_norm(acc, cg_ref[...], cbeta_ref[...])
        u_scr[r * CONV_SUB:(r + 1) * CONV_SUB, :] = (y * _sigmoid(y)).astype(BF16)
    mix = _dot(attn_ref[0], wo_ref[0:MLA_WIDTH, :]) + _dot(u_scr[...], wo_ref[MLA_WIDTH:, :])
    h1 = _layer_norm(DEEPNORM_ALPHA * x_ref[0] + mix, g1_ref[...], b1_ref[...])
    q = (_dot(h1.astype(BF16), wq_ref[...]) * (1.0 / math.sqrt(MEM_HEAD_DIM))).astype(BF16)
    for h in range(MEM_HEADS):
        lo, hi = h * MEM_HEAD_DIM, (h + 1) * MEM_HEAD_DIM
        sc = _dot_nt(q[:, lo:hi], kx_ref[0, :, lo:hi])
        p = jnp.exp(sc - jnp.max(sc, axis=-1, keepdims=True))
        l = jnp.sum(p, axis=-1, keepdims=True)
        o_scr[:, lo:hi] = (_dot(p.astype(BF16), vx_ref[0, :, lo:hi]) / l).astype(BF16)
    xa = _dot(o_scr[...], xwo_ref[...])
    h2 = _layer_norm(DEEPNORM_ALPHA * h1 + xa, g2_ref[...], b2_ref[...])
    h2_out[0] = h2
    h2_hi = h2.astype(BF16)
    h2b_out[0] = h2_hi
    h2_lo = (h2 - h2_hi.astype(F32)).astype(BF16)
    hi_terms = _dot(h2_hi, wr_ref[...])
    logits = hi_terms[:, :LANES] + hi_terms[:, LANES:] + _dot(h2_lo, wr_ref[:, :LANES])
    lt = logits.T[0:aff_out.shape[1], :]
    ex = jnp.exp(lt - jnp.max(lt, axis=0, keepdims=True))
    aff_out[0] = ex / jnp.sum(ex, axis=0, keepdims=True)


def _mid(u_pre, attn, x, conv_w, conv_b, conv_g, conv_beta, w_o, g1, b1, wq, kx, vx, xwo, g2, b2, wr_split, tm):
    b, s, d = x.shape
    m = kx.shape[1]
    nh = tm // HALO
    const = lambda bi, i: (0, 0)
    tile = lambda w: pl.BlockSpec((1, tm, w), lambda bi, i: (bi, i, 0))
    whole = lambda arr: pl.BlockSpec(arr.shape, const)
    return pl.pallas_call(
        _mid_kernel,
        grid=(b, s // tm),
        in_specs=[
            tile(CONV_CH),
            pl.BlockSpec((1, HALO, CONV_CH), lambda bi, i: (bi, jnp.maximum(i * nh - 1, 0), 0)),
            pl.BlockSpec((1, HALO, CONV_CH), lambda bi, i: (bi, jnp.minimum((i + 1) * nh, s // HALO - 1), 0)),
            tile(MLA_WIDTH),
            tile(d),
            whole(conv_w), whole(conv_b), whole(conv_g), whole(conv_beta), whole(w_o), whole(g1), whole(b1),
            whole(wq),
            pl.BlockSpec((1, m, d), lambda bi, i: (bi, 0, 0)),
            pl.BlockSpec((1, m, d), lambda bi, i: (bi, 0, 0)),
            whole(xwo), whole(g2), whole(b2), whole(wr_split),
        ],
        out_specs=[
            tile(d),
            tile(d),
            pl.BlockSpec((1, N_EXPERTS, tm), lambda bi, i: (bi, 0, i)),
        ],
        out_shape=[
            jax.ShapeDtypeStruct((b, s, d), F32),
            jax.ShapeDtypeStruct((b, s, d), BF16),
            jax.ShapeDtypeStruct((b, N_EXPERTS, s), F32),
        ],
        scratch_shapes=[
            pltpu.VMEM((tm + 2 * HALO, CONV_CH), F32),
            pltpu.VMEM((SUBLANES, tm + 2 * HALO, CONV_CH), F32),
            pltpu.VMEM((tm, CONV_CH), BF16),
            pltpu.VMEM((tm, d), BF16),
        ],
        compiler_params=_cparams(("parallel", "parallel")),
        name="mid",
    )(u_pre, u_pre, u_pre, attn, x, conv_w, conv_b, conv_g, conv_beta, w_o, g1, b1, wq, kx, vx, xwo, g2, b2, wr_split)


def _split_router(w_router):
    hi = w_router.astype(BF16)
    lo = (w_router - hi.astype(F32)).astype(BF16)
    pad = jnp.zeros((w_router.shape[0], LANES - w_router.shape[1]), BF16)
    return jnp.concatenate([hi, pad, lo, pad], axis=1)


def _topk_kernel(aff_ref, slot_out, slot_t_out, off_out, *, cap):
    aff = aff_ref[...]
    rows, s = aff.shape
    capf = jnp.float32(cap)

    def not_done(carry):
        return carry[2] > 0

    def bisect(carry):
        lo, hi, _ = carry
        mid = 0.5 * (lo + hi)
        take = jnp.sum(jnp.where(aff >= mid, 1.0, 0.0), axis=1, keepdims=True) >= capf
        lo = jnp.where(take, mid, lo)
        hi = jnp.where(take, hi, mid)
        smallest_in = jnp.min(jnp.where(aff >= lo, aff, jnp.inf), axis=1, keepdims=True)
        largest_in = jnp.max(jnp.where(aff < hi, aff, -jnp.inf), axis=1, keepdims=True)
        open_rows = jnp.sum(jnp.where(smallest_in == largest_in, 0.0, 1.0))
        return lo, hi, open_rows.astype(jnp.int32)

    _, hi, _ = lax.while_loop(not_done, bisect,
                              (jnp.zeros((rows, 1), F32), jnp.full((rows, 1), 2.0, F32), jnp.int32(1)))
    th = jnp.max(jnp.where(aff < hi, aff, -jnp.inf), axis=1, keepdims=True)
    gt = aff > th
    eq = aff == th
    n_gt = jnp.sum(jnp.where(gt, 1.0, 0.0), axis=1, keepdims=True)
    tri = jnp.where(lax.broadcasted_iota(jnp.int32, (s, s), 0) < lax.broadcasted_iota(jnp.int32, (s, s), 1),
                    1.0, 0.0).astype(BF16)
    tie_rank = _dot(jnp.where(eq, 1.0, 0.0).astype(BF16), tri)
    sel = jnp.logical_or(gt, jnp.logical_and(eq, tie_rank < (capf - n_gt)))
    pos = _dot(jnp.where(sel, 1.0, 0.0).astype(BF16), tri)
    slot = jnp.where(sel, pos, -1.0)
    slot_out[...] = slot
    slot_t_out[...] = slot.T
    lanes = off_out.shape[1]
    before = jnp.where(lax.broadcasted_iota(jnp.int32, (s, lanes), 0)
                       < ROUTE_CHUNK * lax.broadcasted_iota(jnp.int32, (s, lanes), 1), 1.0, 0.0).astype(BF16)
    off_out[...] = _dot(jnp.where(sel, 1.0, 0.0).astype(BF16), before)


def _topk(aff2d, cap):
    rows, s = aff2d.shape
    return pl.pallas_call(
        functools.partial(_topk_kernel, cap=cap),
        out_shape=[
            jax.ShapeDtypeStruct((rows, s), F32),
            jax.ShapeDtypeStruct((s, rows), F32),
            jax.ShapeDtypeStruct((rows, LANES), F32),
        ],
        compiler_params=pltpu.CompilerParams(vmem_limit_bytes=VMEM_LIMIT),
        name="topk",
    )(aff2d)


def _route_tables(off_tab, n_b, n_e, cap):
    n_j = off_tab.shape[1] - 1
    off = off_tab.astype(jnp.int32)
    base = jnp.minimum((off[:, :n_j] // BF16_ROWS) * BF16_ROWS, cap - ROUTE_WIN)
    over = jnp.any((off[:, 1:] - base > ROUTE_WIN).reshape(n_b, n_e, n_j), axis=1)
    return base, over.astype(jnp.int32)


def _dispatch_kernel(base_ref, over_ref, h2b_ref, slot_ref, aff_ref, xs_out, gate_out, p_scr, *, cap):
    bi = pl.program_id(0)
    j = pl.program_id(1)
    n_j = pl.num_programs(1)
    n_e, t = slot_ref.shape
    h = h2b_ref[0]

    @pl.when(j == 0)
    def _():
        xs_out[...] = jnp.zeros(xs_out.shape, xs_out.dtype)
        gate_out[...] = jnp.zeros(gate_out.shape, gate_out.dtype)

    @pl.when(over_ref[bi * n_j + j] == 0)
    def _():
        w_iota = lax.broadcasted_iota(jnp.int32, (ROUTE_WIN, t), 0).astype(F32)
        bases = []
        for e in range(n_e):
            base = pl.multiple_of(base_ref[(bi * n_e + e) * n_j + j], BF16_ROWS)
            hit = slot_ref[e:e + 1, :] == w_iota + base.astype(F32)
            p_scr[e * ROUTE_WIN:(e + 1) * ROUTE_WIN, :] = jnp.where(hit, 1.0, 0.0).astype(BF16)
            gate_out[e, 0, pl.ds(base, ROUTE_WIN), :] += jnp.sum(jnp.where(hit, aff_ref[e:e + 1, :], 0.0),
                                                                 axis=1, keepdims=True)
            bases.append(base)
        picked = _dot(p_scr[...], h)
        for e in range(n_e):
            rows = pl.ds(bases[e], ROUTE_WIN)
            xs_out[e, 0, rows, :] += picked[e * ROUTE_WIN:(e + 1) * ROUTE_WIN, :].astype(BF16)

    @pl.when(over_ref[bi * n_j + j] != 0)
    def _():
        c_iota = lax.broadcasted_iota(jnp.int32, (cap, t), 0).astype(F32)
        for e in range(n_e):
            hit = slot_ref[e:e + 1, :] == c_iota
            xs_out[e, 0] += _dot(jnp.where(hit, 1.0, 0.0).astype(BF16), h).astype(BF16)
            gate_out[e, 0] += jnp.sum(jnp.where(hit, aff_ref[e:e + 1, :], 0.0), axis=1, keepdims=True)


def _dispatch(base_flat, over_flat, h2b, slot2d, aff2d, n_e, cap):
    b, s, d = h2b.shape
    t = ROUTE_CHUNK
    grid_spec = pltpu.PrefetchScalarGridSpec(
        num_scalar_prefetch=2,
        grid=(b, s // t),
        in_specs=[
            pl.BlockSpec((1, t, d), lambda bi, j, *_: (bi, j, 0)),
            pl.BlockSpec((n_e, t), lambda bi, j, *_: (bi, j)),
            pl.BlockSpec((n_e, t), lambda bi, j, *_: (bi, j)),
        ],
        out_specs=[
            pl.BlockSpec((n_e, 1, cap, d), lambda bi, j, *_: (0, bi, 0, 0)),
            pl.BlockSpec((n_e, 1, cap, 1), lambda bi, j, *_: (0, bi, 0, 0)),
        ],
        scratch_shapes=[pltpu.VMEM((n_e * ROUTE_WIN, t), BF16)],
    )
    return pl.pallas_call(
        functools.partial(_dispatch_kernel, cap=cap),
        grid_spec=grid_spec,
        out_shape=[
            jax.ShapeDtypeStruct((n_e, b, cap, d), BF16),
            jax.ShapeDtypeStruct((n_e, b, cap, 1), F32),
        ],
        compiler_params=_cparams(("parallel", "arbitrary")),
        name="dispatch",
    )(base_flat, over_flat, h2b, slot2d, aff2d)


FFN_ROWS = 512
FFN_CHUNK = 512


def _experts_kernel(xs_ref, gate_ref, wg_ref, wu_ref, wd_ref, y_out, wg_s, wu_s, wd_s, hid_s):
    e = pl.program_id(0)
    f = pl.program_id(1)
    n_chunks = wg_s.shape[1]

    def stage():
        slot = e % 2
        wg_s[slot, f] = wg_ref[0].astype(BF16)
        wu_s[slot, f] = wu_ref[0].astype(BF16)
        wd_s[slot, f] = wd_ref[0].astype(BF16)

    @pl.when(e == 0)
    def _():
        stage()
        y_out[0] = jnp.zeros(y_out.shape[1:], y_out.dtype)

    @pl.when(e > 0)
    def _():
        stage()
        slot = (e - 1) % 2
        xs = xs_ref[0]
        for c in range(n_chunks):
            g = _dot(xs, wg_s[slot, c])
            u = _dot(xs, wu_s[slot, c])
            hid_s[:, c * FFN_CHUNK:(c + 1) * FFN_CHUNK] = (g * _sigmoid(g) * u).astype(BF16)
        y = _dot(hid_s[:, 0:FFN_CHUNK], wd_s[slot, 0])
        for c in range(1, n_chunks):
            y = y + _dot(hid_s[:, c * FFN_CHUNK:(c + 1) * FFN_CHUNK], wd_s[slot, c])
        y_out[0] = (y * gate_ref[0]).astype(BF16)


def _experts(xs, gate, w_gate, w_up, w_down):
    e, n, d = xs.shape
    ff = w_gate.shape[2]
    n_chunks = ff // FFN_CHUNK
    assert n // FFN_ROWS == n_chunks
    prev = lambda ei, fi: (jnp.maximum(ei - 1, 0), fi, 0)
    cur = lambda ei: jnp.minimum(ei, e - 1)
    return pl.pallas_call(
        _experts_kernel,
        grid=(e + 1, n_chunks),
        in_specs=[
            pl.BlockSpec((1, FFN_ROWS, d), prev),
            pl.BlockSpec((1, FFN_ROWS, 1), prev),
            pl.BlockSpec((1, d, FFN_CHUNK), lambda ei, fi: (cur(ei), 0, fi)),
            pl.BlockSpec((1, d, FFN_CHUNK), lambda ei, fi: (cur(ei), 0, fi)),
            pl.BlockSpec((1, FFN_CHUNK, d), lambda ei, fi: (cur(ei), fi, 0)),
        ],
        out_specs=pl.BlockSpec((1, FFN_ROWS, d), lambda ei, fi: (jnp.where(ei == 0, e, ei - 1), fi, 0)),
        out_shape=jax.ShapeDtypeStruct((e + 1, n, d), BF16),
        scratch_shapes=[
            pltpu.VMEM((2, n_chunks, d, FFN_CHUNK), BF16),
            pltpu.VMEM((2, n_chunks, d, FFN_CHUNK), BF16),
            pltpu.VMEM((2, n_chunks, FFN_CHUNK, d), BF16),
            pltpu.VMEM((FFN_ROWS, ff), BF16),
        ],
        compiler_params=_cparams(("arbitrary", "arbitrary")),
        name="experts",
    )(xs, gate, w_gate, w_up, w_down)


def _combine_kernel(base_ref, over_ref, slot_t_ref, base_t_ref, y_ref, h2_ref, g3_ref, b3_ref, out_ref, ywin_scr,
                    *, cap):
    bi = pl.program_id(0)
    j = pl.program_id(1)
    n_j = pl.num_programs(1)
    n_e = y_ref.shape[0]
    rows = slot_t_ref.shape[1]

    def spread_cols(per_expert):
        width = n_e * per_expert
        col = lax.broadcasted_iota(jnp.int32, (rows, width), 1) // per_expert
        return jnp.where(lax.broadcasted_iota(jnp.int32, (rows, width), 0) == bi * n_e + col, 1.0, 0.0).astype(BF16)

    def lane_in_group(per_expert):
        return (lax.broadcasted_iota(jnp.int32, (1, n_e * per_expert), 1) % per_expert).astype(F32)

    def finish(ff):
        out_ref[0] = _layer_norm(DEEPNORM_ALPHA * h2_ref[0] + ff, g3_ref[...], b3_ref[...])

    slot_bf = slot_t_ref[...].astype(BF16)

    @pl.when(over_ref[bi * n_j + j] == 0)
    def _():
        spread = spread_cols(ROUTE_WIN)
        slot_wide = _dot(slot_bf, spread)
        base_wide = _dot(jnp.broadcast_to(base_t_ref[0], (SUBLANES, rows)).astype(BF16), spread)[0:1, :]
        onehot = jnp.where(slot_wide - base_wide == lane_in_group(ROUTE_WIN), 1.0, 0.0).astype(BF16)
        for e in range(n_e):
            base = pl.multiple_of(base_ref[(bi * n_e + e) * n_j + j], BF16_ROWS)
            ywin_scr[e * ROUTE_WIN:(e + 1) * ROUTE_WIN, :] = y_ref[e, 0, pl.ds(base, ROUTE_WIN), :]
        finish(_dot(onehot, ywin_scr[...]))

    @pl.when(over_ref[bi * n_j + j] != 0)
    def _():
        slot_wide = _dot(slot_bf, spread_cols(cap))
        onehot = jnp.where(slot_wide == lane_in_group(cap), 1.0, 0.0).astype(BF16)
        finish(_dot(onehot, y_ref[:, 0].reshape(n_e * cap, y_ref.shape[3])))


def _combine(base_flat, over_flat, slot_t, base_t, y4, h2, g3, b3, cap):
    b, s, d = h2.shape
    e = y4.shape[0] - 1
    t = ROUTE_CHUNK
    const = lambda bi, j, *_: (0, 0)
    grid_spec = pltpu.PrefetchScalarGridSpec(
        num_scalar_prefetch=2,
        grid=(b, s // t),
        in_specs=[
            pl.BlockSpec((t, slot_t.shape[1]), lambda bi, j, *_: (j, 0)),
            pl.BlockSpec((1, 1, base_t.shape[2]), lambda bi, j, *_: (j, 0, 0)),
            pl.BlockSpec((e, 1, cap, d), lambda bi, j, *_: (0, bi, 0, 0)),
            pl.BlockSpec((1, t, d), lambda bi, j, *_: (bi, j, 0)),
            pl.BlockSpec(g3.shape, const),
            pl.BlockSpec(b3.shape, const),
        ],
        out_specs=pl.BlockSpec((1, t, d), lambda bi, j, *_: (bi, j, 0)),
        scratch_shapes=[pltpu.VMEM((e * ROUTE_WIN, d), BF16)],
    )
    return pl.pallas_call(
        functools.partial(_combine_kernel, cap=cap),
        grid_spec=grid_spec,
        out_shape=jax.ShapeDtypeStruct((b, s, d), F32),
        compiler_params=_cparams(("parallel", "parallel")),
        name="combine",
    )(base_flat, over_flat, slot_t, base_t, y4, h2, g3, b3)


def _extend_weights(w_in, w_uq, w_uk, w_uv):
    half = MLA_ROPE_DIM // 2
    d = w_in.shape[0]
    kr = w_in[:, SPLIT_KV:SPLIT_KR]
    z = lambda n: jnp.zeros((d, n), w_in.dtype)
    kr_full = jnp.concatenate([z(MLA_NOPE_DIM), kr, z(HEAD_PAD - MLA_QK_DIM)], axis=1)
    kr_swap = jnp.concatenate([z(MLA_NOPE_DIM), -kr[:, half:], kr[:, :half], z(HEAD_PAD - MLA_QK_DIM)], axis=1)
    w_in_ext = jnp.concatenate([w_in[:, :SPLIT_KV], kr_full, kr_swap, w_in[:, SPLIT_KR:]], axis=1)

    wq = w_uq.reshape(MLA_Q_RANK, MLA_HEADS, MLA_QK_DIM)
    zq = lambda n: jnp.zeros((MLA_Q_RANK, MLA_HEADS, n), w_uq.dtype)
    wq_full = jnp.concatenate([wq, zq(HEAD_PAD - MLA_QK_DIM)], axis=2)
    wq_swap = jnp.concatenate([zq(MLA_NOPE_DIM), -wq[:, :, MLA_NOPE_DIM + half:],
                               wq[:, :, MLA_NOPE_DIM:MLA_NOPE_DIM + half], zq(HEAD_PAD - MLA_QK_DIM)], axis=2)
    wq_ext = jnp.concatenate([wq_full.reshape(MLA_Q_RANK, QK_PAD), wq_swap.reshape(MLA_Q_RANK, QK_PAD)], axis=1)

    wk = w_uk.reshape(MLA_KV_RANK, MLA_HEADS, MLA_NOPE_DIM)
    wk_full = jnp.concatenate([wk, jnp.zeros((MLA_KV_RANK, MLA_HEADS, HEAD_PAD - MLA_NOPE_DIM), w_uk.dtype)], axis=2)
    wv = w_uv.reshape(MLA_KV_RANK, MLA_HEADS, MLA_V_DIM)
    wv_full = jnp.concatenate([wv, jnp.zeros((MLA_KV_RANK, MLA_HEADS, HEAD_PAD - MLA_V_DIM), w_uv.dtype)], axis=2)
    wkv_ext = jnp.concatenate([wk_full.reshape(MLA_KV_RANK, QK_PAD), wv_full.reshape(MLA_KV_RANK, QK_PAD)], axis=1)
    return w_in_ext.astype(BF16), wq_ext.astype(BF16), wkv_ext.astype(BF16)


def _rope_freq_lanes():
    inv_freq = ROPE_BASE ** (-jnp.arange(0, MLA_ROPE_DIM, 2, dtype=F32) / MLA_ROPE_DIM)
    z = lambda n: jnp.zeros((n,), F32)
    return jnp.concatenate([z(MLA_NOPE_DIM), inv_freq, inv_freq, z(HEAD_PAD - MLA_QK_DIM)])[None, :]


def _value_one_lanes():
    one_hot = (jnp.arange(HEAD_PAD) == MLA_V_DIM).astype(F32)
    return jnp.tile(one_hot, MLA_HEADS)[None, :]


def kernel(x, mem, positions, w_in, q_norm_g, w_uq, kv_norm_g, w_uk, w_uv, conv_w, conv_b, conv_ln_g, conv_ln_b,
           w_o, ln1_g, ln1_b, xa_w_q, xa_w_k, xa_w_v, xa_w_o, ln2_g, ln2_b, w_router, w_gate, w_up, w_down,
           ln3_g, ln3_b):
    assert w_in.shape[0] == DEPTH == 1
    b, s, d = x.shape
    cap = CAPACITY_FACTOR * s // N_EXPERTS
    tm = 512

    w_in_ext, wq_ext, wkv_ext = _extend_weights(w_in[0], w_uq[0], w_uk[0], w_uv[0])
    posf = positions.astype(F32)[..., None]
    q, k, v, u_pre = _front(x, posf, w_in_ext, q_norm_g, wq_ext, kv_norm_g, wkv_ext, _rope_freq_lanes(),
                             _value_one_lanes(), tm)
    attn = _attention(q, k, v, 1024)
    kx, vx = _mem_kv(mem, xa_w_k[0].astype(BF16), xa_w_v[0].astype(BF16))
    h2, h2b, aff = _mid(u_pre, attn, x, conv_w[0], conv_b, conv_ln_g, conv_ln_b, w_o[0].astype(BF16), ln1_g, ln1_b,
                        xa_w_q[0].astype(BF16), kx, vx, xa_w_o[0].astype(BF16), ln2_g, ln2_b,
                        _split_router(w_router[0]), tm)

    aff2d = aff.reshape(b * N_EXPERTS, s)
    slot, slot_t, off_tab = _topk(aff2d, cap)
    n_j = s // ROUTE_CHUNK
    base, over = _route_tables(off_tab[:, :n_j + 1], b, N_EXPERTS, cap)
    base_flat, over_flat = base.reshape(-1), over.reshape(-1)
    xs, gate = _dispatch(base_flat, over_flat, h2b, slot, aff2d, N_EXPERTS, cap)
    y = _experts(xs.reshape(N_EXPERTS, b * cap, d), gate.reshape(N_EXPERTS, b * cap, 1),
                 w_gate[0], w_up[0], w_down[0])
    base_t = base.T.astype(F32).reshape(n_j, 1, b * N_EXPERTS)
    return _combine(base_flat, over_flat, slot_t, base_t, y.reshape(N_EXPERTS + 1, b, cap, d), h2, ln3_g, ln3_b, cap)
```

```python
import functools
import math

import jax
import jax.numpy as jnp
from jax import lax
from jax.experimental import pallas as pl
from jax.experimental.pallas import tpu as pltpu

F32 = jnp.float32
BF16 = jnp.bfloat16

D_MODEL = 1024
MLA_HEADS = 8
MLA_NOPE_DIM = 64
MLA_ROPE_DIM = 32
MLA_QK_DIM = MLA_NOPE_DIM + MLA_ROPE_DIM
MLA_V_DIM = 64
MLA_Q_RANK = 256
MLA_KV_RANK = 128
MLA_WIDTH = MLA_HEADS * MLA_V_DIM
CONV_CH = D_MODEL - MLA_WIDTH
CONV_WIDTH = 31
CONV_PAD = (CONV_WIDTH - 1) // 2
ROPE_BASE = 10000.0
MEM_HEADS = 4
MEM_HEAD_DIM = D_MODEL // MEM_HEADS
N_EXPERTS = 16
EXPERT_FF = 2048
CAPACITY_FACTOR = 2
NORM_EPS = 1e-5
DEPTH = 1
DEEPNORM_ALPHA = (2.0 * DEPTH) ** 0.25
SPLIT_Q = MLA_Q_RANK
SPLIT_KV = SPLIT_Q + MLA_KV_RANK
SPLIT_KR = SPLIT_KV + MLA_ROPE_DIM

HEAD_PAD = 128
QK_PAD = MLA_HEADS * HEAD_PAD
OFF_CQ = 0
OFF_CKV = OFF_CQ + MLA_Q_RANK
OFF_KR = OFF_CKV + MLA_KV_RANK
OFF_KRS = OFF_KR + HEAD_PAD
OFF_A = OFF_KRS + HEAD_PAD
OFF_G = OFF_A + CONV_CH
IN_EXT = OFF_G + CONV_CH

SUBLANES = 8
LANES = 128
BF16_ROWS = 16
ROUTE_CHUNK = 256
ROUTE_WIN = 64
HALO = 16
VMEM_LIMIT = 56 * 1024 * 1024


def _cparams(sem):
    return pltpu.CompilerParams(dimension_semantics=sem, vmem_limit_bytes=VMEM_LIMIT)


def _layer_norm(v, g, b):
    mu = jnp.mean(v, axis=-1, keepdims=True)
    d = v - mu
    var = jnp.mean(d * d, axis=-1, keepdims=True)
    return d * lax.rsqrt(var + NORM_EPS) * g + b


def _rms_norm(v, g):
    return v * lax.rsqrt(jnp.mean(v * v, axis=-1, keepdims=True) + NORM_EPS) * g


def _sigmoid(v):
    return 1.0 / (1.0 + jnp.exp(-v))


def _dot(a, b):
    return jnp.dot(a, b, preferred_element_type=F32)


def _dot_nt(a, b, precision=None):
    return lax.dot_general(a, b, (((1,), (1,)), ((), ())), preferred_element_type=F32, precision=precision)


def _front_kernel(x_ref, pos_ref, win_ref, qg_ref, wq_ref, kvg_ref, wkv_ref, invf_ref, vone_ref,
                  q_out, k_out, v_out, u_out):
    x = x_ref[0].astype(BF16)
    hc = _dot(x, win_ref[...])
    cqn = _rms_norm(hc[:, OFF_CQ:OFF_CKV], qg_ref[...])
    qq = _dot(cqn.astype(BF16), wq_ref[...])
    ckvn = _rms_norm(hc[:, OFF_CKV:OFF_KR], kvg_ref[...])
    kv = _dot(ckvn.astype(BF16), wkv_ref[...])
    tm = x_ref.shape[1]
    ang = invf_ref[...] * pos_ref[0]
    cos_h, sin_h = jnp.cos(ang), jnp.sin(ang)
    ones = lambda n: jnp.ones((n, tm), F32)
    zeros = lambda n: jnp.zeros((n, tm), F32)
    cos = jnp.concatenate([ones(MLA_NOPE_DIM), cos_h, cos_h, ones(HEAD_PAD - MLA_QK_DIM)], axis=0).T
    sin = jnp.concatenate([zeros(MLA_NOPE_DIM), sin_h, sin_h, zeros(HEAD_PAD - MLA_QK_DIM)], axis=0).T
    krot = hc[:, OFF_KR:OFF_KRS] * cos + hc[:, OFF_KRS:OFF_A] * sin
    scale = math.log2(math.e) / math.sqrt(MLA_QK_DIM)
    for h in range(MLA_HEADS):
        lo, hi = h * HEAD_PAD, (h + 1) * HEAD_PAD
        qh = (qq[:, lo:hi] * cos + qq[:, QK_PAD + lo:QK_PAD + hi] * sin) * scale
        q_out[0, :, lo:hi] = qh.astype(BF16)
        k_out[0, :, lo:hi] = (kv[:, lo:hi] + krot).astype(BF16)
    v_out[0] = (kv[:, QK_PAD:] + vone_ref[...]).astype(BF16)
    u_out[0] = hc[:, OFF_A:OFF_G] * _sigmoid(hc[:, OFF_G:IN_EXT])


def _front(x, posf, w_in_ext, qg, wq_ext, kvg, wkv_ext, invf, vone, tm):
    b, s, d = x.shape
    const = lambda bi, i: (0, 0)
    return pl.pallas_call(
        _front_kernel,
        grid=(b, s // tm),
        in_specs=[
            pl.BlockSpec((1, tm, d), lambda bi, i: (bi, i, 0)),
            pl.BlockSpec((1, 1, tm), lambda bi, i: (bi, 0, i)),
            pl.BlockSpec(w_in_ext.shape, const),
            pl.BlockSpec(qg.shape, const),
            pl.BlockSpec(wq_ext.shape, const),
            pl.BlockSpec(kvg.shape, const),
            pl.BlockSpec(wkv_ext.shape, const),
            pl.BlockSpec(invf.shape, const),
            pl.BlockSpec(vone.shape, const),
        ],
        out_specs=[
            pl.BlockSpec((1, tm, QK_PAD), lambda bi, i: (bi, i, 0)),
            pl.BlockSpec((1, tm, QK_PAD), lambda bi, i: (bi, i, 0)),
            pl.BlockSpec((1, tm, QK_PAD), lambda bi, i: (bi, i, 0)),
            pl.BlockSpec((1, tm, CONV_CH), lambda bi, i: (bi, i, 0)),
        ],
        out_shape=[
            jax.ShapeDtypeStruct((b, s, QK_PAD), BF16),
            jax.ShapeDtypeStruct((b, s, QK_PAD), BF16),
            jax.ShapeDtypeStruct((b, s, QK_PAD), BF16),
            jax.ShapeDtypeStruct((b, s, CONV_CH), F32),
        ],
        compiler_params=_cparams(("parallel", "parallel")),
        name="front",
    )(x, posf, w_in_ext, qg, wq_ext, kvg, wkv_ext, invf, vone)


def _attn_kernel(q_ref, k_ref, v_ref, o_ref):
    for h in range(MLA_HEADS):
        lo, hi = h * HEAD_PAD, (h + 1) * HEAD_PAD
        sc = _dot_nt(q_ref[0, :, lo:hi], k_ref[0, :, lo:hi])
        p = jnp.exp2(sc - jnp.max(sc, axis=-1, keepdims=True))
        pv = _dot(p.astype(BF16), v_ref[0, :, lo:hi])
        o = pv[:, :MLA_V_DIM] / pv[:, MLA_V_DIM:MLA_V_DIM + 1]
        o_ref[0, :, h * MLA_V_DIM:(h + 1) * MLA_V_DIM] = o.astype(BF16)


def _attention(q, k, v, tq):
    b, s, _ = q.shape
    return pl.pallas_call(
        _attn_kernel,
        grid=(b, s // tq),
        in_specs=[
            pl.BlockSpec((1, tq, QK_PAD), lambda bi, i: (bi, i, 0)),
            pl.BlockSpec((1, s, QK_PAD), lambda bi, i: (bi, 0, 0)),
            pl.BlockSpec((1, s, QK_PAD), lambda bi, i: (bi, 0, 0)),
        ],
        out_specs=pl.BlockSpec((1, tq, MLA_WIDTH), lambda bi, i: (bi, i, 0)),
        out_shape=jax.ShapeDtypeStruct((b, s, MLA_WIDTH), BF16),
        compiler_params=_cparams(("parallel", "parallel")),
        name="attn",
    )(q, k, v)


def _mem_kv_kernel(mem_ref, wk_ref, wv_ref, k_out, v_out):
    m = mem_ref[0].astype(BF16)
    k_out[0] = _dot(m, wk_ref[...]).astype(BF16)
    v_out[0] = _dot(m, wv_ref[...]).astype(BF16)


def _mem_kv(mem, wk, wv):
    b, m, d = mem.shape
    const = lambda bi: (0, 0)
    return pl.pallas_call(
        _mem_kv_kernel,
        grid=(b,),
        in_specs=[
            pl.BlockSpec((1, m, d), lambda bi: (bi, 0, 0)),
            pl.BlockSpec(wk.shape, const),
            pl.BlockSpec(wv.shape, const),
        ],
        out_specs=[pl.BlockSpec((1, m, d), lambda bi: (bi, 0, 0))] * 2,
        out_shape=[jax.ShapeDtypeStruct((b, m, d), BF16)] * 2,
        compiler_params=_cparams(("parallel",)),
        name="mem_kv",
    )(mem, wk, wv)


CONV_SUB = 64


def _mid_kernel(ucur_ref, uprev_ref, unext_ref, attn_ref, x_ref, cw_ref, cb_ref, cg_ref, cbeta_ref, wo_ref,
                g1_ref, b1_ref, wq_ref, kx_ref, vx_ref, xwo_ref, g2_ref, b2_ref, wr_ref,
                h2_out, h2b_out, aff_out, win_ref, shift_ref, u_scr, o_scr):
    tm = ucur_ref.shape[1]
    i = pl.program_id(1)
    last = pl.num_programs(1) - 1
    win_ref[0:HALO, :] = jnp.where(i > 0, uprev_ref[0], 0.0)
    win_ref[HALO:HALO + tm, :] = ucur_ref[0]
    win_ref[HALO + tm:2 * HALO + tm, :] = jnp.where(i < last, unext_ref[0], 0.0)
    span = tm + 2 * HALO - SUBLANES
    shift_ref[0] = win_ref[...]
    for j in range(1, SUBLANES):
        shift_ref[j, 0:span, :] = win_ref[j:j + span, :]

    for r in range(tm // CONV_SUB):
        acc = jnp.broadcast_to(cb_ref[...], (CONV_SUB, CONV_CH))
        for t in range(CONV_WIDTH):
            off = HALO - CONV_PAD + t
            row = r * CONV_SUB + (off // SUBLANES) * SUBLANES
            acc = acc + shift_ref[off % SUBLANES, row:row + CONV_SUB, :] * cw_ref[t:t + 1, :]
        y = _layer_norm(acc, cg_ref[...], cbeta_ref[...])
        u_scr[r * CONV_SUB:(r + 1) * CONV_SUB, :] = (y * _sigmoid(y)).astype(BF16)
    mix = _dot(attn_ref[0], wo_ref[0:MLA_WIDTH, :]) + _dot(u_scr[...], wo_ref[MLA_WIDTH:, :])
    h1 = _layer_norm(DEEPNORM_ALPHA * x_ref[0] + mix, g1_ref[...], b1_ref[...])
    q = (_dot(h1.astype(BF16), wq_ref[...]) * (1.0 / math.sqrt(MEM_HEAD_DIM))).astype(BF16)
    for h in range(MEM_HEADS):
        lo, hi = h * MEM_HEAD_DIM, (h + 1) * MEM_HEAD_DIM
        sc = _dot_nt(q[:, lo:hi], kx_ref[0, :, lo:hi])
        p = jnp.exp(sc - jnp.max(sc, axis=-1, keepdims=True))
        l = jnp.sum(p, axis=-1, keepdims=True)
        o_scr[:, lo:hi] = (_dot(p.astype(BF16), vx_ref[0, :, lo:hi]) / l).astype(BF16)
    xa = _dot(o_scr[...], xwo_ref[...])
    h2 = _layer_norm(DEEPNORM_ALPHA * h1 + xa, g2_ref[...], b2_ref[...])
    h2_out[0] = h2
    h2_hi = h2.astype(BF16)
    h2b_out[0] = h2_hi
    h2_lo = (h2 - h2_hi.astype(F32)).astype(BF16)
    hi_terms = _dot(h2_hi, wr_ref[...])
    logits = hi_terms[:, :LANES] + hi_terms[:, LANES:] + _dot(h2_lo, wr_ref[:, :LANES])
    lt = logits.T[0:aff_out.shape[1], :]
    ex = jnp.exp(lt - jnp.max(lt, axis=0, keepdims=True))
    aff_out[0] = ex / jnp.sum(ex, axis=0, keepdims=True)


def _mid(u_pre, attn, x, conv_w, conv_b, conv_g, conv_beta, w_o, g1, b1, wq, kx, vx, xwo, g2, b2, wr_split, tm):
    b, s, d = x.shape
    m = kx.shape[1]
    nh = tm // HALO
    const = lambda bi, i: (0, 0)
    tile = lambda w: pl.BlockSpec((1, tm, w), lambda bi, i: (bi, i, 0))
    whole = lambda arr: pl.BlockSpec(arr.shape, const)
    return pl.pallas_call(
        _mid_kernel,
        grid=(b, s // tm),
        in_specs=[
            tile(CONV_CH),
            pl.BlockSpec((1, HALO, CONV_CH), lambda bi, i: (bi, jnp.maximum(i * nh - 1, 0), 0)),
            pl.BlockSpec((1, HALO, CONV_CH), lambda bi, i: (bi, jnp.minimum((i + 1) * nh, s // HALO - 1), 0)),
            tile(MLA_WIDTH),
            tile(d),
            whole(conv_w), whole(conv_b), whole(conv_g), whole(conv_beta), whole(w_o), whole(g1), whole(b1),
            whole(wq),
            pl.BlockSpec((1, m, d), lambda bi, i: (bi, 0, 0)),
            pl.BlockSpec((1, m, d), lambda bi, i: (bi, 0, 0)),
            whole(xwo), whole(g2), whole(b2), whole(wr_split),
        ],
        out_specs=[
            tile(d),
            tile(d),
            pl.BlockSpec((1, N_EXPERTS, tm), lambda bi, i: (bi, 0, i)),
        ],
        out_shape=[
            jax.ShapeDtypeStruct((b, s, d), F32),
            jax.ShapeDtypeStruct((b, s, d), BF16),
            jax.ShapeDtypeStruct((b, N_EXPERTS, s), F32),
        ],
        scratch_shapes=[
            pltpu.VMEM((tm + 2 * HALO, CONV_CH), F32),
            pltpu.VMEM((SUBLANES, tm + 2 * HALO, CONV_CH), F32),
            pltpu.VMEM((tm, CONV_CH), BF16),
            pltpu.VMEM((tm, d), BF16),
        ],
        compiler_params=_cparams(("parallel", "parallel")),
        name="mid",
    )(u_pre, u_pre, u_pre, attn, x, conv_w, conv_b, conv_g, conv_beta, w_o, g1, b1, wq, kx, vx, xwo, g2, b2, wr_split)


def _split_router(w_router):
    hi = w_router.astype(BF16)
    lo = (w_router - hi.astype(F32)).astype(BF16)
    pad = jnp.zeros((w_router.shape[0], LANES - w_router.shape[1]), BF16)
    return jnp.concatenate([hi, pad, lo, pad], axis=1)


def _topk_kernel(aff_ref, slot_out, slot_t_out, off_out, *, cap):
    aff = aff_ref[...]
    rows, s = aff.shape
    capf = jnp.float32(cap)

    def not_done(carry):
        return carry[2] > 0

    def bisect(carry):
        lo, hi, _ = carry
        mid = 0.5 * (lo + hi)
        take = jnp.sum(jnp.where(aff >= mid, 1.0, 0.0), axis=1, keepdims=True) >= capf
        lo = jnp.where(take, mid, lo)
        hi = jnp.where(take, hi, mid)
        smallest_in = jnp.min(jnp.where(aff >= lo, aff, jnp.inf), axis=1, keepdims=True)
        largest_in = jnp.max(jnp.where(aff < hi, aff, -jnp.inf), axis=1, keepdims=True)
        open_rows = jnp.sum(jnp.where(smallest_in == largest_in, 0.0, 1.0))
        return lo, hi, open_rows.astype(jnp.int32)

    _, hi, _ = lax.while_loop(not_done, bisect,
                              (jnp.zeros((rows, 1), F32), jnp.full((rows, 1), 2.0, F32), jnp.int32(1)))
    th = jnp.max(jnp.where(aff < hi, aff, -jnp.inf), axis=1, keepdims=True)
    gt = aff > th
    eq = aff == th
    n_gt = jnp.sum(jnp.where(gt, 1.0, 0.0), axis=1, keepdims=True)
    tri = jnp.where(lax.broadcasted_iota(jnp.int32, (s, s), 0) < lax.broadcasted_iota(jnp.int32, (s, s), 1),
                    1.0, 0.0).astype(BF16)
    tie_rank = _dot(jnp.where(eq, 1.0, 0.0).astype(BF16), tri)
    sel = jnp.logical_or(gt, jnp.logical_and(eq, tie_rank < (capf - n_gt)))
    pos = _dot(jnp.where(sel, 1.0, 0.0).astype(BF16), tri)
    slot = jnp.where(sel, pos, -1.0)
    slot_out[...] = slot
    slot_t_out[...] = slot.T
    lanes = off_out.shape[1]
    before = jnp.where(lax.broadcasted_iota(jnp.int32, (s, lanes), 0)
                       < ROUTE_CHUNK * lax.broadcasted_iota(jnp.int32, (s, lanes), 1), 1.0, 0.0).astype(BF16)
    off_out[...] = _dot(jnp.where(sel, 1.0, 0.0).astype(BF16), before)


def _topk(aff2d, cap):
    rows, s = aff2d.shape
    return pl.pallas_call(
        functools.partial(_topk_kernel, cap=cap),
        out_shape=[
            jax.ShapeDtypeStruct((rows, s), F32),
            jax.ShapeDtypeStruct((s, rows), F32),
            jax.ShapeDtypeStruct((rows, LANES), F32),
        ],
        compiler_params=pltpu.CompilerParams(vmem_limit_bytes=VMEM_LIMIT),
        name="topk",
    )(aff2d)


def _route_tables(off_tab, n_b, n_e, cap):
    n_j = off_tab.shape[1] - 1
    off = off_tab.astype(jnp.int32)
    base = jnp.minimum((off[:, :n_j] // BF16_ROWS) * BF16_ROWS, cap - ROUTE_WIN)
    over = jnp.any((off[:, 1:] - base > ROUTE_WIN).reshape(n_b, n_e, n_j), axis=1)
    return base, over.astype(jnp.int32)


def _dispatch_kernel(base_ref, over_ref, h2b_ref, slot_ref, aff_ref, xs_out, gate_out, p_scr, *, cap):
    bi = pl.program_id(0)
    j = pl.program_id(1)
    n_j = pl.num_programs(1)
    n_e, t = slot_ref.shape
    h = h2b_ref[0]

    @pl.when(j == 0)
    def _():
        xs_out[...] = jnp.zeros(xs_out.shape, xs_out.dtype)
        gate_out[...] = jnp.zeros(gate_out.shape, gate_out.dtype)

    @pl.when(over_ref[bi * n_j + j] == 0)
    def _():
        w_iota = lax.broadcasted_iota(jnp.int32, (ROUTE_WIN, t), 0).astype(F32)
        bases = []
        for e in range(n_e):
            base = pl.multiple_of(base_ref[(bi * n_e + e) * n_j + j], BF16_ROWS)
            hit = slot_ref[e:e + 1, :] == w_iota + base.astype(F32)
            p_scr[e * ROUTE_WIN:(e + 1) * ROUTE_WIN, :] = jnp.where(hit, 1.0, 0.0).astype(BF16)
            gate_out[e, 0, pl.ds(base, ROUTE_WIN), :] += jnp.sum(jnp.where(hit, aff_ref[e:e + 1, :], 0.0),
                                                                 axis=1, keepdims=True)
            bases.append(base)
        picked = _dot(p_scr[...], h)
        for e in range(n_e):
            rows = pl.ds(bases[e], ROUTE_WIN)
            xs_out[e, 0, rows, :] += picked[e * ROUTE_WIN:(e + 1) * ROUTE_WIN, :].astype(BF16)

    @pl.when(over_ref[bi * n_j + j] != 0)
    def _():
        c_iota = lax.broadcasted_iota(jnp.int32, (cap, t), 0).astype(F32)
        for e in range(n_e):
            hit = slot_ref[e:e + 1, :] == c_iota
            xs_out[e, 0] += _dot(jnp.where(hit, 1.0, 0.0).astype(BF16), h).astype(BF16)
            gate_out[e, 0] += jnp.sum(jnp.where(hit, aff_ref[e:e + 1, :], 0.0), axis=1, keepdims=True)


def _dispatch(base_flat, over_flat, h2b, slot2d, aff2d, n_e, cap):
    b, s, d = h2b.shape
    t = ROUTE_CHUNK
    grid_spec = pltpu.PrefetchScalarGridSpec(
        num_scalar_prefetch=2,
        grid=(b, s // t),
        in_specs=[
            pl.BlockSpec((1, t, d), lambda bi, j, *_: (bi, j, 0)),
            pl.BlockSpec((n_e, t), lambda bi, j, *_: (bi, j)),
            pl.BlockSpec((n_e, t), lambda bi, j, *_: (bi, j)),
        ],
        out_specs=[
            pl.BlockSpec((n_e, 1, cap, d), lambda bi, j, *_: (0, bi, 0, 0)),
            pl.BlockSpec((n_e, 1, cap, 1), lambda bi, j, *_: (0, bi, 0, 0)),
        ],
        scratch_shapes=[pltpu.VMEM((n_e * ROUTE_WIN, t), BF16)],
    )
    return pl.pallas_call(
        functools.partial(_dispatch_kernel, cap=cap),
        grid_spec=grid_spec,
        out_shape=[
            jax.ShapeDtypeStruct((n_e, b, cap, d), BF16),
            jax.ShapeDtypeStruct((n_e, b, cap, 1), F32),
        ],
        compiler_params=_cparams(("parallel", "arbitrary")),
        name="dispatch",
    )(base_flat, over_flat, h2b, slot2d, aff2d)


FFN_ROWS = 512
FFN_CHUNK = 512


def _experts_kernel(xs_ref, gate_ref, wg_ref, wu_ref, wd_ref, y_out, wg_s, wu_s, wd_s, hid_s):
    e = pl.program_id(0)
    f = pl.program_id(1)
    n_chunks = wg_s.shape[1]

    def stage():
        slot = e % 2
        wg_s[slot, f] = wg_ref[0].astype(BF16)
        wu_s[slot, f] = wu_ref[0].astype(BF16)
        wd_s[slot, f] = wd_ref[0].astype(BF16)

    @pl.when(e == 0)
    def _():
        stage()
        y_out[0] = jnp.zeros(y_out.shape[1:], y_out.dtype)

    @pl.when(e > 0)
    def _():
        stage()
        slot = (e - 1) % 2
        xs = xs_ref[0]
        for c in range(n_chunks):
            g = _dot(xs, wg_s[slot, c])
            u = _dot(xs, wu_s[slot, c])
            hid_s[:, c * FFN_CHUNK:(c + 1) * FFN_CHUNK] = (g * _sigmoid(g) * u).astype(BF16)
        y = _dot(hid_s[:, 0:FFN_CHUNK], wd_s[slot, 0])
        for c in range(1, n_chunks):
            y = y + _dot(hid_s[:, c * FFN_CHUNK:(c + 1) * FFN_CHUNK], wd_s[slot, c])
        y_out[0] = (y * gate_ref[0]).astype(BF16)


def _experts(xs, gate, w_gate, w_up, w_down):
    e, n, d = xs.shape
    ff = w_gate.shape[2]
    n_chunks = ff // FFN_CHUNK
    assert n // FFN_ROWS == n_chunks
    prev = lambda ei, fi: (jnp.maximum(ei - 1, 0), fi, 0)
    cur = lambda ei: jnp.minimum(ei, e - 1)
    return pl.pallas_call(
        _experts_kernel,
        grid=(e + 1, n_chunks),
        in_specs=[
            pl.BlockSpec((1, FFN_ROWS, d), prev),
            pl.BlockSpec((1, FFN_ROWS, 1), prev),
            pl.BlockSpec((1, d, FFN_CHUNK), lambda ei, fi: (cur(ei), 0, fi)),
            pl.BlockSpec((1, d, FFN_CHUNK), lambda ei, fi: (cur(ei), 0, fi)),
            pl.BlockSpec((1, FFN_CHUNK, d), lambda ei, fi: (cur(ei), fi, 0)),
        ],
        out_specs=pl.BlockSpec((1, FFN_ROWS, d), lambda ei, fi: (jnp.where(ei == 0, e, ei - 1), fi, 0)),
        out_shape=jax.ShapeDtypeStruct((e + 1, n, d), BF16),
        scratch_shapes=[
            pltpu.VMEM((2, n_chunks, d, FFN_CHUNK), BF16),
            pltpu.VMEM((2, n_chunks, d, FFN_CHUNK), BF16),
            pltpu.VMEM((2, n_chunks, FFN_CHUNK, d), BF16),
            pltpu.VMEM((FFN_ROWS, ff), BF16),
        ],
        compiler_params=_cparams(("arbitrary", "arbitrary")),
        name="experts",
    )(xs, gate, w_gate, w_up, w_down)


def _combine_kernel(base_ref, over_ref, slot_t_ref, base_t_ref, y_ref, h2_ref, g3_ref, b3_ref, out_ref, ywin_scr,
                    *, cap):
    bi = pl.program_id(0)
    j = pl.program_id(1)
    n_j = pl.num_programs(1)
    n_e = y_ref.shape[0]
    rows = slot_t_ref.shape[1]

    def spread_cols(per_expert):
        width = n_e * per_expert
        col = lax.broadcasted_iota(jnp.int32, (rows, width), 1) // per_expert
        return jnp.where(lax.broadcasted_iota(jnp.int32, (rows, width), 0) == bi * n_e + col, 1.0, 0.0).astype(BF16)

    def lane_in_group(per_expert):
        return (lax.broadcasted_iota(jnp.int32, (1, n_e * per_expert), 1) % per_expert).astype(F32)

    def finish(ff):
        out_ref[0] = _layer_norm(DEEPNORM_ALPHA * h2_ref[0] + ff, g3_ref[...], b3_ref[...])

    slot_bf = slot_t_ref[...].astype(BF16)

    @pl.when(over_ref[bi * n_j + j] == 0)
    def _():
        spread = spread_cols(ROUTE_WIN)
        slot_wide = _dot(slot_bf, spread)
        base_wide = _dot(jnp.broadcast_to(base_t_ref[0], (SUBLANES, rows)).astype(BF16), spread)[0:1, :]
        onehot = jnp.where(slot_wide - base_wide == lane_in_group(ROUTE_WIN), 1.0, 0.0).astype(BF16)
        for e in range(n_e):
            base = pl.multiple_of(base_ref[(bi * n_e + e) * n_j + j], BF16_ROWS)
            ywin_scr[e * ROUTE_WIN:(e + 1) * ROUTE_WIN, :] = y_ref[e, 0, pl.ds(base, ROUTE_WIN), :]
        finish(_dot(onehot, ywin_scr[...]))

    @pl.when(over_ref[bi * n_j + j] != 0)
    def _():
        slot_wide = _dot(slot_bf, spread_cols(cap))
        onehot = jnp.where(slot_wide == lane_in_group(cap), 1.0, 0.0).astype(BF16)
        finish(_dot(onehot, y_ref[:, 0].reshape(n_e * cap, y_ref.shape[3])))


def _combine(base_flat, over_flat, slot_t, base_t, y4, h2, g3, b3, cap):
    b, s, d = h2.shape
    e = y4.shape[0] - 1
    t = ROUTE_CHUNK
    const = lambda bi, j, *_: (0, 0)
    grid_spec = pltpu.PrefetchScalarGridSpec(
        num_scalar_prefetch=2,
        grid=(b, s // t),
        in_specs=[
            pl.BlockSpec((t, slot_t.shape[1]), lambda bi, j, *_: (j, 0)),
            pl.BlockSpec((1, 1, base_t.shape[2]), lambda bi, j, *_: (j, 0, 0)),
            pl.BlockSpec((e, 1, cap, d), lambda bi, j, *_: (0, bi, 0, 0)),
            pl.BlockSpec((1, t, d), lambda bi, j, *_: (bi, j, 0)),
            pl.BlockSpec(g3.shape, const),
            pl.BlockSpec(b3.shape, const),
        ],
        out_specs=pl.BlockSpec((1, t, d), lambda bi, j, *_: (bi, j, 0)),
        scratch_shapes=[pltpu.VMEM((e * ROUTE_WIN, d), BF16)],
    )
    return pl.pallas_call(
        functools.partial(_combine_kernel, cap=cap),
        grid_spec=grid_spec,
        out_shape=jax.ShapeDtypeStruct((b, s, d), F32),
        compiler_params=_cparams(("parallel", "parallel")),
        name="combine",
    )(base_flat, over_flat, slot_t, base_t, y4, h2, g3, b3)


def _extend_weights(w_in, w_uq, w_uk, w_uv):
    half = MLA_ROPE_DIM // 2
    d = w_in.shape[0]
    kr = w_in[:, SPLIT_KV:SPLIT_KR]
    z = lambda n: jnp.zeros((d, n), w_in.dtype)
    kr_full = jnp.concatenate([z(MLA_NOPE_DIM), kr, z(HEAD_PAD - MLA_QK_DIM)], axis=1)
    kr_swap = jnp.concatenate([z(MLA_NOPE_DIM), -kr[:, half:], kr[:, :half], z(HEAD_PAD - MLA_QK_DIM)], axis=1)
    w_in_ext = jnp.concatenate([w_in[:, :SPLIT_KV], kr_full, kr_swap, w_in[:, SPLIT_KR:]], axis=1)

    wq = w_uq.reshape(MLA_Q_RANK, MLA_HEADS, MLA_QK_DIM)
    zq = lambda n: jnp.zeros((MLA_Q_RANK, MLA_HEADS, n), w_uq.dtype)
    wq_full = jnp.concatenate([wq, zq(HEAD_PAD - MLA_QK_DIM)], axis=2)
    wq_swap = jnp.concatenate([zq(MLA_NOPE_DIM), -wq[:, :, MLA_NOPE_DIM + half:],
                               wq[:, :, MLA_NOPE_DIM:MLA_NOPE_DIM + half], zq(HEAD_PAD - MLA_QK_DIM)], axis=2)
    wq_ext = jnp.concatenate([wq_full.reshape(MLA_Q_RANK, QK_PAD), wq_swap.reshape(MLA_Q_RANK, QK_PAD)], axis=1)

    wk = w_uk.reshape(MLA_KV_RANK, MLA_HEADS, MLA_NOPE_DIM)
    wk_full = jnp.concatenate([wk, jnp.zeros((MLA_KV_RANK, MLA_HEADS, HEAD_PAD - MLA_NOPE_DIM), w_uk.dtype)], axis=2)
    wv = w_uv.reshape(MLA_KV_RANK, MLA_HEADS, MLA_V_DIM)
    wv_full = jnp.concatenate([wv, jnp.zeros((MLA_KV_RANK, MLA_HEADS, HEAD_PAD - MLA_V_DIM), w_uv.dtype)], axis=2)
    wkv_ext = jnp.concatenate([wk_full.reshape(MLA_KV_RANK, QK_PAD), wv_full.reshape(MLA_KV_RANK, QK_PAD)], axis=1)
    return w_in_ext.astype(BF16), wq_ext.astype(BF16), wkv_ext.astype(BF16)


def _rope_freq_column():
    inv_freq = ROPE_BASE ** (-jnp.arange(0, MLA_ROPE_DIM, 2, dtype=F32) / MLA_ROPE_DIM)
    return inv_freq[:, None]


def _value_one_lanes():
    one_hot = (jnp.arange(HEAD_PAD) == MLA_V_DIM).astype(F32)
    return jnp.tile(one_hot, MLA_HEADS)[None, :]


def kernel(x, mem, positions, w_in, q_norm_g, w_uq, kv_norm_g, w_uk, w_uv, conv_w, conv_b, conv_ln_g, conv_ln_b,
           w_o, ln1_g, ln1_b, xa_w_q, xa_w_k, xa_w_v, xa_w_o, ln2_g, ln2_b, w_router, w_gate, w_up, w_down,
           ln3_g, ln3_b):
    assert w_in.shape[0] == DEPTH == 1
    b, s, d = x.shape
    cap = CAPACITY_FACTOR * s // N_EXPERTS
    tm = 512

    w_in_ext, wq_ext, wkv_ext = _extend_weights(w_in[0], w_uq[0], w_uk[0], w_uv[0])
    posf = positions.astype(F32)[:, None, :]
    q, k, v, u_pre = _front(x, posf, w_in_ext, q_norm_g, wq_ext, kv_norm_g, wkv_ext, _rope_freq_column(),
                             _value_one_lanes(), tm)
    attn = _attention(q, k, v, 1024)
    kx, vx = _mem_kv(mem, xa_w_k[0].astype(BF16), xa_w_v[0].astype(BF16))
    h2, h2b, aff = _mid(u_pre, attn, x, conv_w[0], conv_b, conv_ln_g, conv_ln_b, w_o[0].astype(BF16), ln1_g, ln1_b,
                        xa_w_q[0].astype(BF16), kx, vx, xa_w_o[0].astype(BF16), ln2_g, ln2_b,
                        _split_router(w_router[0]), tm)

    aff2d = aff.reshape(b * N_EXPERTS, s)
    slot, slot_t, off_tab = _topk(aff2d, cap)
    n_j = s // ROUTE_CHUNK
    base, over = _route_tables(off_tab[:, :n_j + 1], b, N_EXPERTS, cap)
    base_flat, over_flat = base.reshape(-1), over.reshape(-1)
    xs, gate = _dispatch(base_flat, over_flat, h2b, slot, aff2d, N_EXPERTS, cap)
    y = _experts(xs.reshape(N_EXPERTS, b * cap, d), gate.reshape(N_EXPERTS, b * cap, 1),
                 w_gate[0], w_up[0], w_down[0])
    base_t = base.T.astype(F32).reshape(n_j, 1, b * N_EXPERTS)
    return _combine(base_flat, over_flat, slot_t, base_t, y.reshape(N_EXPERTS + 1, b, cap, d), h2, ln3_g, ln3_b, cap)
```

```python
import functools
import math

import jax
import jax.numpy as jnp
from jax import lax
from jax.experimental import pallas as pl
from jax.experimental.pallas import tpu as pltpu

F32 = jnp.float32
BF16 = jnp.bfloat16

D_MODEL = 1024
MLA_HEADS = 8
MLA_NOPE_DIM = 64
MLA_ROPE_DIM = 32
MLA_QK_DIM = MLA_NOPE_DIM + MLA_ROPE_DIM
MLA_V_DIM = 64
MLA_Q_RANK = 256
MLA_KV_RANK = 128
MLA_WIDTH = MLA_HEADS * MLA_V_DIM
CONV_CH = D_MODEL - MLA_WIDTH
CONV_WIDTH = 31
CONV_PAD = (CONV_WIDTH - 1) // 2
ROPE_BASE = 10000.0
MEM_HEADS = 4
MEM_HEAD_DIM = D_MODEL // MEM_HEADS
N_EXPERTS = 16
EXPERT_FF = 2048
CAPACITY_FACTOR = 2
NORM_EPS = 1e-5
DEPTH = 1
DEEPNORM_ALPHA = (2.0 * DEPTH) ** 0.25
SPLIT_Q = MLA_Q_RANK
SPLIT_KV = SPLIT_Q + MLA_KV_RANK
SPLIT_KR = SPLIT_KV + MLA_ROPE_DIM

HEAD_PAD = 128
QK_PAD = MLA_HEADS * HEAD_PAD
OFF_CQ = 0
OFF_CKV = OFF_CQ + MLA_Q_RANK
OFF_KR = OFF_CKV + MLA_KV_RANK
OFF_KRS = OFF_KR + HEAD_PAD
OFF_A = OFF_KRS + HEAD_PAD
OFF_G = OFF_A + CONV_CH
IN_EXT = OFF_G + CONV_CH

SUBLANES = 8
LANES = 128
BF16_ROWS = 16
ROUTE_CHUNK = 256
ROUTE_WIN = 64
HALO = 16
VMEM_LIMIT = 56 * 1024 * 1024


def _cparams(sem):
    return pltpu.CompilerParams(dimension_semantics=sem, vmem_limit_bytes=VMEM_LIMIT)


def _layer_norm(v, g, b):
    mu = jnp.mean(v, axis=-1, keepdims=True)
    d = v - mu
    var = jnp.mean(d * d, axis=-1, keepdims=True)
    return d * lax.rsqrt(var + NORM_EPS) * g + b


def _rms_norm(v, g):
    return v * lax.rsqrt(jnp.mean(v * v, axis=-1, keepdims=True) + NORM_EPS) * g


def _sigmoid(v):
    return 1.0 / (1.0 + jnp.exp(-v))


def _dot(a, b):
    return jnp.dot(a, b, preferred_element_type=F32)


def _dot_nt(a, b, precision=None):
    return lax.dot_general(a, b, (((1,), (1,)), ((), ())), preferred_element_type=F32, precision=precision)


def _front_kernel(x_ref, pos_ref, win_ref, qg_ref, wq_ref, kvg_ref, wkv_ref, invf_ref, vone_ref,
                  q_out, k_out, v_out, u_out):
    x = x_ref[0].astype(BF16)
    hc = _dot(x, win_ref[...])
    cqn = _rms_norm(hc[:, OFF_CQ:OFF_CKV], qg_ref[...])
    qq = _dot(cqn.astype(BF16), wq_ref[...])
    ckvn = _rms_norm(hc[:, OFF_CKV:OFF_KR], kvg_ref[...])
    kv = _dot(ckvn.astype(BF16), wkv_ref[...])
    tm = x_ref.shape[1]
    ang = invf_ref[...] * pos_ref[0]
    cos_h, sin_h = jnp.cos(ang), jnp.sin(ang)
    ones = lambda n: jnp.ones((n, tm), F32)
    zeros = lambda n: jnp.zeros((n, tm), F32)
    cos = jnp.concatenate([ones(MLA_NOPE_DIM), cos_h, cos_h, ones(HEAD_PAD - MLA_QK_DIM)], axis=0).T
    sin = jnp.concatenate([zeros(MLA_NOPE_DIM), sin_h, sin_h, zeros(HEAD_PAD - MLA_QK_DIM)], axis=0).T
    krot = hc[:, OFF_KR:OFF_KRS] * cos + hc[:, OFF_KRS:OFF_A] * sin
    scale = math.log2(math.e) / math.sqrt(MLA_QK_DIM)
    for h in range(MLA_HEADS):
        lo, hi = h * HEAD_PAD, (h + 1) * HEAD_PAD
        qh = (qq[:, lo:hi] * cos + qq[:, QK_PAD + lo:QK_PAD + hi] * sin) * scale
        q_out[0, :, lo:hi] = qh.astype(BF16)
        k_out[0, :, lo:hi] = (kv[:, lo:hi] + krot).astype(BF16)
    v_out[0] = (kv[:, QK_PAD:] + vone_ref[...]).astype(BF16)
    u_out[0] = hc[:, OFF_A:OFF_G] * _sigmoid(hc[:, OFF_G:IN_EXT])


def _front(x, posf, w_in_ext, qg, wq_ext, kvg, wkv_ext, invf, vone, tm):
    b, s, d = x.shape
    const = lambda bi, i: (0, 0)
    return pl.pallas_call(
        _front_kernel,
        grid=(b, s // tm),
        in_specs=[
            pl.BlockSpec((1, tm, d), lambda bi, i: (bi, i, 0)),
            pl.BlockSpec((1, 1, tm), lambda bi, i: (bi, 0, i)),
            pl.BlockSpec(w_in_ext.shape, const),
            pl.BlockSpec(qg.shape, const),
            pl.BlockSpec(wq_ext.shape, const),
            pl.BlockSpec(kvg.shape, const),
            pl.BlockSpec(wkv_ext.shape, const),
            pl.BlockSpec(invf.shape, const),
            pl.BlockSpec(vone.shape, const),
        ],
        out_specs=[
            pl.BlockSpec((1, tm, QK_PAD), lambda bi, i: (bi, i, 0)),
            pl.BlockSpec((1, tm, QK_PAD), lambda bi, i: (bi, i, 0)),
            pl.BlockSpec((1, tm, QK_PAD), lambda bi, i: (bi, i, 0)),
            pl.BlockSpec((1, tm, CONV_CH), lambda bi, i: (bi, i, 0)),
        ],
        out_shape=[
            jax.ShapeDtypeStruct((b, s, QK_PAD), BF16),
            jax.ShapeDtypeStruct((b, s, QK_PAD), BF16),
            jax.ShapeDtypeStruct((b, s, QK_PAD), BF16),
            jax.ShapeDtypeStruct((b, s, CONV_CH), F32),
        ],
        compiler_params=_cparams(("parallel", "parallel")),
        name="front",
    )(x, posf, w_in_ext, qg, wq_ext, kvg, wkv_ext, invf, vone)


def _attn_kernel(q_ref, k_ref, v_ref, o_ref):
    for h in range(MLA_HEADS):
        lo, hi = h * HEAD_PAD, (h + 1) * HEAD_PAD
        sc = _dot_nt(q_ref[0, :, lo:hi], k_ref[0, :, lo:hi])
        p = jnp.exp2(sc - jnp.max(sc, axis=-1, keepdims=True))
        pv = _dot(p.astype(BF16), v_ref[0, :, lo:hi])
        o = pv[:, :MLA_V_DIM] / pv[:, MLA_V_DIM:MLA_V_DIM + 1]
        o_ref[0, :, h * MLA_V_DIM:(h + 1) * MLA_V_DIM] = o.astype(BF16)


def _attention(q, k, v, tq):
    b, s, _ = q.shape
    return pl.pallas_call(
        _attn_kernel,
        grid=(b, s // tq),
        in_specs=[
            pl.BlockSpec((1, tq, QK_PAD), lambda bi, i: (bi, i, 0)),
            pl.BlockSpec((1, s, QK_PAD), lambda bi, i: (bi, 0, 0)),
            pl.BlockSpec((1, s, QK_PAD), lambda bi, i: (bi, 0, 0)),
        ],
        out_specs=pl.BlockSpec((1, tq, MLA_WIDTH), lambda bi, i: (bi, i, 0)),
        out_shape=jax.ShapeDtypeStruct((b, s, MLA_WIDTH), BF16),
        compiler_params=_cparams(("parallel", "parallel")),
        name="attn",
    )(q, k, v)


def _mem_kv_kernel(mem_ref, wk_ref, wv_ref, k_out, v_out):
    m = mem_ref[0].astype(BF16)
    k_out[0] = _dot(m, wk_ref[...]).astype(BF16)
    v_out[0] = _dot(m, wv_ref[...]).astype(BF16)


def _mem_kv(mem, wk, wv):
    b, m, d = mem.shape
    const = lambda bi: (0, 0)
    return pl.pallas_call(
        _mem_kv_kernel,
        grid=(b,),
        in_specs=[
            pl.BlockSpec((1, m, d), lambda bi: (bi, 0, 0)),
            pl.BlockSpec(wk.shape, const),
            pl.BlockSpec(wv.shape, const),
        ],
        out_specs=[pl.BlockSpec((1, m, d), lambda bi: (bi, 0, 0))] * 2,
        out_shape=[jax.ShapeDtypeStruct((b, m, d), BF16)] * 2,
        compiler_params=_cparams(("parallel",)),
        name="mem_kv",
    )(mem, wk, wv)


CONV_SUB = 64


def _mid_kernel(ucur_ref, uprev_ref, unext_ref, attn_ref, x_ref, cw_ref, cb_ref, cg_ref, cbeta_ref, wo_ref,
                g1_ref, b1_ref, wq_ref, kx_ref, vx_ref, xwo_ref, g2_ref, b2_ref, wr_ref,
                h2_out, h2b_out, aff_out, win_ref, shift_ref, u_scr, o_scr):
    tm = ucur_ref.shape[1]
    i = pl.program_id(1)
    last = pl.num_programs(1) - 1
    win_ref[0:HALO, :] = jnp.where(i > 0, uprev_ref[0], 0.0)
    win_ref[HALO:HALO + tm, :] = ucur_ref[0]
    win_ref[HALO + tm:2 * HALO + tm, :] = jnp.where(i < last, unext_ref[0], 0.0)
    span = tm + 2 * HALO - SUBLANES
    shift_ref[0] = win_ref[...]
    for j in range(1, SUBLANES):
        shift_ref[j, 0:span, :] = win_ref[j:j + span, :]

    for r in range(tm // CONV_SUB):
        acc = jnp.broadcast_to(cb_ref[...], (CONV_SUB, CONV_CH))
        for t in range(CONV_WIDTH):
            off = HALO - CONV_PAD + t
            row = r * CONV_SUB + (off // SUBLANES) * SUBLANES
            acc = acc + shift_ref[off % SUBLANES, row:row + CONV_SUB, :] * cw_ref[t:t + 1, :]
        y = _layer_norm(acc, cg_ref[...], cbeta_ref[...])
        u_scr[r * CONV_SUB:(r + 1) * CONV_SUB, :] = (y * _sigmoid(y)).astype(BF16)
    mix = _dot(attn_ref[0], wo_ref[0:MLA_WIDTH, :]) + _dot(u_scr[...], wo_ref[MLA_WIDTH:, :])
    h1 = _layer_norm(DEEPNORM_ALPHA * x_ref[0] + mix, g1_ref[...], b1_ref[...])
    q = (_dot(h1.astype(BF16), wq_ref[...]) * (1.0 / math.sqrt(MEM_HEAD_DIM))).astype(BF16)
    for h in range(MEM_HEADS):
        lo, hi = h * MEM_HEAD_DIM, (h + 1) * MEM_HEAD_DIM
        sc = _dot_nt(q[:, lo:hi], kx_ref[0, :, lo:hi])
        p = jnp.exp(sc - jnp.max(sc, axis=-1, keepdims=True))
        l = jnp.sum(p, axis=-1, keepdims=True)
        o_scr[:, lo:hi] = (_dot(p.astype(BF16), vx_ref[0, :, lo:hi]) / l).astype(BF16)
    xa = _dot(o_scr[...], xwo_ref[...])
    h2 = _layer_norm(DEEPNORM_ALPHA * h1 + xa, g2_ref[...], b2_ref[...])
    h2_out[0] = h2
    h2_hi = h2.astype(BF16)
    h2b_out[0] = h2_hi
    h2_lo = (h2 - h2_hi.astype(F32)).astype(BF16)
    hi_terms = _dot(h2_hi, wr_ref[...])
    logits = hi_terms[:, :LANES] + hi_terms[:, LANES:] + _dot(h2_lo, wr_ref[:, :LANES])
    lt = logits.T[0:aff_out.shape[1], :]
    ex = jnp.exp(lt - jnp.max(lt, axis=0, keepdims=True))
    aff_out[0] = ex / jnp.sum(ex, axis=0, keepdims=True)


def _mid(u_pre, attn, x, conv_w, conv_b, conv_g, conv_beta, w_o, g1, b1, wq, kx, vx, xwo, g2, b2, wr_split, tm):
    b, s, d = x.shape
    m = kx.shape[1]
    nh = tm // HALO
    const = lambda bi, i: (0, 0)
    tile = lambda w: pl.BlockSpec((1, tm, w), lambda bi, i: (bi, i, 0))
    whole = lambda arr: pl.BlockSpec(arr.shape, const)
    return pl.pallas_call(
        _mid_kernel,
        grid=(b, s // tm),
        in_specs=[
            tile(CONV_CH),
            pl.BlockSpec((1, HALO, CONV_CH), lambda bi, i: (bi, jnp.maximum(i * nh - 1, 0), 0)),
            pl.BlockSpec((1, HALO, CONV_CH), lambda bi, i: (bi, jnp.minimum((i + 1) * nh, s // HALO - 1), 0)),
            tile(MLA_WIDTH),
            tile(d),
            whole(conv_w), whole(conv_b), whole(conv_g), whole(conv_beta), whole(w_o), whole(g1), whole(b1),
            whole(wq),
            pl.BlockSpec((1, m, d), lambda bi, i: (bi, 0, 0)),
            pl.BlockSpec((1, m, d), lambda bi, i: (bi, 0, 0)),
            whole(xwo), whole(g2), whole(b2), whole(wr_split),
        ],
        out_specs=[
            tile(d),
            tile(d),
            pl.BlockSpec((1, N_EXPERTS, tm), lambda bi, i: (bi, 0, i)),
        ],
        out_shape=[
            jax.ShapeDtypeStruct((b, s, d), F32),
            jax.ShapeDtypeStruct((b, s, d), BF16),
            jax.ShapeDtypeStruct((b, N_EXPERTS, s), F32),
        ],
        scratch_shapes=[
            pltpu.VMEM((tm + 2 * HALO, CONV_CH), F32),
            pltpu.VMEM((SUBLANES, tm + 2 * HALO, CONV_CH), F32),
            pltpu.VMEM((tm, CONV_CH), BF16),
            pltpu.VMEM((tm, d), BF16),
        ],
        compiler_params=_cparams(("parallel", "parallel")),
        name="mid",
    )(u_pre, u_pre, u_pre, attn, x, conv_w, conv_b, conv_g, conv_beta, w_o, g1, b1, wq, kx, vx, xwo, g2, b2, wr_split)


def _split_router(w_router):
    hi = w_router.astype(BF16)
    lo = (w_router - hi.astype(F32)).astype(BF16)
    pad = jnp.zeros((w_router.shape[0], LANES - w_router.shape[1]), BF16)
    return jnp.concatenate([hi, pad, lo, pad], axis=1)


BISECT_STEPS_PER_CHECK = 4


def _topk_kernel(aff_ref, slot_out, slot_t_out, off_out, *, cap):
    aff = aff_ref[...]
    rows, s = aff.shape
    capf = jnp.float32(cap)

    def not_done(carry):
        return carry[2] > 0

    def halve(_, bounds):
        lo, hi = bounds
        mid = 0.5 * (lo + hi)
        take = jnp.sum(jnp.where(aff >= mid, 1.0, 0.0), axis=1, keepdims=True) >= capf
        return jnp.where(take, mid, lo), jnp.where(take, hi, mid)

    def bisect(carry):
        lo, hi = lax.fori_loop(0, BISECT_STEPS_PER_CHECK, halve, carry[:2])
        smallest_in = jnp.min(jnp.where(aff >= lo, aff, jnp.inf), axis=1, keepdims=True)
        largest_in = jnp.max(jnp.where(aff < hi, aff, -jnp.inf), axis=1, keepdims=True)
        open_rows = jnp.sum(jnp.where(smallest_in == largest_in, 0.0, 1.0))
        return lo, hi, open_rows.astype(jnp.int32)

    _, hi, _ = lax.while_loop(not_done, bisect,
                              (jnp.zeros((rows, 1), F32), jnp.full((rows, 1), 2.0, F32), jnp.int32(1)))
    th = jnp.max(jnp.where(aff < hi, aff, -jnp.inf), axis=1, keepdims=True)
    gt = aff > th
    eq = aff == th
    n_gt = jnp.sum(jnp.where(gt, 1.0, 0.0), axis=1, keepdims=True)
    tri = jnp.where(lax.broadcasted_iota(jnp.int32, (s, s), 0) < lax.broadcasted_iota(jnp.int32, (s, s), 1),
                    1.0, 0.0).astype(BF16)
    tie_rank = _dot(jnp.where(eq, 1.0, 0.0).astype(BF16), tri)
    sel = jnp.logical_or(gt, jnp.logical_and(eq, tie_rank < (capf - n_gt)))
    pos = _dot(jnp.where(sel, 1.0, 0.0).astype(BF16), tri)
    slot = jnp.where(sel, pos, -1.0)
    slot_out[...] = slot
    slot_t_out[...] = slot.T
    lanes = off_out.shape[1]
    before = jnp.where(lax.broadcasted_iota(jnp.int32, (s, lanes), 0)
                       < ROUTE_CHUNK * lax.broadcasted_iota(jnp.int32, (s, lanes), 1), 1.0, 0.0).astype(BF16)
    off_out[...] = _dot(jnp.where(sel, 1.0, 0.0).astype(BF16), before)


def _topk(aff2d, cap):
    rows, s = aff2d.shape
    return pl.pallas_call(
        functools.partial(_topk_kernel, cap=cap),
        out_shape=[
            jax.ShapeDtypeStruct((rows, s), F32),
            jax.ShapeDtypeStruct((s, rows), F32),
            jax.ShapeDtypeStruct((rows, LANES), F32),
        ],
        compiler_params=pltpu.CompilerParams(vmem_limit_bytes=VMEM_LIMIT),
        name="topk",
    )(aff2d)


def _route_tables(off_tab, n_b, n_e, cap):
    n_j = off_tab.shape[1] - 1
    off = off_tab.astype(jnp.int32)
    base = jnp.minimum((off[:, :n_j] // BF16_ROWS) * BF16_ROWS, cap - ROUTE_WIN)
    over = jnp.any((off[:, 1:] - base > ROUTE_WIN).reshape(n_b, n_e, n_j), axis=1)
    return base, over.astype(jnp.int32)


def _dispatch_kernel(base_ref, over_ref, h2b_ref, slot_ref, aff_ref, xs_out, gate_out, p_scr, gate_scr, *, cap):
    bi = pl.program_id(0)
    j = pl.program_id(1)
    n_j = pl.num_programs(1)
    n_e, t = slot_ref.shape
    h = h2b_ref[0]

    @pl.when(j == 0)
    def _():
        xs_out[...] = jnp.zeros(xs_out.shape, xs_out.dtype)
        gate_scr[...] = jnp.zeros(gate_scr.shape, gate_scr.dtype)

    @pl.when(over_ref[bi * n_j + j] == 0)
    def _():
        w_iota = lax.broadcasted_iota(jnp.int32, (ROUTE_WIN, t), 0).astype(F32)
        bases = []
        for e in range(n_e):
            base = pl.multiple_of(base_ref[(bi * n_e + e) * n_j + j], BF16_ROWS)
            hit = slot_ref[e:e + 1, :] == w_iota + base.astype(F32)
            p_scr[e * ROUTE_WIN:(e + 1) * ROUTE_WIN, :] = jnp.where(hit, 1.0, 0.0).astype(BF16)
            gate_scr[e, pl.ds(base, ROUTE_WIN), :] += jnp.sum(jnp.where(hit, aff_ref[e:e + 1, :], 0.0),
                                                              axis=1, keepdims=True)
            bases.append(base)
        picked = _dot(p_scr[...], h)
        for e in range(n_e):
            rows = pl.ds(bases[e], ROUTE_WIN)
            xs_out[e, 0, rows, :] += picked[e * ROUTE_WIN:(e + 1) * ROUTE_WIN, :].astype(BF16)

    @pl.when(over_ref[bi * n_j + j] != 0)
    def _():
        c_iota = lax.broadcasted_iota(jnp.int32, (cap, t), 0).astype(F32)
        for e in range(n_e):
            hit = slot_ref[e:e + 1, :] == c_iota
            xs_out[e, 0] += _dot(jnp.where(hit, 1.0, 0.0).astype(BF16), h).astype(BF16)
            gate_scr[e] += jnp.sum(jnp.where(hit, aff_ref[e:e + 1, :], 0.0), axis=1, keepdims=True)

    @pl.when(j == n_j - 1)
    def _():
        eye = lax.broadcasted_iota(jnp.int32, (cap, cap), 0) == lax.broadcasted_iota(jnp.int32, (cap, cap), 1)
        for e in range(n_e):
            gate_out[e, 0] = jnp.sum(jnp.where(eye, gate_scr[e], 0.0), axis=0, keepdims=True)


def _dispatch(base_flat, over_flat, h2b, slot2d, aff2d, n_e, cap):
    b, s, d = h2b.shape
    t = ROUTE_CHUNK
    grid_spec = pltpu.PrefetchScalarGridSpec(
        num_scalar_prefetch=2,
        grid=(b, s // t),
        in_specs=[
            pl.BlockSpec((1, t, d), lambda bi, j, *_: (bi, j, 0)),
            pl.BlockSpec((n_e, t), lambda bi, j, *_: (bi, j)),
            pl.BlockSpec((n_e, t), lambda bi, j, *_: (bi, j)),
        ],
        out_specs=[
            pl.BlockSpec((n_e, 1, cap, d), lambda bi, j, *_: (0, bi, 0, 0)),
            pl.BlockSpec((n_e, 1, 1, cap), lambda bi, j, *_: (0, bi, 0, 0)),
        ],
        scratch_shapes=[pltpu.VMEM((n_e * ROUTE_WIN, t), BF16), pltpu.VMEM((n_e, cap, 1), F32)],
    )
    return pl.pallas_call(
        functools.partial(_dispatch_kernel, cap=cap),
        grid_spec=grid_spec,
        out_shape=[
            jax.ShapeDtypeStruct((n_e, b, cap, d), BF16),
            jax.ShapeDtypeStruct((n_e, b, 1, cap), F32),
        ],
        compiler_params=_cparams(("parallel", "arbitrary")),
        name="dispatch",
    )(base_flat, over_flat, h2b, slot2d, aff2d)


FFN_ROWS = 512
FFN_CHUNK = 512


def _experts_kernel(xs_ref, gate_ref, wg_ref, wu_ref, wd_ref, y_out, wg_s, wu_s, wd_s, hid_s):
    e = pl.program_id(0)
    f = pl.program_id(1)
    n_chunks = wg_s.shape[1]

    def stage():
        slot = e % 2
        wg_s[slot, f] = wg_ref[0].astype(BF16)
        wu_s[slot, f] = wu_ref[0].astype(BF16)
        wd_s[slot, f] = wd_ref[0].astype(BF16)

    @pl.when(e == 0)
    def _():
        stage()
        y_out[0] = jnp.zeros(y_out.shape[1:], y_out.dtype)

    @pl.when(e > 0)
    def _():
        stage()
        slot = (e - 1) % 2
        xs = xs_ref[0]
        for c in range(n_chunks):
            g = _dot(xs, wg_s[slot, c])
            u = _dot(xs, wu_s[slot, c])
            hid_s[:, c * FFN_CHUNK:(c + 1) * FFN_CHUNK] = (g * _sigmoid(g) * u).astype(BF16)
        y = _dot(hid_s[:, 0:FFN_CHUNK], wd_s[slot, 0])
        for c in range(1, n_chunks):
            y = y + _dot(hid_s[:, c * FFN_CHUNK:(c + 1) * FFN_CHUNK], wd_s[slot, c])
        rows = xs.shape[0]
        eye = lax.broadcasted_iota(jnp.int32, (rows, rows), 0) == lax.broadcasted_iota(jnp.int32, (rows, rows), 1)
        gate_col = jnp.sum(jnp.where(eye, gate_ref[0], 0.0), axis=1, keepdims=True)
        y_out[0] = (y * gate_col).astype(BF16)


def _experts(xs, gate, w_gate, w_up, w_down):
    e, n, d = xs.shape
    ff = w_gate.shape[2]
    n_chunks = ff // FFN_CHUNK
    assert n // FFN_ROWS == n_chunks
    prev = lambda ei, fi: (jnp.maximum(ei - 1, 0), fi, 0)
    cur = lambda ei: jnp.minimum(ei, e - 1)
    return pl.pallas_call(
        _experts_kernel,
        grid=(e + 1, n_chunks),
        in_specs=[
            pl.BlockSpec((1, FFN_ROWS, d), prev),
            pl.BlockSpec((1, 1, FFN_ROWS), lambda ei, fi: (jnp.maximum(ei - 1, 0), 0, fi)),
            pl.BlockSpec((1, d, FFN_CHUNK), lambda ei, fi: (cur(ei), 0, fi)),
            pl.BlockSpec((1, d, FFN_CHUNK), lambda ei, fi: (cur(ei), 0, fi)),
            pl.BlockSpec((1, FFN_CHUNK, d), lambda ei, fi: (cur(ei), fi, 0)),
        ],
        out_specs=pl.BlockSpec((1, FFN_ROWS, d), lambda ei, fi: (jnp.where(ei == 0, e, ei - 1), fi, 0)),
        out_shape=jax.ShapeDtypeStruct((e + 1, n, d), BF16),
        scratch_shapes=[
            pltpu.VMEM((2, n_chunks, d, FFN_CHUNK), BF16),
            pltpu.VMEM((2, n_chunks, d, FFN_CHUNK), BF16),
            pltpu.VMEM((2, n_chunks, FFN_CHUNK, d), BF16),
            pltpu.VMEM((FFN_ROWS, ff), BF16),
        ],
        compiler_params=_cparams(("arbitrary", "arbitrary")),
        name="experts",
    )(xs, gate, w_gate, w_up, w_down)


def _combine_kernel(base_ref, over_ref, slot_t_ref, base_t_ref, y_ref, h2_ref, g3_ref, b3_ref, out_ref, ywin_scr,
                    *, cap):
    bi = pl.program_id(0)
    j = pl.program_id(1)
    n_j = pl.num_programs(1)
    n_e = y_ref.shape[0]
    rows = slot_t_ref.shape[1]

    def spread_cols(per_expert):
        width = n_e * per_expert
        col = lax.broadcasted_iota(jnp.int32, (rows, width), 1) // per_expert
        return jnp.where(lax.broadcasted_iota(jnp.int32, (rows, width), 0) == bi * n_e + col, 1.0, 0.0).astype(BF16)

    def lane_in_group(per_expert):
        return (lax.broadcasted_iota(jnp.int32, (1, n_e * per_expert), 1) % per_expert).astype(F32)

    def finish(ff):
        out_ref[0] = _layer_norm(DEEPNORM_ALPHA * h2_ref[0] + ff, g3_ref[...], b3_ref[...])

    slot_bf = slot_t_ref[...].astype(BF16)

    @pl.when(over_ref[bi * n_j + j] == 0)
    def _():
        spread = spread_cols(ROUTE_WIN)
        slot_wide = _dot(slot_bf, spread)
        base_wide = _dot(jnp.broadcast_to(base_t_ref[0], (SUBLANES, rows)).astype(BF16), spread)[0:1, :]
        onehot = jnp.where(slot_wide - base_wide == lane_in_group(ROUTE_WIN), 1.0, 0.0).astype(BF16)
        for e in range(n_e):
            base = pl.multiple_of(base_ref[(bi * n_e + e) * n_j + j], BF16_ROWS)
            ywin_scr[e * ROUTE_WIN:(e + 1) * ROUTE_WIN, :] = y_ref[e, 0, pl.ds(base, ROUTE_WIN), :]
        finish(_dot(onehot, ywin_scr[...]))

    @pl.when(over_ref[bi * n_j + j] != 0)
    def _():
        slot_wide = _dot(slot_bf, spread_cols(cap))
        onehot = jnp.where(slot_wide == lane_in_group(cap), 1.0, 0.0).astype(BF16)
        finish(_dot(onehot, y_ref[:, 0].reshape(n_e * cap, y_ref.shape[3])))


def _combine(base_flat, over_flat, slot_t, base_t, y4, h2, g3, b3, cap):
    b, s, d = h2.shape
    e = y4.shape[0] - 1
    t = ROUTE_CHUNK
    const = lambda bi, j, *_: (0, 0)
    grid_spec = pltpu.PrefetchScalarGridSpec(
        num_scalar_prefetch=2,
        grid=(b, s // t),
        in_specs=[
            pl.BlockSpec((t, slot_t.shape[1]), lambda bi, j, *_: (j, 0)),
            pl.BlockSpec((1, 1, base_t.shape[2]), lambda bi, j, *_: (j, 0, 0)),
            pl.BlockSpec((e, 1, cap, d), lambda bi, j, *_: (0, bi, 0, 0)),
            pl.BlockSpec((1, t, d), lambda bi, j, *_: (bi, j, 0)),
            pl.BlockSpec(g3.shape, const),
            pl.BlockSpec(b3.shape, const),
        ],
        out_specs=pl.BlockSpec((1, t, d), lambda bi, j, *_: (bi, j, 0)),
        scratch_shapes=[pltpu.VMEM((e * ROUTE_WIN, d), BF16)],
    )
    return pl.pallas_call(
        functools.partial(_combine_kernel, cap=cap),
        grid_spec=grid_spec,
        out_shape=jax.ShapeDtypeStruct((b, s, d), F32),
        compiler_params=_cparams(("parallel", "parallel")),
        name="combine",
    )(base_flat, over_flat, slot_t, base_t, y4, h2, g3, b3)


def _extend_weights(w_in, w_uq, w_uk, w_uv):
    half = MLA_ROPE_DIM // 2
    d = w_in.shape[0]
    kr = w_in[:, SPLIT_KV:SPLIT_KR]
    z = lambda n: jnp.zeros((d, n), w_in.dtype)
    kr_full = jnp.concatenate([z(MLA_NOPE_DIM), kr, z(HEAD_PAD - MLA_QK_DIM)], axis=1)
    kr_swap = jnp.concatenate([z(MLA_NOPE_DIM), -kr[:, half:], kr[:, :half], z(HEAD_PAD - MLA_QK_DIM)], axis=1)
    w_in_ext = jnp.concatenate([w_in[:, :SPLIT_KV], kr_full, kr_swap, w_in[:, SPLIT_KR:]], axis=1)

    wq = w_uq.reshape(MLA_Q_RANK, MLA_HEADS, MLA_QK_DIM)
    zq = lambda n: jnp.zeros((MLA_Q_RANK, MLA_HEADS, n), w_uq.dtype)
    wq_full = jnp.concatenate([wq, zq(HEAD_PAD - MLA_QK_DIM)], axis=2)
    wq_swap = jnp.concatenate([zq(MLA_NOPE_DIM), -wq[:, :, MLA_NOPE_DIM + half:],
                               wq[:, :, MLA_NOPE_DIM:MLA_NOPE_DIM + half], zq(HEAD_PAD - MLA_QK_DIM)], axis=2)
    wq_ext = jnp.concatenate([wq_full.reshape(MLA_Q_RANK, QK_PAD), wq_swap.reshape(MLA_Q_RANK, QK_PAD)], axis=1)

    wk = w_uk.reshape(MLA_KV_RANK, MLA_HEADS, MLA_NOPE_DIM)
    wk_full = jnp.concatenate([wk, jnp.zeros((MLA_KV_RANK, MLA_HEADS, HEAD_PAD - MLA_NOPE_DIM), w_uk.dtype)], axis=2)
    wv = w_uv.reshape(MLA_KV_RANK, MLA_HEADS, MLA_V_DIM)
    wv_full = jnp.concatenate([wv, jnp.zeros((MLA_KV_RANK, MLA_HEADS, HEAD_PAD - MLA_V_DIM), w_uv.dtype)], axis=2)
    wkv_ext = jnp.concatenate([wk_full.reshape(MLA_KV_RANK, QK_PAD), wv_full.reshape(MLA_KV_RANK, QK_PAD)], axis=1)
    return w_in_ext.astype(BF16), wq_ext.astype(BF16), wkv_ext.astype(BF16)


def _rope_freq_column():
    inv_freq = ROPE_BASE ** (-jnp.arange(0, MLA_ROPE_DIM, 2, dtype=F32) / MLA_ROPE_DIM)
    return inv_freq[:, None]


def _value_one_lanes():
    one_hot = (jnp.arange(HEAD_PAD) == MLA_V_DIM).astype(F32)
    return jnp.tile(one_hot, MLA_HEADS)[None, :]


def kernel(x, mem, positions, w_in, q_norm_g, w_uq, kv_norm_g, w_uk, w_uv, conv_w, conv_b, conv_ln_g, conv_ln_b,
           w_o, ln1_g, ln1_b, xa_w_q, xa_w_k, xa_w_v, xa_w_o, ln2_g, ln2_b, w_router, w_gate, w_up, w_down,
           ln3_g, ln3_b):
    assert w_in.shape[0] == DEPTH == 1
    b, s, d = x.shape
    cap = CAPACITY_FACTOR * s // N_EXPERTS
    tm = 512

    w_in_ext, wq_ext, wkv_ext = _extend_weights(w_in[0], w_uq[0], w_uk[0], w_uv[0])
    posf = positions.astype(F32)[:, None, :]
    q, k, v, u_pre = _front(x, posf, w_in_ext, q_norm_g, wq_ext, kv_norm_g, wkv_ext, _rope_freq_column(),
                             _value_one_lanes(), tm)
    attn = _attention(q, k, v, 1024)
    kx, vx = _mem_kv(mem, xa_w_k[0].astype(BF16), xa_w_v[0].astype(BF16))
    h2, h2b, aff = _mid(u_pre, attn, x, conv_w[0], conv_b, conv_ln_g, conv_ln_b, w_o[0].astype(BF16), ln1_g, ln1_b,
                        xa_w_q[0].astype(BF16), kx, vx, xa_w_o[0].astype(BF16), ln2_g, ln2_b,
                        _split_router(w_router[0]), tm)

    aff2d = aff.reshape(b * N_EXPERTS, s)
    slot, slot_t, off_tab = _topk(aff2d, cap)
    n_j = s // ROUTE_CHUNK
    base, over = _route_tables(off_tab[:, :n_j + 1], b, N_EXPERTS, cap)
    base_flat, over_flat = base.reshape(-1), over.reshape(-1)
    xs, gate = _dispatch(base_flat, over_flat, h2b, slot, aff2d, N_EXPERTS, cap)
    y = _experts(xs.reshape(N_EXPERTS, b * cap, d), gate.reshape(N_EXPERTS, 1, b * cap),
                 w_gate[0], w_up[0], w_down[0])
    base_t = base.T.astype(F32).reshape(n_j, 1, b * N_EXPERTS)
    return _combine(base_flat, over_flat, slot_t, base_t, y.reshape(N_EXPERTS + 1, b, cap, d), h2, ln3_g, ln3_b, cap)
```

```python
import functools
import math

import jax
import jax.numpy as jnp
from jax import lax
from jax.experimental import pallas as pl
from jax.experimental.pallas import tpu as pltpu

F32 = jnp.float32
BF16 = jnp.bfloat16

D_MODEL = 1024
MLA_HEADS = 8
MLA_NOPE_DIM = 64
MLA_ROPE_DIM = 32
MLA_QK_DIM = MLA_NOPE_DIM + MLA_ROPE_DIM
MLA_V_DIM = 64
MLA_Q_RANK = 256
MLA_KV_RANK = 128
MLA_WIDTH = MLA_HEADS * MLA_V_DIM
CONV_CH = D_MODEL - MLA_WIDTH
CONV_WIDTH = 31
CONV_PAD = (CONV_WIDTH - 1) // 2
ROPE_BASE = 10000.0
MEM_HEADS = 4
MEM_HEAD_DIM = D_MODEL // MEM_HEADS
N_EXPERTS = 16
EXPERT_FF = 2048
CAPACITY_FACTOR = 2
NORM_EPS = 1e-5
DEPTH = 1
DEEPNORM_ALPHA = (2.0 * DEPTH) ** 0.25
SPLIT_Q = MLA_Q_RANK
SPLIT_KV = SPLIT_Q + MLA_KV_RANK
SPLIT_KR = SPLIT_KV + MLA_ROPE_DIM

HEAD_PAD = 128
QK_PAD = MLA_HEADS * HEAD_PAD
OFF_CQ = 0
OFF_CKV = OFF_CQ + MLA_Q_RANK
OFF_KR = OFF_CKV + MLA_KV_RANK
OFF_KRS = OFF_KR + HEAD_PAD
OFF_A = OFF_KRS + HEAD_PAD
OFF_G = OFF_A + CONV_CH
IN_EXT = OFF_G + CONV_CH

SUBLANES = 8
LANES = 128
BF16_ROWS = 16
ROUTE_CHUNK = 256
ROUTE_WIN = 64
ROUTE_INNER = 4
HALO = 16
VMEM_LIMIT = 56 * 1024 * 1024


def _cparams(sem):
    return pltpu.CompilerParams(dimension_semantics=sem, vmem_limit_bytes=VMEM_LIMIT)


def _layer_norm(v, g, b):
    mu = jnp.mean(v, axis=-1, keepdims=True)
    d = v - mu
    var = jnp.mean(d * d, axis=-1, keepdims=True)
    return d * lax.rsqrt(var + NORM_EPS) * g + b


def _rms_norm(v, g):
    return v * lax.rsqrt(jnp.mean(v * v, axis=-1, keepdims=True) + NORM_EPS) * g


def _sigmoid(v):
    return 1.0 / (1.0 + jnp.exp(-v))


def _dot(a, b):
    return jnp.dot(a, b, preferred_element_type=F32)


def _dot_nt(a, b, precision=None):
    return lax.dot_general(a, b, (((1,), (1,)), ((), ())), preferred_element_type=F32, precision=precision)


def _front_kernel(x_ref, pos_ref, win_ref, qg_ref, wq_ref, kvg_ref, wkv_ref, invf_ref, vone_ref,
                  q_out, k_out, v_out, u_out):
    x = x_ref[0].astype(BF16)
    hc = _dot(x, win_ref[...])
    cqn = _rms_norm(hc[:, OFF_CQ:OFF_CKV], qg_ref[...])
    qq = _dot(cqn.astype(BF16), wq_ref[...])
    ckvn = _rms_norm(hc[:, OFF_CKV:OFF_KR], kvg_ref[...])
    kv = _dot(ckvn.astype(BF16), wkv_ref[...])
    tm = x_ref.shape[1]
    ang = invf_ref[...] * pos_ref[0]
    cos_h, sin_h = jnp.cos(ang), jnp.sin(ang)
    ones = lambda n: jnp.ones((n, tm), F32)
    zeros = lambda n: jnp.zeros((n, tm), F32)
    cos = jnp.concatenate([ones(MLA_NOPE_DIM), cos_h, cos_h, ones(HEAD_PAD - MLA_QK_DIM)], axis=0).T
    sin = jnp.concatenate([zeros(MLA_NOPE_DIM), sin_h, sin_h, zeros(HEAD_PAD - MLA_QK_DIM)], axis=0).T
    krot = hc[:, OFF_KR:OFF_KRS] * cos + hc[:, OFF_KRS:OFF_A] * sin
    scale = math.log2(math.e) / math.sqrt(MLA_QK_DIM)
    for h in range(MLA_HEADS):
        lo, hi = h * HEAD_PAD, (h + 1) * HEAD_PAD
        qh = (qq[:, lo:hi] * cos + qq[:, QK_PAD + lo:QK_PAD + hi] * sin) * scale
        q_out[0, :, lo:hi] = qh.astype(BF16)
        k_out[0, :, lo:hi] = (kv[:, lo:hi] + krot).astype(BF16)
    v_out[0] = (kv[:, QK_PAD:] + vone_ref[...]).astype(BF16)
    u_out[0] = hc[:, OFF_A:OFF_G] * _sigmoid(hc[:, OFF_G:IN_EXT])


def _front(x, posf, w_in_ext, qg, wq_ext, kvg, wkv_ext, invf, vone, tm):
    b, s, d = x.shape
    const = lambda bi, i: (0, 0)
    return pl.pallas_call(
        _front_kernel,
        grid=(b, s // tm),
        in_specs=[
            pl.BlockSpec((1, tm, d), lambda bi, i: (bi, i, 0)),
            pl.BlockSpec((1, 1, tm), lambda bi, i: (bi, 0, i)),
            pl.BlockSpec(w_in_ext.shape, const),
            pl.BlockSpec(qg.shape, const),
            pl.BlockSpec(wq_ext.shape, const),
            pl.BlockSpec(kvg.shape, const),
            pl.BlockSpec(wkv_ext.shape, const),
            pl.BlockSpec(invf.shape, const),
            pl.BlockSpec(vone.shape, const),
        ],
        out_specs=[
            pl.BlockSpec((1, tm, QK_PAD), lambda bi, i: (bi, i, 0)),
            pl.BlockSpec((1, tm, QK_PAD), lambda bi, i: (bi, i, 0)),
            pl.BlockSpec((1, tm, QK_PAD), lambda bi, i: (bi, i, 0)),
            pl.BlockSpec((1, tm, CONV_CH), lambda bi, i: (bi, i, 0)),
        ],
        out_shape=[
            jax.ShapeDtypeStruct((b, s, QK_PAD), BF16),
            jax.ShapeDtypeStruct((b, s, QK_PAD), BF16),
            jax.ShapeDtypeStruct((b, s, QK_PAD), BF16),
            jax.ShapeDtypeStruct((b, s, CONV_CH), F32),
        ],
        compiler_params=_cparams(("parallel", "parallel")),
        name="front",
    )(x, posf, w_in_ext, qg, wq_ext, kvg, wkv_ext, invf, vone)


def _attn_kernel(q_ref, k_ref, v_ref, o_ref):
    for h in range(MLA_HEADS):
        lo, hi = h * HEAD_PAD, (h + 1) * HEAD_PAD
        sc = _dot_nt(q_ref[0, :, lo:hi], k_ref[0, :, lo:hi])
        p = jnp.exp2(sc - jnp.max(sc, axis=-1, keepdims=True))
        pv = _dot(p.astype(BF16), v_ref[0, :, lo:hi])
        o = pv[:, :MLA_V_DIM] / pv[:, MLA_V_DIM:MLA_V_DIM + 1]
        o_ref[0, :, h * MLA_V_DIM:(h + 1) * MLA_V_DIM] = o.astype(BF16)


def _attention(q, k, v, tq):
    b, s, _ = q.shape
    return pl.pallas_call(
        _attn_kernel,
        grid=(b, s // tq),
        in_specs=[
            pl.BlockSpec((1, tq, QK_PAD), lambda bi, i: (bi, i, 0)),
            pl.BlockSpec((1, s, QK_PAD), lambda bi, i: (bi, 0, 0)),
            pl.BlockSpec((1, s, QK_PAD), lambda bi, i: (bi, 0, 0)),
        ],
        out_specs=pl.BlockSpec((1, tq, MLA_WIDTH), lambda bi, i: (bi, i, 0)),
        out_shape=jax.ShapeDtypeStruct((b, s, MLA_WIDTH), BF16),
        compiler_params=_cparams(("parallel", "parallel")),
        name="attn",
    )(q, k, v)


def _mem_kv_kernel(mem_ref, wk_ref, wv_ref, k_out, v_out):
    m = mem_ref[0].astype(BF16)
    k_out[0] = _dot(m, wk_ref[...]).astype(BF16)
    v_out[0] = _dot(m, wv_ref[...]).astype(BF16)


def _mem_kv(mem, wk, wv):
    b, m, d = mem.shape
    const = lambda bi: (0, 0)
    return pl.pallas_call(
        _mem_kv_kernel,
        grid=(b,),
        in_specs=[
            pl.BlockSpec((1, m, d), lambda bi: (bi, 0, 0)),
            pl.BlockSpec(wk.shape, const),
            pl.BlockSpec(wv.shape, const),
        ],
        out_specs=[pl.BlockSpec((1, m, d), lambda bi: (bi, 0, 0))] * 2,
        out_shape=[jax.ShapeDtypeStruct((b, m, d), BF16)] * 2,
        compiler_params=_cparams(("parallel",)),
        name="mem_kv",
    )(mem, wk, wv)


CONV_SUB = 64


def _mid_kernel(ucur_ref, uprev_ref, unext_ref, attn_ref, x_ref, cw_ref, cb_ref, cg_ref, cbeta_ref, wo_ref,
                g1_ref, b1_ref, wq_ref, kx_ref, vx_ref, xwo_ref, g2_ref, b2_ref, wr_ref,
                h2_out, h2b_out, aff_out, win_ref, shift_ref, u_scr, o_scr):
    tm = ucur_ref.shape[1]
    i = pl.program_id(1)
    last = pl.num_programs(1) - 1
    win_ref[0:HALO, :] = jnp.where(i > 0, uprev_ref[0], 0.0)
    win_ref[HALO:HALO + tm, :] = ucur_ref[0]
    win_ref[HALO + tm:2 * HALO + tm, :] = jnp.where(i < last, unext_ref[0], 0.0)
    span = tm + 2 * HALO - SUBLANES
    shift_ref[0] = win_ref[...]
    for j in range(1, SUBLANES):
        shift_ref[j, 0:span, :] = win_ref[j:j + span, :]

    for r in range(tm // CONV_SUB):
        acc = jnp.broadcast_to(cb_ref[...], (CONV_SUB, CONV_CH))
        for t in range(CONV_WIDTH):
            off = HALO - CONV_PAD + t
            row = r * CONV_SUB + (off // SUBLANES) * SUBLANES
            acc = acc + shift_ref[off % SUBLANES, row:row + CONV_SUB, :] * cw_ref[t:t + 1, :]
        y = _layer_norm(acc, cg_ref[...], cbeta_ref[...])
        u_scr[r * CONV_SUB:(r + 1) * CONV_SUB, :] = (y * _sigmoid(y)).astype(BF16)
    mix = _dot(attn_ref[0], wo_ref[0:MLA_WIDTH, :]) + _dot(u_scr[...], wo_ref[MLA_WIDTH:, :])
    h1 = _layer_norm(DEEPNORM_ALPHA * x_ref[0] + mix, g1_ref[...], b1_ref[...])
    q = (_dot(h1.astype(BF16), wq_ref[...]) * (1.0 / math.sqrt(MEM_HEAD_DIM))).astype(BF16)
    for h in range(MEM_HEADS):
        lo, hi = h * MEM_HEAD_DIM, (h + 1) * MEM_HEAD_DIM
        sc = _dot_nt(q[:, lo:hi], kx_ref[0, :, lo:hi])
        p = jnp.exp(sc - jnp.max(sc, axis=-1, keepdims=True))
        l = jnp.sum(p, axis=-1, keepdims=True)
        o_scr[:, lo:hi] = (_dot(p.astype(BF16), vx_ref[0, :, lo:hi]) / l).astype(BF16)
    xa = _dot(o_scr[...], xwo_ref[...])
    h2 = _layer_norm(DEEPNORM_ALPHA * h1 + xa, g2_ref[...], b2_ref[...])
    h2_out[0] = h2
    h2_hi = h2.astype(BF16)
    h2b_out[0] = h2_hi
    h2_lo = (h2 - h2_hi.astype(F32)).astype(BF16)
    hi_terms = _dot(h2_hi, wr_ref[...])
    logits = hi_terms[:, :LANES] + hi_terms[:, LANES:] + _dot(h2_lo, wr_ref[:, :LANES])
    lt = logits.T[0:aff_out.shape[1], :]
    ex = jnp.exp(lt - jnp.max(lt, axis=0, keepdims=True))
    aff_out[0] = ex / jnp.sum(ex, axis=0, keepdims=True)


def _mid(u_pre, attn, x, conv_w, conv_b, conv_g, conv_beta, w_o, g1, b1, wq, kx, vx, xwo, g2, b2, wr_split, tm):
    b, s, d = x.shape
    m = kx.shape[1]
    nh = tm // HALO
    const = lambda bi, i: (0, 0)
    tile = lambda w: pl.BlockSpec((1, tm, w), lambda bi, i: (bi, i, 0))
    whole = lambda arr: pl.BlockSpec(arr.shape, const)
    return pl.pallas_call(
        _mid_kernel,
        grid=(b, s // tm),
        in_specs=[
            tile(CONV_CH),
            pl.BlockSpec((1, HALO, CONV_CH), lambda bi, i: (bi, jnp.maximum(i * nh - 1, 0), 0)),
            pl.BlockSpec((1, HALO, CONV_CH), lambda bi, i: (bi, jnp.minimum((i + 1) * nh, s // HALO - 1), 0)),
            tile(MLA_WIDTH),
            tile(d),
            whole(conv_w), whole(conv_b), whole(conv_g), whole(conv_beta), whole(w_o), whole(g1), whole(b1),
            whole(wq),
            pl.BlockSpec((1, m, d), lambda bi, i: (bi, 0, 0)),
            pl.BlockSpec((1, m, d), lambda bi, i: (bi, 0, 0)),
            whole(xwo), whole(g2), whole(b2), whole(wr_split),
        ],
        out_specs=[
            tile(d),
            tile(d),
            pl.BlockSpec((1, N_EXPERTS, tm), lambda bi, i: (bi, 0, i)),
        ],
        out_shape=[
            jax.ShapeDtypeStruct((b, s, d), F32),
            jax.ShapeDtypeStruct((b, s, d), BF16),
            jax.ShapeDtypeStruct((b, N_EXPERTS, s), F32),
        ],
        scratch_shapes=[
            pltpu.VMEM((tm + 2 * HALO, CONV_CH), F32),
            pltpu.VMEM((SUBLANES, tm + 2 * HALO, CONV_CH), F32),
            pltpu.VMEM((tm, CONV_CH), BF16),
            pltpu.VMEM((tm, d), BF16),
        ],
        compiler_params=_cparams(("parallel", "parallel")),
        name="mid",
    )(u_pre, u_pre, u_pre, attn, x, conv_w, conv_b, conv_g, conv_beta, w_o, g1, b1, wq, kx, vx, xwo, g2, b2, wr_split)


def _split_router(w_router):
    hi = w_router.astype(BF16)
    lo = (w_router - hi.astype(F32)).astype(BF16)
    pad = jnp.zeros((w_router.shape[0], LANES - w_router.shape[1]), BF16)
    return jnp.concatenate([hi, pad, lo, pad], axis=1)


BISECT_STEPS_PER_CHECK = 4


def _topk_kernel(aff_ref, slot_out, slot_t_out, off_out, *, cap):
    aff = aff_ref[...]
    rows, s = aff.shape
    capf = jnp.float32(cap)

    def not_done(carry):
        return carry[2] > 0

    def halve(_, bounds):
        lo, hi = bounds
        mid = 0.5 * (lo + hi)
        take = jnp.sum(jnp.where(aff >= mid, 1.0, 0.0), axis=1, keepdims=True) >= capf
        return jnp.where(take, mid, lo), jnp.where(take, hi, mid)

    def bisect(carry):
        lo, hi = lax.fori_loop(0, BISECT_STEPS_PER_CHECK, halve, carry[:2])
        smallest_in = jnp.min(jnp.where(aff >= lo, aff, jnp.inf), axis=1, keepdims=True)
        largest_in = jnp.max(jnp.where(aff < hi, aff, -jnp.inf), axis=1, keepdims=True)
        open_rows = jnp.sum(jnp.where(smallest_in == largest_in, 0.0, 1.0))
        return lo, hi, open_rows.astype(jnp.int32)

    _, hi, _ = lax.while_loop(not_done, bisect,
                              (jnp.zeros((rows, 1), F32), jnp.full((rows, 1), 2.0, F32), jnp.int32(1)))
    th = jnp.max(jnp.where(aff < hi, aff, -jnp.inf), axis=1, keepdims=True)
    gt = aff > th
    eq = aff == th
    n_gt = jnp.sum(jnp.where(gt, 1.0, 0.0), axis=1, keepdims=True)
    tri = jnp.where(lax.broadcasted_iota(jnp.int32, (s, s), 0) < lax.broadcasted_iota(jnp.int32, (s, s), 1),
                    1.0, 0.0).astype(BF16)
    tie_rank = _dot(jnp.where(eq, 1.0, 0.0).astype(BF16), tri)
    sel = jnp.logical_or(gt, jnp.logical_and(eq, tie_rank < (capf - n_gt)))
    pos = _dot(jnp.where(sel, 1.0, 0.0).astype(BF16), tri)
    slot = jnp.where(sel, pos, -1.0)
    slot_out[...] = slot
    slot_t_out[...] = slot.T
    lanes = off_out.shape[1]
    before = jnp.where(lax.broadcasted_iota(jnp.int32, (s, lanes), 0)
                       < ROUTE_CHUNK * lax.broadcasted_iota(jnp.int32, (s, lanes), 1), 1.0, 0.0).astype(BF16)
    off_out[...] = _dot(jnp.where(sel, 1.0, 0.0).astype(BF16), before)


def _topk(aff2d, cap):
    rows, s = aff2d.shape
    return pl.pallas_call(
        functools.partial(_topk_kernel, cap=cap),
        out_shape=[
            jax.ShapeDtypeStruct((rows, s), F32),
            jax.ShapeDtypeStruct((s, rows), F32),
            jax.ShapeDtypeStruct((rows, LANES), F32),
        ],
        compiler_params=pltpu.CompilerParams(vmem_limit_bytes=VMEM_LIMIT),
        name="topk",
    )(aff2d)


def _route_tables(off_tab, n_b, n_e, cap):
    n_j = off_tab.shape[1] - 1
    off = off_tab.astype(jnp.int32)
    base = jnp.minimum((off[:, :n_j] // BF16_ROWS) * BF16_ROWS, cap - ROUTE_WIN)
    over = jnp.any((off[:, 1:] - base > ROUTE_WIN).reshape(n_b, n_e, n_j), axis=1)
    return base, over.astype(jnp.int32)


def _dispatch_kernel(base_ref, over_ref, h2b_ref, slot_ref, aff_ref, xs_out, gate_out, p_scr, gate_scr, *, cap):
    bi = pl.program_id(0)
    jo = pl.program_id(1)
    n_j = pl.num_programs(1) * ROUTE_INNER
    n_e, _, t = slot_ref.shape

    @pl.when(jo == 0)
    def _():
        xs_out[...] = jnp.zeros(xs_out.shape, xs_out.dtype)
        gate_scr[...] = jnp.zeros(gate_scr.shape, gate_scr.dtype)

    def chunk(ji, carry):
        j = jo * ROUTE_INNER + ji
        h = h2b_ref[0, pl.ds(pl.multiple_of(ji * t, t), t), :]
        slot_row = lambda e: slot_ref[e, pl.ds(j, 1), :]
        aff_row = lambda e: aff_ref[e, pl.ds(j, 1), :]

        @pl.when(over_ref[bi * n_j + j] == 0)
        def _():
            w_iota = lax.broadcasted_iota(jnp.int32, (ROUTE_WIN, t), 0).astype(F32)
            bases = []
            for e in range(n_e):
                base = pl.multiple_of(base_ref[(bi * n_e + e) * n_j + j], BF16_ROWS)
                hit = slot_row(e) == w_iota + base.astype(F32)
                p_scr[e * ROUTE_WIN:(e + 1) * ROUTE_WIN, :] = jnp.where(hit, 1.0, 0.0).astype(BF16)
                gate_scr[e, pl.ds(base, ROUTE_WIN), :] += jnp.sum(jnp.where(hit, aff_row(e), 0.0),
                                                                  axis=1, keepdims=True)
                bases.append(base)
            picked = _dot(p_scr[...], h)
            for e in range(n_e):
                rows = pl.ds(bases[e], ROUTE_WIN)
                xs_out[e, 0, rows, :] += picked[e * ROUTE_WIN:(e + 1) * ROUTE_WIN, :].astype(BF16)

        @pl.when(over_ref[bi * n_j + j] != 0)
        def _():
            c_iota = lax.broadcasted_iota(jnp.int32, (cap, t), 0).astype(F32)
            for e in range(n_e):
                hit = slot_row(e) == c_iota
                xs_out[e, 0] += _dot(jnp.where(hit, 1.0, 0.0).astype(BF16), h).astype(BF16)
                gate_scr[e] += jnp.sum(jnp.where(hit, aff_row(e), 0.0), axis=1, keepdims=True)

        return carry

    lax.fori_loop(0, ROUTE_INNER, chunk, 0)

    @pl.when(jo == pl.num_programs(1) - 1)
    def _():
        eye = lax.broadcasted_iota(jnp.int32, (cap, cap), 0) == lax.broadcasted_iota(jnp.int32, (cap, cap), 1)
        for e in range(n_e):
            gate_out[e, 0] = jnp.sum(jnp.where(eye, gate_scr[e], 0.0), axis=0, keepdims=True)


def _dispatch(base_flat, over_flat, h2b, slot3d, aff3d, n_e, cap):
    b, s, d = h2b.shape
    t = ROUTE_CHUNK
    rows = ROUTE_INNER * t
    grid_spec = pltpu.PrefetchScalarGridSpec(
        num_scalar_prefetch=2,
        grid=(b, s // rows),
        in_specs=[
            pl.BlockSpec((1, rows, d), lambda bi, jo, *_: (bi, jo, 0)),
            pl.BlockSpec((n_e, s // t, t), lambda bi, jo, *_: (bi, 0, 0)),
            pl.BlockSpec((n_e, s // t, t), lambda bi, jo, *_: (bi, 0, 0)),
        ],
        out_specs=[
            pl.BlockSpec((n_e, 1, cap, d), lambda bi, jo, *_: (0, bi, 0, 0)),
            pl.BlockSpec((n_e, 1, 1, cap), lambda bi, jo, *_: (0, bi, 0, 0)),
        ],
        scratch_shapes=[pltpu.VMEM((n_e * ROUTE_WIN, t), BF16), pltpu.VMEM((n_e, cap, 1), F32)],
    )
    return pl.pallas_call(
        functools.partial(_dispatch_kernel, cap=cap),
        grid_spec=grid_spec,
        out_shape=[
            jax.ShapeDtypeStruct((n_e, b, cap, d), BF16),
            jax.ShapeDtypeStruct((n_e, b, 1, cap), F32),
        ],
        compiler_params=_cparams(("parallel", "arbitrary")),
        name="dispatch",
    )(base_flat, over_flat, h2b, slot3d, aff3d)


FFN_ROWS = 512
FFN_CHUNK = 512


def _experts_kernel(xs_ref, gate_ref, wg_ref, wu_ref, wd_ref, y_out, wg_s, wu_s, wd_s, hid_s):
    e = pl.program_id(0)
    f = pl.program_id(1)
    n_chunks = wg_s.shape[1]

    def stage():
        slot = e % 2
        wg_s[slot, f] = wg_ref[0].astype(BF16)
        wu_s[slot, f] = wu_ref[0].astype(BF16)
        wd_s[slot, f] = wd_ref[0].astype(BF16)

    @pl.when(e == 0)
    def _():
        stage()
        y_out[0] = jnp.zeros(y_out.shape[1:], y_out.dtype)

    @pl.when(e > 0)
    def _():
        stage()
        slot = (e - 1) % 2
        xs = xs_ref[0]
        for c in range(n_chunks):
            g = _dot(xs, wg_s[slot, c])
            u = _dot(xs, wu_s[slot, c])
            hid_s[:, c * FFN_CHUNK:(c + 1) * FFN_CHUNK] = (g * _sigmoid(g) * u).astype(BF16)
        y = _dot(hid_s[:, 0:FFN_CHUNK], wd_s[slot, 0])
        for c in range(1, n_chunks):
            y = y + _dot(hid_s[:, c * FFN_CHUNK:(c + 1) * FFN_CHUNK], wd_s[slot, c])
        rows = xs.shape[0]
        eye = lax.broadcasted_iota(jnp.int32, (rows, rows), 0) == lax.broadcasted_iota(jnp.int32, (rows, rows), 1)
        gate_col = jnp.sum(jnp.where(eye, gate_ref[0], 0.0), axis=1, keepdims=True)
        y_out[0] = (y * gate_col).astype(BF16)


def _experts(xs, gate, w_gate, w_up, w_down):
    e, n, d = xs.shape
    ff = w_gate.shape[2]
    n_chunks = ff // FFN_CHUNK
    assert n // FFN_ROWS == n_chunks
    prev = lambda ei, fi: (jnp.maximum(ei - 1, 0), fi, 0)
    cur = lambda ei: jnp.minimum(ei, e - 1)
    return pl.pallas_call(
        _experts_kernel,
        grid=(e + 1, n_chunks),
        in_specs=[
            pl.BlockSpec((1, FFN_ROWS, d), prev),
            pl.BlockSpec((1, 1, FFN_ROWS), lambda ei, fi: (jnp.maximum(ei - 1, 0), 0, fi)),
            pl.BlockSpec((1, d, FFN_CHUNK), lambda ei, fi: (cur(ei), 0, fi)),
            pl.BlockSpec((1, d, FFN_CHUNK), lambda ei, fi: (cur(ei), 0, fi)),
            pl.BlockSpec((1, FFN_CHUNK, d), lambda ei, fi: (cur(ei), fi, 0)),
        ],
        out_specs=pl.BlockSpec((1, FFN_ROWS, d), lambda ei, fi: (jnp.where(ei == 0, e, ei - 1), fi, 0)),
        out_shape=jax.ShapeDtypeStruct((e + 1, n, d), BF16),
        scratch_shapes=[
            pltpu.VMEM((2, n_chunks, d, FFN_CHUNK), BF16),
            pltpu.VMEM((2, n_chunks, d, FFN_CHUNK), BF16),
            pltpu.VMEM((2, n_chunks, FFN_CHUNK, d), BF16),
            pltpu.VMEM((FFN_ROWS, ff), BF16),
        ],
        compiler_params=_cparams(("arbitrary", "arbitrary")),
        name="experts",
    )(xs, gate, w_gate, w_up, w_down)


def _combine_kernel(base_ref, over_ref, slot_t_ref, base_t_ref, y_ref, h2_ref, g3_ref, b3_ref, out_ref, ywin_scr,
                    *, cap):
    bi = pl.program_id(0)
    jo = pl.program_id(1)
    n_j = pl.num_programs(1) * ROUTE_INNER
    n_e = y_ref.shape[0]
    rows = slot_t_ref.shape[1]
    t = ROUTE_CHUNK

    def spread_cols(per_expert):
        width = n_e * per_expert
        col = lax.broadcasted_iota(jnp.int32, (rows, width), 1) // per_expert
        return jnp.where(lax.broadcasted_iota(jnp.int32, (rows, width), 0) == bi * n_e + col, 1.0, 0.0).astype(BF16)

    def lane_in_group(per_expert):
        return (lax.broadcasted_iota(jnp.int32, (1, n_e * per_expert), 1) % per_expert).astype(F32)

    def chunk(ji, carry):
        j = jo * ROUTE_INNER + ji
        tok = pl.ds(pl.multiple_of(ji * t, t), t)

        def finish(ff):
            out_ref[0, tok, :] = _layer_norm(DEEPNORM_ALPHA * h2_ref[0, tok, :] + ff, g3_ref[...], b3_ref[...])

        slot_bf = slot_t_ref[tok, :].astype(BF16)

        @pl.when(over_ref[bi * n_j + j] == 0)
        def _():
            spread = spread_cols(ROUTE_WIN)
            slot_wide = _dot(slot_bf, spread)
            base_wide = _dot(jnp.broadcast_to(base_t_ref[ji], (SUBLANES, rows)).astype(BF16), spread)[0:1, :]
            onehot = jnp.where(slot_wide - base_wide == lane_in_group(ROUTE_WIN), 1.0, 0.0).astype(BF16)
            for e in range(n_e):
                base = pl.multiple_of(base_ref[(bi * n_e + e) * n_j + j], BF16_ROWS)
                ywin_scr[e * ROUTE_WIN:(e + 1) * ROUTE_WIN, :] = y_ref[e, 0, pl.ds(base, ROUTE_WIN), :]
            finish(_dot(onehot, ywin_scr[...]))

        @pl.when(over_ref[bi * n_j + j] != 0)
        def _():
            slot_wide = _dot(slot_bf, spread_cols(cap))
            onehot = jnp.where(slot_wide == lane_in_group(cap), 1.0, 0.0).astype(BF16)
            finish(_dot(onehot, y_ref[:, 0].reshape(n_e * cap, y_ref.shape[3])))

        return carry

    lax.fori_loop(0, ROUTE_INNER, chunk, 0)


def _combine(base_flat, over_flat, slot_t, base_t, y4, h2, g3, b3, cap):
    b, s, d = h2.shape
    e = y4.shape[0] - 1
    rows = ROUTE_INNER * ROUTE_CHUNK
    const = lambda bi, jo, *_: (0, 0)
    grid_spec = pltpu.PrefetchScalarGridSpec(
        num_scalar_prefetch=2,
        grid=(b, s // rows),
        in_specs=[
            pl.BlockSpec((rows, slot_t.shape[1]), lambda bi, jo, *_: (jo, 0)),
            pl.BlockSpec((ROUTE_INNER, 1, base_t.shape[2]), lambda bi, jo, *_: (jo, 0, 0)),
            pl.BlockSpec((e, 1, cap, d), lambda bi, jo, *_: (0, bi, 0, 0)),
            pl.BlockSpec((1, rows, d), lambda bi, jo, *_: (bi, jo, 0)),
            pl.BlockSpec(g3.shape, const),
            pl.BlockSpec(b3.shape, const),
        ],
        out_specs=pl.BlockSpec((1, rows, d), lambda bi, jo, *_: (bi, jo, 0)),
        scratch_shapes=[pltpu.VMEM((e * ROUTE_WIN, d), BF16)],
    )
    return pl.pallas_call(
        functools.partial(_combine_kernel, cap=cap),
        grid_spec=grid_spec,
        out_shape=jax.ShapeDtypeStruct((b, s, d), F32),
        compiler_params=_cparams(("parallel", "parallel")),
        name="combine",
    )(base_flat, over_flat, slot_t, base_t, y4, h2, g3, b3)


def _extend_weights(w_in, w_uq, w_uk, w_uv):
    half = MLA_ROPE_DIM // 2
    d = w_in.shape[0]
    kr = w_in[:, SPLIT_KV:SPLIT_KR]
    z = lambda n: jnp.zeros((d, n), w_in.dtype)
    kr_full = jnp.concatenate([z(MLA_NOPE_DIM), kr, z(HEAD_PAD - MLA_QK_DIM)], axis=1)
    kr_swap = jnp.concatenate([z(MLA_NOPE_DIM), -kr[:, half:], kr[:, :half], z(HEAD_PAD - MLA_QK_DIM)], axis=1)
    w_in_ext = jnp.concatenate([w_in[:, :SPLIT_KV], kr_full, kr_swap, w_in[:, SPLIT_KR:]], axis=1)

    wq = w_uq.reshape(MLA_Q_RANK, MLA_HEADS, MLA_QK_DIM)
    zq = lambda n: jnp.zeros((MLA_Q_RANK, MLA_HEADS, n), w_uq.dtype)
    wq_full = jnp.concatenate([wq, zq(HEAD_PAD - MLA_QK_DIM)], axis=2)
    wq_swap = jnp.concatenate([zq(MLA_NOPE_DIM), -wq[:, :, MLA_NOPE_DIM + half:],
                               wq[:, :, MLA_NOPE_DIM:MLA_NOPE_DIM + half], zq(HEAD_PAD - MLA_QK_DIM)], axis=2)
    wq_ext = jnp.concatenate([wq_full.reshape(MLA_Q_RANK, QK_PAD), wq_swap.reshape(MLA_Q_RANK, QK_PAD)], axis=1)

    wk = w_uk.reshape(MLA_KV_RANK, MLA_HEADS, MLA_NOPE_DIM)
    wk_full = jnp.concatenate([wk, jnp.zeros((MLA_KV_RANK, MLA_HEADS, HEAD_PAD - MLA_NOPE_DIM), w_uk.dtype)], axis=2)
    wv = w_uv.reshape(MLA_KV_RANK, MLA_HEADS, MLA_V_DIM)
    wv_full = jnp.concatenate([wv, jnp.zeros((MLA_KV_RANK, MLA_HEADS, HEAD_PAD - MLA_V_DIM), w_uv.dtype)], axis=2)
    wkv_ext = jnp.concatenate([wk_full.reshape(MLA_KV_RANK, QK_PAD), wv_full.reshape(MLA_KV_RANK, QK_PAD)], axis=1)
    return w_in_ext.astype(BF16), wq_ext.astype(BF16), wkv_ext.astype(BF16)


def _rope_freq_column():
    inv_freq = ROPE_BASE ** (-jnp.arange(0, MLA_ROPE_DIM, 2, dtype=F32) / MLA_ROPE_DIM)
    return inv_freq[:, None]


def _value_one_lanes():
    one_hot = (jnp.arange(HEAD_PAD) == MLA_V_DIM).astype(F32)
    return jnp.tile(one_hot, MLA_HEADS)[None, :]


def kernel(x, mem, positions, w_in, q_norm_g, w_uq, kv_norm_g, w_uk, w_uv, conv_w, conv_b, conv_ln_g, conv_ln_b,
           w_o, ln1_g, ln1_b, xa_w_q, xa_w_k, xa_w_v, xa_w_o, ln2_g, ln2_b, w_router, w_gate, w_up, w_down,
           ln3_g, ln3_b):
    assert w_in.shape[0] == DEPTH == 1
    b, s, d = x.shape
    cap = CAPACITY_FACTOR * s // N_EXPERTS
    tm = 512

    w_in_ext, wq_ext, wkv_ext = _extend_weights(w_in[0], w_uq[0], w_uk[0], w_uv[0])
    posf = positions.astype(F32)[:, None, :]
    q, k, v, u_pre = _front(x, posf, w_in_ext, q_norm_g, wq_ext, kv_norm_g, wkv_ext, _rope_freq_column(),
                             _value_one_lanes(), tm)
    attn = _attention(q, k, v, 1024)
    kx, vx = _mem_kv(mem, xa_w_k[0].astype(BF16), xa_w_v[0].astype(BF16))
    h2, h2b, aff = _mid(u_pre, attn, x, conv_w[0], conv_b, conv_ln_g, conv_ln_b, w_o[0].astype(BF16), ln1_g, ln1_b,
                        xa_w_q[0].astype(BF16), kx, vx, xa_w_o[0].astype(BF16), ln2_g, ln2_b,
                        _split_router(w_router[0]), tm)

    aff2d = aff.reshape(b * N_EXPERTS, s)
    slot, slot_t, off_tab = _topk(aff2d, cap)
    n_j = s // ROUTE_CHUNK
    base, over = _route_tables(off_tab[:, :n_j + 1], b, N_EXPERTS, cap)
    base_flat, over_flat = base.reshape(-1), over.reshape(-1)
    by_chunk = lambda a: a.reshape(b * N_EXPERTS, n_j, ROUTE_CHUNK)
    xs, gate = _dispatch(base_flat, over_flat, h2b, by_chunk(slot), by_chunk(aff2d), N_EXPERTS, cap)
    y = _experts(xs.reshape(N_EXPERTS, b * cap, d), gate.reshape(N_EXPERTS, 1, b * cap),
                 w_gate[0], w_up[0], w_down[0])
    base_t = base.T.astype(F32).reshape(n_j, 1, b * N_EXPERTS)
    return _combine(base_flat, over_flat, slot_t, base_t, y.reshape(N_EXPERTS + 1, b, cap, d), h2, ln3_g, ln3_b, cap)
```

```python
import functools
import math

import jax
import jax.numpy as jnp
from jax import lax
from jax.experimental import pallas as pl
from jax.experimental.pallas import tpu as pltpu

F32 = jnp.float32
BF16 = jnp.bfloat16

D_MODEL = 1024
MLA_HEADS = 8
MLA_NOPE_DIM = 64
MLA_ROPE_DIM = 32
MLA_QK_DIM = MLA_NOPE_DIM + MLA_ROPE_DIM
MLA_V_DIM = 64
MLA_Q_RANK = 256
MLA_KV_RANK = 128
MLA_WIDTH = MLA_HEADS * MLA_V_DIM
CONV_CH = D_MODEL - MLA_WIDTH
CONV_WIDTH = 31
CONV_PAD = (CONV_WIDTH - 1) // 2
ROPE_BASE = 10000.0
MEM_HEADS = 4
MEM_HEAD_DIM = D_MODEL // MEM_HEADS
N_EXPERTS = 16
EXPERT_FF = 2048
CAPACITY_FACTOR = 2
NORM_EPS = 1e-5
DEPTH = 1
DEEPNORM_ALPHA = (2.0 * DEPTH) ** 0.25
SPLIT_Q = MLA_Q_RANK
SPLIT_KV = SPLIT_Q + MLA_KV_RANK
SPLIT_KR = SPLIT_KV + MLA_ROPE_DIM

HEAD_PAD = 128
QK_PAD = MLA_HEADS * HEAD_PAD
OFF_CQ = 0
OFF_CKV = OFF_CQ + MLA_Q_RANK
OFF_KR = OFF_CKV + MLA_KV_RANK
OFF_KRS = OFF_KR + HEAD_PAD
OFF_A = OFF_KRS + HEAD_PAD
OFF_G = OFF_A + CONV_CH
IN_EXT = OFF_G + CONV_CH

SUBLANES = 8
LANES = 128
BF16_ROWS = 16
ROUTE_CHUNK = 256
ROUTE_WIN = 64
ROUTE_INNER = 4
HALO = 16
VMEM_LIMIT = 56 * 1024 * 1024


def _cparams(sem):
    return pltpu.CompilerParams(dimension_semantics=sem, vmem_limit_bytes=VMEM_LIMIT)


def _layer_norm(v, g, b):
    mu = jnp.mean(v, axis=-1, keepdims=True)
    d = v - mu
    var = jnp.mean(d * d, axis=-1, keepdims=True)
    return d * lax.rsqrt(var + NORM_EPS) * g + b


def _rms_norm(v, g):
    return v * lax.rsqrt(jnp.mean(v * v, axis=-1, keepdims=True) + NORM_EPS) * g


def _sigmoid(v):
    return 1.0 / (1.0 + jnp.exp(-v))


def _dot(a, b):
    return jnp.dot(a, b, preferred_element_type=F32)


def _dot_nt(a, b, precision=None):
    return lax.dot_general(a, b, (((1,), (1,)), ((), ())), preferred_element_type=F32, precision=precision)


def _front_kernel(x_ref, pos_ref, win_ref, qg_ref, wq_ref, kvg_ref, wkv_ref, invf_ref, vone_ref,
                  q_out, k_out, v_out, u_out):
    x = x_ref[0].astype(BF16)
    hc = _dot(x, win_ref[...])
    cqn = _rms_norm(hc[:, OFF_CQ:OFF_CKV], qg_ref[...])
    qq = _dot(cqn.astype(BF16), wq_ref[...])
    ckvn = _rms_norm(hc[:, OFF_CKV:OFF_KR], kvg_ref[...])
    kv = _dot(ckvn.astype(BF16), wkv_ref[...])
    tm = x_ref.shape[1]
    ang = invf_ref[...] * pos_ref[0]
    cos_h, sin_h = jnp.cos(ang), jnp.sin(ang)
    ones = lambda n: jnp.ones((n, tm), F32)
    zeros = lambda n: jnp.zeros((n, tm), F32)
    cos = jnp.concatenate([ones(MLA_NOPE_DIM), cos_h, cos_h, ones(HEAD_PAD - MLA_QK_DIM)], axis=0).T
    sin = jnp.concatenate([zeros(MLA_NOPE_DIM), sin_h, sin_h, zeros(HEAD_PAD - MLA_QK_DIM)], axis=0).T
    krot = hc[:, OFF_KR:OFF_KRS] * cos + hc[:, OFF_KRS:OFF_A] * sin
    scale = math.log2(math.e) / math.sqrt(MLA_QK_DIM)
    for h in range(MLA_HEADS):
        lo, hi = h * HEAD_PAD, (h + 1) * HEAD_PAD
        qh = (qq[:, lo:hi] * cos + qq[:, QK_PAD + lo:QK_PAD + hi] * sin) * scale
        q_out[0, :, lo:hi] = qh.astype(BF16)
        k_out[0, :, lo:hi] = (kv[:, lo:hi] + krot).astype(BF16)
    v_out[0] = (kv[:, QK_PAD:] + vone_ref[...]).astype(BF16)
    u_out[0] = hc[:, OFF_A:OFF_G] * _sigmoid(hc[:, OFF_G:IN_EXT])


def _front(x, posf, w_in_ext, qg, wq_ext, kvg, wkv_ext, invf, vone, tm):
    b, s, d = x.shape
    const = lambda bi, i: (0, 0)
    return pl.pallas_call(
        _front_kernel,
        grid=(b, s // tm),
        in_specs=[
            pl.BlockSpec((1, tm, d), lambda bi, i: (bi, i, 0)),
            pl.BlockSpec((1, 1, tm), lambda bi, i: (bi, 0, i)),
            pl.BlockSpec(w_in_ext.shape, const),
            pl.BlockSpec(qg.shape, const),
            pl.BlockSpec(wq_ext.shape, const),
            pl.BlockSpec(kvg.shape, const),
            pl.BlockSpec(wkv_ext.shape, const),
            pl.BlockSpec(invf.shape, const),
            pl.BlockSpec(vone.shape, const),
        ],
        out_specs=[
            pl.BlockSpec((1, tm, QK_PAD), lambda bi, i: (bi, i, 0)),
            pl.BlockSpec((1, tm, QK_PAD), lambda bi, i: (bi, i, 0)),
            pl.BlockSpec((1, tm, QK_PAD), lambda bi, i: (bi, i, 0)),
            pl.BlockSpec((1, tm, CONV_CH), lambda bi, i: (bi, i, 0)),
        ],
        out_shape=[
            jax.ShapeDtypeStruct((b, s, QK_PAD), BF16),
            jax.ShapeDtypeStruct((b, s, QK_PAD), BF16),
            jax.ShapeDtypeStruct((b, s, QK_PAD), BF16),
            jax.ShapeDtypeStruct((b, s, CONV_CH), F32),
        ],
        compiler_params=_cparams(("parallel", "parallel")),
        name="front",
    )(x, posf, w_in_ext, qg, wq_ext, kvg, wkv_ext, invf, vone)


def _attn_kernel(q_ref, k_ref, v_ref, o_ref):
    for h in range(MLA_HEADS):
        lo, hi = h * HEAD_PAD, (h + 1) * HEAD_PAD
        sc = _dot_nt(q_ref[0, :, lo:hi], k_ref[0, :, lo:hi])
        p = jnp.exp2(sc - jnp.max(sc, axis=-1, keepdims=True))
        pv = _dot(p.astype(BF16), v_ref[0, :, lo:hi])
        o = pv[:, :MLA_V_DIM] / pv[:, MLA_V_DIM:MLA_V_DIM + 1]
        o_ref[0, :, h * MLA_V_DIM:(h + 1) * MLA_V_DIM] = o.astype(BF16)


def _attention(q, k, v, tq):
    b, s, _ = q.shape
    return pl.pallas_call(
        _attn_kernel,
        grid=(b, s // tq),
        in_specs=[
            pl.BlockSpec((1, tq, QK_PAD), lambda bi, i: (bi, i, 0)),
            pl.BlockSpec((1, s, QK_PAD), lambda bi, i: (bi, 0, 0)),
            pl.BlockSpec((1, s, QK_PAD), lambda bi, i: (bi, 0, 0)),
        ],
        out_specs=pl.BlockSpec((1, tq, MLA_WIDTH), lambda bi, i: (bi, i, 0)),
        out_shape=jax.ShapeDtypeStruct((b, s, MLA_WIDTH), BF16),
        compiler_params=_cparams(("parallel", "parallel")),
        name="attn",
    )(q, k, v)


def _mem_kv_kernel(mem_ref, wk_ref, wv_ref, k_out, v_out):
    m = mem_ref[0].astype(BF16)
    k_out[0] = _dot(m, wk_ref[...]).astype(BF16)
    v_out[0] = _dot(m, wv_ref[...]).astype(BF16)


def _mem_kv(mem, wk, wv):
    b, m, d = mem.shape
    const = lambda bi: (0, 0)
    return pl.pallas_call(
        _mem_kv_kernel,
        grid=(b,),
        in_specs=[
            pl.BlockSpec((1, m, d), lambda bi: (bi, 0, 0)),
            pl.BlockSpec(wk.shape, const),
            pl.BlockSpec(wv.shape, const),
        ],
        out_specs=[pl.BlockSpec((1, m, d), lambda bi: (bi, 0, 0))] * 2,
        out_shape=[jax.ShapeDtypeStruct((b, m, d), BF16)] * 2,
        compiler_params=_cparams(("parallel",)),
        name="mem_kv",
    )(mem, wk, wv)


CONV_SUB = 64


def _mid_kernel(ucur_ref, uprev_ref, unext_ref, attn_ref, x_ref, cw_ref, cb_ref, cg_ref, cbeta_ref, wo_ref,
                g1_ref, b1_ref, wq_ref, kx_ref, vx_ref, xwo_ref, g2_ref, b2_ref, wr_ref,
                h2_out, h2b_out, aff_out, win_ref, shift_ref, u_scr, o_scr):
    tm = ucur_ref.shape[1]
    i = pl.program_id(1)
    last = pl.num_programs(1) - 1
    win_ref[0:HALO, :] = jnp.where(i > 0, uprev_ref[0], 0.0)
    win_ref[HALO:HALO + tm, :] = ucur_ref[0]
    win_ref[HALO + tm:2 * HALO + tm, :] = jnp.where(i < last, unext_ref[0], 0.0)
    span = tm + 2 * HALO - SUBLANES
    shift_ref[0] = win_ref[...]
    for j in range(1, SUBLANES):
        shift_ref[j, 0:span, :] = win_ref[j:j + span, :]

    for r in range(tm // CONV_SUB):
        acc = jnp.broadcast_to(cb_ref[...], (CONV_SUB, CONV_CH))
        for t in range(CONV_WIDTH):
            off = HALO - CONV_PAD + t
            row = r * CONV_SUB + (off // SUBLANES) * SUBLANES
            acc = acc + shift_ref[off % SUBLANES, row:row + CONV_SUB, :] * cw_ref[t:t + 1, :]
        y = _layer_norm(acc, cg_ref[...], cbeta_ref[...])
        u_scr[r * CONV_SUB:(r + 1) * CONV_SUB, :] = (y * _sigmoid(y)).astype(BF16)
    mix = _dot(attn_ref[0], wo_ref[0:MLA_WIDTH, :]) + _dot(u_scr[...], wo_ref[MLA_WIDTH:, :])
    h1 = _layer_norm(DEEPNORM_ALPHA * x_ref[0] + mix, g1_ref[...], b1_ref[...])
    q = (_dot(h1.astype(BF16), wq_ref[...]) * (1.0 / math.sqrt(MEM_HEAD_DIM))).astype(BF16)
    for h in range(MEM_HEADS):
        lo, hi = h * MEM_HEAD_DIM, (h + 1) * MEM_HEAD_DIM
        sc = _dot_nt(q[:, lo:hi], kx_ref[0, :, lo:hi])
        p = jnp.exp(sc - jnp.max(sc, axis=-1, keepdims=True))
        l = jnp.sum(p, axis=-1, keepdims=True)
        o_scr[:, lo:hi] = (_dot(p.astype(BF16), vx_ref[0, :, lo:hi]) / l).astype(BF16)
    xa = _dot(o_scr[...], xwo_ref[...])
    h2 = _layer_norm(DEEPNORM_ALPHA * h1 + xa, g2_ref[...], b2_ref[...])
    h2_out[0] = h2
    h2_hi = h2.astype(BF16)
    h2b_out[0] = h2_hi
    h2_lo = (h2 - h2_hi.astype(F32)).astype(BF16)
    hi_terms = _dot(h2_hi, wr_ref[...])
    logits = hi_terms[:, :LANES] + hi_terms[:, LANES:] + _dot(h2_lo, wr_ref[:, :LANES])
    lt = logits.T[0:aff_out.shape[1], :]
    ex = jnp.exp(lt - jnp.max(lt, axis=0, keepdims=True))
    aff_out[0] = ex / jnp.sum(ex, axis=0, keepdims=True)


def _mid(u_pre, attn, x, conv_w, conv_b, conv_g, conv_beta, w_o, g1, b1, wq, kx, vx, xwo, g2, b2, wr_split, tm):
    b, s, d = x.shape
    m = kx.shape[1]
    nh = tm // HALO
    const = lambda bi, i: (0, 0)
    tile = lambda w: pl.BlockSpec((1, tm, w), lambda bi, i: (bi, i, 0))
    whole = lambda arr: pl.BlockSpec(arr.shape, const)
    return pl.pallas_call(
        _mid_kernel,
        grid=(b, s // tm),
        in_specs=[
            tile(CONV_CH),
            pl.BlockSpec((1, HALO, CONV_CH), lambda bi, i: (bi, jnp.maximum(i * nh - 1, 0), 0)),
            pl.BlockSpec((1, HALO, CONV_CH), lambda bi, i: (bi, jnp.minimum((i + 1) * nh, s // HALO - 1), 0)),
            tile(MLA_WIDTH),
            tile(d),
            whole(conv_w), whole(conv_b), whole(conv_g), whole(conv_beta), whole(w_o), whole(g1), whole(b1),
            whole(wq),
            pl.BlockSpec((1, m, d), lambda bi, i: (bi, 0, 0)),
            pl.BlockSpec((1, m, d), lambda bi, i: (bi, 0, 0)),
            whole(xwo), whole(g2), whole(b2), whole(wr_split),
        ],
        out_specs=[
            tile(d),
            tile(d),
            pl.BlockSpec((1, N_EXPERTS, tm), lambda bi, i: (bi, 0, i)),
        ],
        out_shape=[
            jax.ShapeDtypeStruct((b, s, d), F32),
            jax.ShapeDtypeStruct((b, s, d), BF16),
            jax.ShapeDtypeStruct((b, N_EXPERTS, s), F32),
        ],
        scratch_shapes=[
            pltpu.VMEM((tm + 2 * HALO, CONV_CH), F32),
            pltpu.VMEM((SUBLANES, tm + 2 * HALO, CONV_CH), F32),
            pltpu.VMEM((tm, CONV_CH), BF16),
            pltpu.VMEM((tm, d), BF16),
        ],
        compiler_params=_cparams(("parallel", "parallel")),
        name="mid",
    )(u_pre, u_pre, u_pre, attn, x, conv_w, conv_b, conv_g, conv_beta, w_o, g1, b1, wq, kx, vx, xwo, g2, b2, wr_split)


def _split_router(w_router):
    hi = w_router.astype(BF16)
    lo = (w_router - hi.astype(F32)).astype(BF16)
    pad = jnp.zeros((w_router.shape[0], LANES - w_router.shape[1]), BF16)
    return jnp.concatenate([hi, pad, lo, pad], axis=1)


BISECT_STEPS_PER_CHECK = 4


def _topk_kernel(aff_ref, slot_out, slot_t_out, off_out, *, cap):
    aff = aff_ref[...]
    rows, s = aff.shape
    capf = jnp.float32(cap)

    def not_done(carry):
        return carry[2] > 0

    def halve(_, bounds):
        lo, hi = bounds
        mid = 0.5 * (lo + hi)
        take = jnp.sum(jnp.where(aff >= mid, 1.0, 0.0), axis=1, keepdims=True) >= capf
        return jnp.where(take, mid, lo), jnp.where(take, hi, mid)

    def bisect(carry):
        lo, hi = lax.fori_loop(0, BISECT_STEPS_PER_CHECK, halve, carry[:2])
        smallest_in = jnp.min(jnp.where(aff >= lo, aff, jnp.inf), axis=1, keepdims=True)
        largest_in = jnp.max(jnp.where(aff < hi, aff, -jnp.inf), axis=1, keepdims=True)
        open_rows = jnp.sum(jnp.where(smallest_in == largest_in, 0.0, 1.0))
        return lo, hi, open_rows.astype(jnp.int32)

    _, hi, _ = lax.while_loop(not_done, bisect,
                              (jnp.zeros((rows, 1), F32), jnp.full((rows, 1), 2.0, F32), jnp.int32(1)))
    th = jnp.max(jnp.where(aff < hi, aff, -jnp.inf), axis=1, keepdims=True)
    gt = aff > th
    eq = aff == th
    n_gt = jnp.sum(jnp.where(gt, 1.0, 0.0), axis=1, keepdims=True)
    tri = jnp.where(lax.broadcasted_iota(jnp.int32, (s, s), 0) < lax.broadcasted_iota(jnp.int32, (s, s), 1),
                    1.0, 0.0).astype(BF16)
    tie_rank = _dot(jnp.where(eq, 1.0, 0.0).astype(BF16), tri)
    sel = jnp.logical_or(gt, jnp.logical_and(eq, tie_rank < (capf - n_gt)))
    pos = _dot(jnp.where(sel, 1.0, 0.0).astype(BF16), tri)
    slot = jnp.where(sel, pos, -1.0)
    slot_out[...] = slot
    slot_t_out[...] = slot.T
    lanes = off_out.shape[1]
    before = jnp.where(lax.broadcasted_iota(jnp.int32, (s, lanes), 0)
                       < ROUTE_CHUNK * lax.broadcasted_iota(jnp.int32, (s, lanes), 1), 1.0, 0.0).astype(BF16)
    off_out[...] = _dot(jnp.where(sel, 1.0, 0.0).astype(BF16), before)


def _topk(aff2d, cap):
    rows, s = aff2d.shape
    return pl.pallas_call(
        functools.partial(_topk_kernel, cap=cap),
        out_shape=[
            jax.ShapeDtypeStruct((rows, s), F32),
            jax.ShapeDtypeStruct((s, rows), F32),
            jax.ShapeDtypeStruct((rows, LANES), F32),
        ],
        compiler_params=pltpu.CompilerParams(vmem_limit_bytes=VMEM_LIMIT),
        name="topk",
    )(aff2d)


def _route_tables(off_tab, n_b, n_e, cap):
    n_j = off_tab.shape[1] - 1
    off = off_tab.astype(jnp.int32)
    base = jnp.minimum((off[:, :n_j] // BF16_ROWS) * BF16_ROWS, cap - ROUTE_WIN)
    over = jnp.any((off[:, 1:] - base > ROUTE_WIN).reshape(n_b, n_e, n_j), axis=1)
    return base, over.astype(jnp.int32)


def _dispatch_kernel(base_ref, over_ref, h2b_ref, slot_ref, aff_ref, xs_out, gate_out, p_scr, gate_scr, *, cap):
    bi = pl.program_id(0)
    jo = pl.program_id(1)
    n_j = pl.num_programs(1) * ROUTE_INNER
    n_e, _, t = slot_ref.shape

    @pl.when(jo == 0)
    def _():
        xs_out[...] = jnp.zeros(xs_out.shape, xs_out.dtype)
        gate_scr[...] = jnp.zeros(gate_scr.shape, gate_scr.dtype)

    def chunk(ji, carry):
        j = jo * ROUTE_INNER + ji
        h = h2b_ref[0, pl.ds(pl.multiple_of(ji * t, t), t), :]
        slot_row = lambda e: slot_ref[e, pl.ds(j, 1), :]
        aff_row = lambda e: aff_ref[e, pl.ds(j, 1), :]

        @pl.when(over_ref[bi * n_j + j] == 0)
        def _():
            w_iota = lax.broadcasted_iota(jnp.int32, (ROUTE_WIN, t), 0).astype(F32)
            bases = []
            for e in range(n_e):
                base = pl.multiple_of(base_ref[(bi * n_e + e) * n_j + j], BF16_ROWS)
                hit = slot_row(e) == w_iota + base.astype(F32)
                p_scr[e * ROUTE_WIN:(e + 1) * ROUTE_WIN, :] = jnp.where(hit, 1.0, 0.0).astype(BF16)
                gate_scr[e, pl.ds(base, ROUTE_WIN), :] += jnp.sum(jnp.where(hit, aff_row(e), 0.0),
                                                                  axis=1, keepdims=True)
                bases.append(base)
            picked = _dot(p_scr[...], h)
            for e in range(n_e):
                rows = pl.ds(bases[e], ROUTE_WIN)
                xs_out[e, 0, rows, :] += picked[e * ROUTE_WIN:(e + 1) * ROUTE_WIN, :].astype(BF16)

        @pl.when(over_ref[bi * n_j + j] != 0)
        def _():
            c_iota = lax.broadcasted_iota(jnp.int32, (cap, t), 0).astype(F32)
            for e in range(n_e):
                hit = slot_row(e) == c_iota
                xs_out[e, 0] += _dot(jnp.where(hit, 1.0, 0.0).astype(BF16), h).astype(BF16)
                gate_scr[e] += jnp.sum(jnp.where(hit, aff_row(e), 0.0), axis=1, keepdims=True)

        return carry

    lax.fori_loop(0, ROUTE_INNER, chunk, 0)

    @pl.when(jo == pl.num_programs(1) - 1)
    def _():
        eye = lax.broadcasted_iota(jnp.int32, (cap, cap), 0) == lax.broadcasted_iota(jnp.int32, (cap, cap), 1)
        for e in range(n_e):
            gate_out[e, 0] = jnp.sum(jnp.where(eye, gate_scr[e], 0.0), axis=0, keepdims=True)


def _dispatch(base_flat, over_flat, h2b, slot3d, aff3d, n_e, cap):
    b, s, d = h2b.shape
    t = ROUTE_CHUNK
    rows = ROUTE_INNER * t
    grid_spec = pltpu.PrefetchScalarGridSpec(
        num_scalar_prefetch=2,
        grid=(b, s // rows),
        in_specs=[
            pl.BlockSpec((1, rows, d), lambda bi, jo, *_: (bi, jo, 0)),
            pl.BlockSpec((n_e, s // t, t), lambda bi, jo, *_: (bi, 0, 0)),
            pl.BlockSpec((n_e, s // t, t), lambda bi, jo, *_: (bi, 0, 0)),
        ],
        out_specs=[
            pl.BlockSpec((n_e, 1, cap, d), lambda bi, jo, *_: (0, bi, 0, 0)),
            pl.BlockSpec((n_e, 1, 1, cap), lambda bi, jo, *_: (0, bi, 0, 0)),
        ],
        scratch_shapes=[pltpu.VMEM((n_e * ROUTE_WIN, t), BF16), pltpu.VMEM((n_e, cap, 1), F32)],
    )
    return pl.pallas_call(
        functools.partial(_dispatch_kernel, cap=cap),
        grid_spec=grid_spec,
        out_shape=[
            jax.ShapeDtypeStruct((n_e, b, cap, d), BF16),
            jax.ShapeDtypeStruct((n_e, b, 1, cap), F32),
        ],
        compiler_params=_cparams(("parallel", "arbitrary")),
        name="dispatch",
    )(base_flat, over_flat, h2b, slot3d, aff3d)


FFN_ROWS = 512
FFN_CHUNK = 512


def _experts_kernel(xs_ref, gate_ref, wg_ref, wu_ref, wd_ref, y_out, wg_s, wu_s, wd_s, hid_s):
    e = pl.program_id(0)
    f = pl.program_id(1)
    n_chunks = wg_s.shape[1]

    def stage():
        slot = e % 2
        wg_s[slot, f] = wg_ref[0].astype(BF16)
        wu_s[slot, f] = wu_ref[0].astype(BF16)
        wd_s[slot, f] = wd_ref[0].astype(BF16)

    @pl.when(e == 0)
    def _():
        stage()
        y_out[0] = jnp.zeros(y_out.shape[1:], y_out.dtype)

    @pl.when(e > 0)
    def _():
        stage()
        slot = (e - 1) % 2
        xs = xs_ref[0]
        for c in range(n_chunks):
            g = _dot(xs, wg_s[slot, c])
            u = _dot(xs, wu_s[slot, c])
            hid_s[:, c * FFN_CHUNK:(c + 1) * FFN_CHUNK] = (g * _sigmoid(g) * u).astype(BF16)
        y = _dot(hid_s[:, 0:FFN_CHUNK], wd_s[slot, 0])
        for c in range(1, n_chunks):
            y = y + _dot(hid_s[:, c * FFN_CHUNK:(c + 1) * FFN_CHUNK], wd_s[slot, c])
        rows = xs.shape[0]
        eye = lax.broadcasted_iota(jnp.int32, (rows, rows), 0) == lax.broadcasted_iota(jnp.int32, (rows, rows), 1)
        gate_col = jnp.sum(jnp.where(eye, gate_ref[0], 0.0), axis=1, keepdims=True)
        y_out[0] = (y * gate_col).astype(BF16)


def _experts(xs, gate, w_gate, w_up, w_down):
    e, n, d = xs.shape
    ff = w_gate.shape[2]
    n_chunks = ff // FFN_CHUNK
    assert n // FFN_ROWS == n_chunks
    prev = lambda ei, fi: (jnp.maximum(ei - 1, 0), fi, 0)
    cur = lambda ei: jnp.minimum(ei, e - 1)
    return pl.pallas_call(
        _experts_kernel,
        grid=(e + 1, n_chunks),
        in_specs=[
            pl.BlockSpec((1, FFN_ROWS, d), prev),
            pl.BlockSpec((1, 1, FFN_ROWS), lambda ei, fi: (jnp.maximum(ei - 1, 0), 0, fi)),
            pl.BlockSpec((1, d, FFN_CHUNK), lambda ei, fi: (cur(ei), 0, fi)),
            pl.BlockSpec((1, d, FFN_CHUNK), lambda ei, fi: (cur(ei), 0, fi)),
            pl.BlockSpec((1, FFN_CHUNK, d), lambda ei, fi: (cur(ei), fi, 0)),
        ],
        out_specs=pl.BlockSpec((1, FFN_ROWS, d), lambda ei, fi: (jnp.where(ei == 0, e, ei - 1), fi, 0)),
        out_shape=jax.ShapeDtypeStruct((e + 1, n, d), BF16),
        scratch_shapes=[
            pltpu.VMEM((2, n_chunks, d, FFN_CHUNK), BF16),
            pltpu.VMEM((2, n_chunks, d, FFN_CHUNK), BF16),
            pltpu.VMEM((2, n_chunks, FFN_CHUNK, d), BF16),
            pltpu.VMEM((FFN_ROWS, ff), BF16),
        ],
        compiler_params=_cparams(("arbitrary", "arbitrary")),
        name="experts",
    )(xs, gate, w_gate, w_up, w_down)


def _combine_kernel(base_ref, over_ref, slot_t_ref, base_t_ref, y_ref, h2_ref, g3_ref, b3_ref, out_ref, ywin_scr,
                    *, cap):
    bi = pl.program_id(0)
    jo = pl.program_id(1)
    n_j = pl.num_programs(1) * ROUTE_INNER
    n_e = y_ref.shape[0]
    rows = slot_t_ref.shape[1]
    t = ROUTE_CHUNK

    def spread_cols(per_expert):
        width = n_e * per_expert
        col = lax.broadcasted_iota(jnp.int32, (rows, width), 1) // per_expert
        return jnp.where(lax.broadcasted_iota(jnp.int32, (rows, width), 0) == bi * n_e + col, 1.0, 0.0).astype(BF16)

    def lane_in_group(per_expert):
        return (lax.broadcasted_iota(jnp.int32, (1, n_e * per_expert), 1) % per_expert).astype(F32)

    def chunk(ji, carry):
        j = jo * ROUTE_INNER + ji
        tok = pl.ds(pl.multiple_of(ji * t, t), t)

        def finish(ff):
            out_ref[0, tok, :] = _layer_norm(DEEPNORM_ALPHA * h2_ref[0, tok, :] + ff, g3_ref[...], b3_ref[...])

        slot_bf = slot_t_ref[tok, :].astype(BF16)

        @pl.when(over_ref[bi * n_j + j] == 0)
        def _():
            spread = spread_cols(ROUTE_WIN)
            slot_wide = _dot(slot_bf, spread)
            base_wide = _dot(jnp.broadcast_to(base_t_ref[ji], (SUBLANES, rows)).astype(BF16), spread)[0:1, :]
            onehot = jnp.where(slot_wide - base_wide == lane_in_group(ROUTE_WIN), 1.0, 0.0).astype(BF16)
            for e in range(n_e):
                base = pl.multiple_of(base_ref[(bi * n_e + e) * n_j + j], BF16_ROWS)
                ywin_scr[e * ROUTE_WIN:(e + 1) * ROUTE_WIN, :] = y_ref[e, 0, pl.ds(base, ROUTE_WIN), :]
            finish(_dot(onehot, ywin_scr[...]))

        @pl.when(over_ref[bi * n_j + j] != 0)
        def _():
            slot_wide = _dot(slot_bf, spread_cols(cap))
            onehot = jnp.where(slot_wide == lane_in_group(cap), 1.0, 0.0).astype(BF16)
            finish(_dot(onehot, y_ref[:, 0].reshape(n_e * cap, y_ref.shape[3])))

        return carry

    lax.fori_loop(0, ROUTE_INNER, chunk, 0)


def _combine(base_flat, over_flat, slot_t, base_t, y4, h2, g3, b3, cap):
    b, s, d = h2.shape
    e = y4.shape[0] - 1
    rows = ROUTE_INNER * ROUTE_CHUNK
    const = lambda bi, jo, *_: (0, 0)
    grid_spec = pltpu.PrefetchScalarGridSpec(
        num_scalar_prefetch=2,
        grid=(b, s // rows),
        in_specs=[
            pl.BlockSpec((rows, slot_t.shape[1]), lambda bi, jo, *_: (jo, 0)),
            pl.BlockSpec((ROUTE_INNER, 1, base_t.shape[2]), lambda bi, jo, *_: (jo, 0, 0)),
            pl.BlockSpec((e, 1, cap, d), lambda bi, jo, *_: (0, bi, 0, 0)),
            pl.BlockSpec((1, rows, d), lambda bi, jo, *_: (bi, jo, 0)),
            pl.BlockSpec(g3.shape, const),
            pl.BlockSpec(b3.shape, const),
        ],
        out_specs=pl.BlockSpec((1, rows, d), lambda bi, jo, *_: (bi, jo, 0)),
        scratch_shapes=[pltpu.VMEM((e * ROUTE_WIN, d), BF16)],
    )
    return pl.pallas_call(
        functools.partial(_combine_kernel, cap=cap),
        grid_spec=grid_spec,
        out_shape=jax.ShapeDtypeStruct((b, s, d), F32),
        compiler_params=_cparams(("parallel", "parallel")),
        name="combine",
    )(base_flat, over_flat, slot_t, base_t, y4, h2, g3, b3)


def _extend_weights(w_in, w_uq, w_uk, w_uv):
    half = MLA_ROPE_DIM // 2
    d = w_in.shape[0]
    kr = w_in[:, SPLIT_KV:SPLIT_KR]
    z = lambda n: jnp.zeros((d, n), w_in.dtype)
    kr_full = jnp.concatenate([z(MLA_NOPE_DIM), kr, z(HEAD_PAD - MLA_QK_DIM)], axis=1)
    kr_swap = jnp.concatenate([z(MLA_NOPE_DIM), -kr[:, half:], kr[:, :half], z(HEAD_PAD - MLA_QK_DIM)], axis=1)
    w_in_ext = jnp.concatenate([w_in[:, :SPLIT_KV], kr_full, kr_swap, w_in[:, SPLIT_KR:]], axis=1)

    wq = w_uq.reshape(MLA_Q_RANK, MLA_HEADS, MLA_QK_DIM)
    zq = lambda n: jnp.zeros((MLA_Q_RANK, MLA_HEADS, n), w_uq.dtype)
    wq_full = jnp.concatenate([wq, zq(HEAD_PAD - MLA_QK_DIM)], axis=2)
    wq_swap = jnp.concatenate([zq(MLA_NOPE_DIM), -wq[:, :, MLA_NOPE_DIM + half:],
                               wq[:, :, MLA_NOPE_DIM:MLA_NOPE_DIM + half], zq(HEAD_PAD - MLA_QK_DIM)], axis=2)
    wq_ext = jnp.concatenate([wq_full.reshape(MLA_Q_RANK, QK_PAD), wq_swap.reshape(MLA_Q_RANK, QK_PAD)], axis=1)

    wk = w_uk.reshape(MLA_KV_RANK, MLA_HEADS, MLA_NOPE_DIM)
    wk_full = jnp.concatenate([wk, jnp.zeros((MLA_KV_RANK, MLA_HEADS, HEAD_PAD - MLA_NOPE_DIM), w_uk.dtype)], axis=2)
    wv = w_uv.reshape(MLA_KV_RANK, MLA_HEADS, MLA_V_DIM)
    wv_full = jnp.concatenate([wv, jnp.zeros((MLA_KV_RANK, MLA_HEADS, HEAD_PAD - MLA_V_DIM), w_uv.dtype)], axis=2)
    wkv_ext = jnp.concatenate([wk_full.reshape(MLA_KV_RANK, QK_PAD), wv_full.reshape(MLA_KV_RANK, QK_PAD)], axis=1)
    return w_in_ext.astype(BF16), wq_ext.astype(BF16), wkv_ext.astype(BF16)


def _rope_freq_column():
    inv_freq = ROPE_BASE ** (-jnp.arange(0, MLA_ROPE_DIM, 2, dtype=F32) / MLA_ROPE_DIM)
    return inv_freq[:, None]


def _value_one_lanes():
    one_hot = (jnp.arange(HEAD_PAD) == MLA_V_DIM).astype(F32)
    return jnp.tile(one_hot, MLA_HEADS)[None, :]


def kernel(x, mem, positions, w_in, q_norm_g, w_uq, kv_norm_g, w_uk, w_uv, conv_w, conv_b, conv_ln_g, conv_ln_b,
           w_o, ln1_g, ln1_b, xa_w_q, xa_w_k, xa_w_v, xa_w_o, ln2_g, ln2_b, w_router, w_gate, w_up, w_down,
           ln3_g, ln3_b):
    assert w_in.shape[0] == DEPTH == 1
    b, s, d = x.shape
    cap = CAPACITY_FACTOR * s // N_EXPERTS
    tm = 512

    w_in_ext, wq_ext, wkv_ext = _extend_weights(w_in[0], w_uq[0], w_uk[0], w_uv[0])
    posf = positions.astype(F32)[:, None, :]
    q, k, v, u_pre = _front(x, posf, w_in_ext, q_norm_g, wq_ext, kv_norm_g, wkv_ext, _rope_freq_column(),
                             _value_one_lanes(), 1024)
    attn = _attention(q, k, v, 1024)
    kx, vx = _mem_kv(mem, xa_w_k[0].astype(BF16), xa_w_v[0].astype(BF16))
    h2, h2b, aff = _mid(u_pre, attn, x, conv_w[0], conv_b, conv_ln_g, conv_ln_b, w_o[0].astype(BF16), ln1_g, ln1_b,
                        xa_w_q[0].astype(BF16), kx, vx, xa_w_o[0].astype(BF16), ln2_g, ln2_b,
                        _split_router(w_router[0]), tm)

    aff2d = aff.reshape(b * N_EXPERTS, s)
    slot, slot_t, off_tab = _topk(aff2d, cap)
    n_j = s // ROUTE_CHUNK
    base, over = _route_tables(off_tab[:, :n_j + 1], b, N_EXPERTS, cap)
    base_flat, over_flat = base.reshape(-1), over.reshape(-1)
    by_chunk = lambda a: a.reshape(b * N_EXPERTS, n_j, ROUTE_CHUNK)
    xs, gate = _dispatch(base_flat, over_flat, h2b, by_chunk(slot), by_chunk(aff2d), N_EXPERTS, cap)
    y = _experts(xs.reshape(N_EXPERTS, b * cap, d), gate.reshape(N_EXPERTS, 1, b * cap),
                 w_gate[0], w_up[0], w_down[0])
    base_t = base.T.astype(F32).reshape(n_j, 1, b * N_EXPERTS)
    return _combine(base_flat, over_flat, slot_t, base_t, y.reshape(N_EXPERTS + 1, b, cap, d), h2, ln3_g, ln3_b, cap)
```

```python
import functools
import math

import jax
import jax.numpy as jnp
from jax import lax
from jax.experimental import pallas as pl
from jax.experimental.pallas import tpu as pltpu

F32 = jnp.float32
BF16 = jnp.bfloat16

D_MODEL = 1024
MLA_HEADS = 8
MLA_NOPE_DIM = 64
MLA_ROPE_DIM = 32
MLA_QK_DIM = MLA_NOPE_DIM + MLA_ROPE_DIM
MLA_V_DIM = 64
MLA_Q_RANK = 256
MLA_KV_RANK = 128
MLA_WIDTH = MLA_HEADS * MLA_V_DIM
CONV_CH = D_MODEL - MLA_WIDTH
CONV_WIDTH = 31
CONV_PAD = (CONV_WIDTH - 1) // 2
ROPE_BASE = 10000.0
MEM_HEADS = 4
MEM_HEAD_DIM = D_MODEL // MEM_HEADS
MEM_Q_SCALE = 1.0 / math.sqrt(MEM_HEAD_DIM)
assert math.frexp(MEM_Q_SCALE)[0] == 0.5
N_EXPERTS = 16
EXPERT_FF = 2048
CAPACITY_FACTOR = 2
NORM_EPS = 1e-5
DEPTH = 1
DEEPNORM_ALPHA = (2.0 * DEPTH) ** 0.25
SPLIT_Q = MLA_Q_RANK
SPLIT_KV = SPLIT_Q + MLA_KV_RANK
SPLIT_KR = SPLIT_KV + MLA_ROPE_DIM

HEAD_PAD = 128
QK_PAD = MLA_HEADS * HEAD_PAD
OFF_CQ = 0
OFF_CKV = OFF_CQ + MLA_Q_RANK
OFF_KR = OFF_CKV + MLA_KV_RANK
OFF_KRS = OFF_KR + HEAD_PAD
OFF_A = OFF_KRS + HEAD_PAD
OFF_G = OFF_A + CONV_CH
IN_EXT = OFF_G + CONV_CH

SUBLANES = 8
LANES = 128
BF16_ROWS = 16
ROUTE_CHUNK = 256
ROUTE_WIN = 64
ROUTE_INNER = 4
HALO = 16
VMEM_LIMIT = 56 * 1024 * 1024


def _cparams(sem):
    return pltpu.CompilerParams(dimension_semantics=sem, vmem_limit_bytes=VMEM_LIMIT)


def _layer_norm(v, g, b):
    mu = jnp.mean(v, axis=-1, keepdims=True)
    d = v - mu
    var = jnp.mean(d * d, axis=-1, keepdims=True)
    return d * lax.rsqrt(var + NORM_EPS) * g + b


def _rms_norm(v, g):
    return v * lax.rsqrt(jnp.mean(v * v, axis=-1, keepdims=True) + NORM_EPS) * g


def _sigmoid(v):
    return 1.0 / (1.0 + jnp.exp(-v))


def _dot(a, b):
    return jnp.dot(a, b, preferred_element_type=F32)


def _dot_nt(a, b, precision=None):
    return lax.dot_general(a, b, (((1,), (1,)), ((), ())), preferred_element_type=F32, precision=precision)


def _front_kernel(x_ref, pos_ref, win_ref, qg_ref, wq_ref, kvg_ref, wkv_ref, invf_ref, vone_ref,
                  q_out, k_out, v_out, u_out):
    x = x_ref[0].astype(BF16)
    hc = _dot(x, win_ref[...])
    cqn = _rms_norm(hc[:, OFF_CQ:OFF_CKV], qg_ref[...])
    qq = _dot(cqn.astype(BF16), wq_ref[...])
    ckvn = _rms_norm(hc[:, OFF_CKV:OFF_KR], kvg_ref[...])
    kv = _dot(ckvn.astype(BF16), wkv_ref[...])
    tm = x_ref.shape[1]
    ang = invf_ref[...] * pos_ref[0]
    cos_h, sin_h = jnp.cos(ang), jnp.sin(ang)
    ones = lambda n: jnp.ones((n, tm), F32)
    zeros = lambda n: jnp.zeros((n, tm), F32)
    cos = jnp.concatenate([ones(MLA_NOPE_DIM), cos_h, cos_h, ones(HEAD_PAD - MLA_QK_DIM)], axis=0).T
    sin = jnp.concatenate([zeros(MLA_NOPE_DIM), sin_h, sin_h, zeros(HEAD_PAD - MLA_QK_DIM)], axis=0).T
    krot = hc[:, OFF_KR:OFF_KRS] * cos + hc[:, OFF_KRS:OFF_A] * sin
    scale = math.log2(math.e) / math.sqrt(MLA_QK_DIM)
    for h in range(MLA_HEADS):
        lo, hi = h * HEAD_PAD, (h + 1) * HEAD_PAD
        qh = (qq[:, lo:hi] * cos + qq[:, QK_PAD + lo:QK_PAD + hi] * sin) * scale
        q_out[0, :, lo:hi] = qh.astype(BF16)
        k_out[0, :, lo:hi] = (kv[:, lo:hi] + krot).astype(BF16)
    v_out[0] = (kv[:, QK_PAD:] + vone_ref[...]).astype(BF16)
    u_out[0] = hc[:, OFF_A:OFF_G] * _sigmoid(hc[:, OFF_G:IN_EXT])


def _front(x, posf, w_in_ext, qg, wq_ext, kvg, wkv_ext, invf, vone, tm):
    b, s, d = x.shape
    const = lambda bi, i: (0, 0)
    return pl.pallas_call(
        _front_kernel,
        grid=(b, s // tm),
        in_specs=[
            pl.BlockSpec((1, tm, d), lambda bi, i: (bi, i, 0)),
            pl.BlockSpec((1, 1, tm), lambda bi, i: (bi, 0, i)),
            pl.BlockSpec(w_in_ext.shape, const),
            pl.BlockSpec(qg.shape, const),
            pl.BlockSpec(wq_ext.shape, const),
            pl.BlockSpec(kvg.shape, const),
            pl.BlockSpec(wkv_ext.shape, const),
            pl.BlockSpec(invf.shape, const),
            pl.BlockSpec(vone.shape, const),
        ],
        out_specs=[
            pl.BlockSpec((1, tm, QK_PAD), lambda bi, i: (bi, i, 0)),
            pl.BlockSpec((1, tm, QK_PAD), lambda bi, i: (bi, i, 0)),
            pl.BlockSpec((1, tm, QK_PAD), lambda bi, i: (bi, i, 0)),
            pl.BlockSpec((1, tm, CONV_CH), lambda bi, i: (bi, i, 0)),
        ],
        out_shape=[
            jax.ShapeDtypeStruct((b, s, QK_PAD), BF16),
            jax.ShapeDtypeStruct((b, s, QK_PAD), BF16),
            jax.ShapeDtypeStruct((b, s, QK_PAD), BF16),
            jax.ShapeDtypeStruct((b, s, CONV_CH), F32),
        ],
        compiler_params=_cparams(("parallel", "parallel")),
        name="front",
    )(x, posf, w_in_ext, qg, wq_ext, kvg, wkv_ext, invf, vone)


def _attn_kernel(q_ref, k_ref, v_ref, o_ref):
    for h in range(MLA_HEADS):
        lo, hi = h * HEAD_PAD, (h + 1) * HEAD_PAD
        sc = _dot_nt(q_ref[0, :, lo:hi], k_ref[0, :, lo:hi])
        p = jnp.exp2(sc - jnp.max(sc, axis=-1, keepdims=True))
        pv = _dot(p.astype(BF16), v_ref[0, :, lo:hi])
        o = pv[:, :MLA_V_DIM] / pv[:, MLA_V_DIM:MLA_V_DIM + 1]
        o_ref[0, :, h * MLA_V_DIM:(h + 1) * MLA_V_DIM] = o.astype(BF16)


def _attention(q, k, v, tq):
    b, s, _ = q.shape
    return pl.pallas_call(
        _attn_kernel,
        grid=(b, s // tq),
        in_specs=[
            pl.BlockSpec((1, tq, QK_PAD), lambda bi, i: (bi, i, 0)),
            pl.BlockSpec((1, s, QK_PAD), lambda bi, i: (bi, 0, 0)),
            pl.BlockSpec((1, s, QK_PAD), lambda bi, i: (bi, 0, 0)),
        ],
        out_specs=pl.BlockSpec((1, tq, MLA_WIDTH), lambda bi, i: (bi, i, 0)),
        out_shape=jax.ShapeDtypeStruct((b, s, MLA_WIDTH), BF16),
        compiler_params=_cparams(("parallel", "parallel")),
        name="attn",
    )(q, k, v)


def _mem_kv_kernel(mem_ref, wk_ref, wv_ref, k_out, v_out):
    m = mem_ref[0].astype(BF16)
    k_out[0] = _dot(m, wk_ref[...]).astype(BF16)
    v_out[0] = _dot(m, wv_ref[...]).astype(BF16)


def _mem_kv(mem, wk, wv):
    b, m, d = mem.shape
    const = lambda bi: (0, 0)
    return pl.pallas_call(
        _mem_kv_kernel,
        grid=(b,),
        in_specs=[
            pl.BlockSpec((1, m, d), lambda bi: (bi, 0, 0)),
            pl.BlockSpec(wk.shape, const),
            pl.BlockSpec(wv.shape, const),
        ],
        out_specs=[pl.BlockSpec((1, m, d), lambda bi: (bi, 0, 0))] * 2,
        out_shape=[jax.ShapeDtypeStruct((b, m, d), BF16)] * 2,
        compiler_params=_cparams(("parallel",)),
        name="mem_kv",
    )(mem, wk, wv)


CONV_SUB = 64


def _mid_kernel(ucur_ref, uprev_ref, unext_ref, attn_ref, x_ref, cw_ref, cb_ref, cg_ref, cbeta_ref, wo_ref,
                g1_ref, b1_ref, wq_ref, kx_ref, vx_ref, xwo_ref, g2_ref, b2_ref, wr_ref,
                h2_out, h2b_out, aff_out, win_ref, shift_ref, u_scr, o_scr):
    tm = ucur_ref.shape[1]
    i = pl.program_id(1)
    last = pl.num_programs(1) - 1
    win_ref[0:HALO, :] = jnp.where(i > 0, uprev_ref[0], 0.0)
    win_ref[HALO:HALO + tm, :] = ucur_ref[0]
    win_ref[HALO + tm:2 * HALO + tm, :] = jnp.where(i < last, unext_ref[0], 0.0)
    span = tm + 2 * HALO - SUBLANES
    shift_ref[0] = win_ref[...]
    for j in range(1, SUBLANES):
        shift_ref[j, 0:span, :] = win_ref[j:j + span, :]

    for r in range(tm // CONV_SUB):
        acc = jnp.broadcast_to(cb_ref[...], (CONV_SUB, CONV_CH))
        for t in range(CONV_WIDTH):
            off = HALO - CONV_PAD + t
            row = r * CONV_SUB + (off // SUBLANES) * SUBLANES
            acc = acc + shift_ref[off % SUBLANES, row:row + CONV_SUB, :] * cw_ref[t:t + 1, :]
        y = _layer_norm(acc, cg_ref[...], cbeta_ref[...])
        u_scr[r * CONV_SUB:(r + 1) * CONV_SUB, :] = (y * _sigmoid(y)).astype(BF16)
    mix = _dot(attn_ref[0], wo_ref[0:MLA_WIDTH, :]) + _dot(u_scr[...], wo_ref[MLA_WIDTH:, :])
    h1 = _layer_norm(DEEPNORM_ALPHA * x_ref[0] + mix, g1_ref[...], b1_ref[...])
    q = _dot(h1.astype(BF16), wq_ref[...]).astype(BF16)
    for h in range(MEM_HEADS):
        lo, hi = h * MEM_HEAD_DIM, (h + 1) * MEM_HEAD_DIM
        sc = _dot_nt(q[:, lo:hi], kx_ref[0, :, lo:hi])
        p = jnp.exp(sc - jnp.max(sc, axis=-1, keepdims=True))
        l = jnp.sum(p, axis=-1, keepdims=True)
        o_scr[:, lo:hi] = (_dot(p.astype(BF16), vx_ref[0, :, lo:hi]) / l).astype(BF16)
    xa = _dot(o_scr[...], xwo_ref[...])
    h2 = _layer_norm(DEEPNORM_ALPHA * h1 + xa, g2_ref[...], b2_ref[...])
    h2_out[0] = h2
    h2_hi = h2.astype(BF16)
    h2b_out[0] = h2_hi
    h2_lo = (h2 - h2_hi.astype(F32)).astype(BF16)
    hi_terms = _dot(h2_hi, wr_ref[...])
    logits = hi_terms[:, :LANES] + hi_terms[:, LANES:] + _dot(h2_lo, wr_ref[:, :LANES])
    lt = logits.T[0:aff_out.shape[1], :]
    ex = jnp.exp(lt - jnp.max(lt, axis=0, keepdims=True))
    aff_out[0] = ex / jnp.sum(ex, axis=0, keepdims=True)


def _mid(u_pre, attn, x, conv_w, conv_b, conv_g, conv_beta, w_o, g1, b1, wq, kx, vx, xwo, g2, b2, wr_split, tm):
    b, s, d = x.shape
    m = kx.shape[1]
    nh = tm // HALO
    const = lambda bi, i: (0, 0)
    tile = lambda w: pl.BlockSpec((1, tm, w), lambda bi, i: (bi, i, 0))
    whole = lambda arr: pl.BlockSpec(arr.shape, const)
    return pl.pallas_call(
        _mid_kernel,
        grid=(b, s // tm),
        in_specs=[
            tile(CONV_CH),
            pl.BlockSpec((1, HALO, CONV_CH), lambda bi, i: (bi, jnp.maximum(i * nh - 1, 0), 0)),
            pl.BlockSpec((1, HALO, CONV_CH), lambda bi, i: (bi, jnp.minimum((i + 1) * nh, s // HALO - 1), 0)),
            tile(MLA_WIDTH),
            tile(d),
            whole(conv_w), whole(conv_b), whole(conv_g), whole(conv_beta), whole(w_o), whole(g1), whole(b1),
            whole(wq),
            pl.BlockSpec((1, m, d), lambda bi, i: (bi, 0, 0)),
            pl.BlockSpec((1, m, d), lambda bi, i: (bi, 0, 0)),
            whole(xwo), whole(g2), whole(b2), whole(wr_split),
        ],
        out_specs=[
            tile(d),
            tile(d),
            pl.BlockSpec((1, N_EXPERTS, tm), lambda bi, i: (bi, 0, i)),
        ],
        out_shape=[
            jax.ShapeDtypeStruct((b, s, d), F32),
            jax.ShapeDtypeStruct((b, s, d), BF16),
            jax.ShapeDtypeStruct((b, N_EXPERTS, s), F32),
        ],
        scratch_shapes=[
            pltpu.VMEM((tm + 2 * HALO, CONV_CH), F32),
            pltpu.VMEM((SUBLANES, tm + 2 * HALO, CONV_CH), F32),
            pltpu.VMEM((tm, CONV_CH), BF16),
            pltpu.VMEM((tm, d), BF16),
        ],
        compiler_params=_cparams(("parallel", "parallel")),
        name="mid",
    )(u_pre, u_pre, u_pre, attn, x, conv_w, conv_b, conv_g, conv_beta, w_o, g1, b1, wq, kx, vx, xwo, g2, b2, wr_split)


def _split_router(w_router):
    hi = w_router.astype(BF16)
    lo = (w_router - hi.astype(F32)).astype(BF16)
    pad = jnp.zeros((w_router.shape[0], LANES - w_router.shape[1]), BF16)
    return jnp.concatenate([hi, pad, lo, pad], axis=1)


BISECT_STEPS_PER_CHECK = 4


def _topk_kernel(aff_ref, slot_out, slot_t_out, off_out, *, cap):
    aff = aff_ref[...]
    rows, s = aff.shape
    capf = jnp.float32(cap)

    def not_done(carry):
        return carry[2] > 0

    def halve(_, bounds):
        lo, hi = bounds
        mid = 0.5 * (lo + hi)
        take = jnp.sum(jnp.where(aff >= mid, 1.0, 0.0), axis=1, keepdims=True) >= capf
        return jnp.where(take, mid, lo), jnp.where(take, hi, mid)

    def bisect(carry):
        lo, hi = lax.fori_loop(0, BISECT_STEPS_PER_CHECK, halve, carry[:2])
        smallest_in = jnp.min(jnp.where(aff >= lo, aff, jnp.inf), axis=1, keepdims=True)
        largest_in = jnp.max(jnp.where(aff < hi, aff, -jnp.inf), axis=1, keepdims=True)
        open_rows = jnp.sum(jnp.where(smallest_in == largest_in, 0.0, 1.0))
        return lo, hi, open_rows.astype(jnp.int32)

    _, hi, _ = lax.while_loop(not_done, bisect,
                              (jnp.zeros((rows, 1), F32), jnp.full((rows, 1), 2.0, F32), jnp.int32(1)))
    th = jnp.max(jnp.where(aff < hi, aff, -jnp.inf), axis=1, keepdims=True)
    gt = aff > th
    eq = aff == th
    n_gt = jnp.sum(jnp.where(gt, 1.0, 0.0), axis=1, keepdims=True)
    tri = jnp.where(lax.broadcasted_iota(jnp.int32, (s, s), 0) < lax.broadcasted_iota(jnp.int32, (s, s), 1),
                    1.0, 0.0).astype(BF16)
    tie_rank = _dot(jnp.where(eq, 1.0, 0.0).astype(BF16), tri)
    sel = jnp.logical_or(gt, jnp.logical_and(eq, tie_rank < (capf - n_gt)))
    pos = _dot(jnp.where(sel, 1.0, 0.0).astype(BF16), tri)
    slot = jnp.where(sel, pos, -1.0)
    slot_out[...] = slot
    slot_t_out[...] = slot.T
    lanes = off_out.shape[1]
    before = jnp.where(lax.broadcasted_iota(jnp.int32, (s, lanes), 0)
                       < ROUTE_CHUNK * lax.broadcasted_iota(jnp.int32, (s, lanes), 1), 1.0, 0.0).astype(BF16)
    off_out[...] = _dot(jnp.where(sel, 1.0, 0.0).astype(BF16), before)


def _topk(aff2d, cap):
    rows, s = aff2d.shape
    return pl.pallas_call(
        functools.partial(_topk_kernel, cap=cap),
        out_shape=[
            jax.ShapeDtypeStruct((rows, s), F32),
            jax.ShapeDtypeStruct((s, rows), F32),
            jax.ShapeDtypeStruct((rows, LANES), F32),
        ],
        compiler_params=pltpu.CompilerParams(vmem_limit_bytes=VMEM_LIMIT),
        name="topk",
    )(aff2d)


def _route_tables(off_tab, n_b, n_e, cap):
    n_j = off_tab.shape[1] - 1
    off = off_tab.astype(jnp.int32)
    base = jnp.minimum((off[:, :n_j] // BF16_ROWS) * BF16_ROWS, cap - ROUTE_WIN)
    over = jnp.any((off[:, 1:] - base > ROUTE_WIN).reshape(n_b, n_e, n_j), axis=1)
    return base, over.astype(jnp.int32)


def _dispatch_kernel(base_ref, over_ref, h2b_ref, slot_ref, aff_ref, xs_out, gate_out, p_scr, gate_scr, *, cap):
    bi = pl.program_id(0)
    jo = pl.program_id(1)
    n_j = pl.num_programs(1) * ROUTE_INNER
    n_e, _, t = slot_ref.shape

    @pl.when(jo == 0)
    def _():
        xs_out[...] = jnp.zeros(xs_out.shape, xs_out.dtype)
        gate_scr[...] = jnp.zeros(gate_scr.shape, gate_scr.dtype)

    def chunk(ji, carry):
        j = jo * ROUTE_INNER + ji
        h = h2b_ref[0, pl.ds(pl.multiple_of(ji * t, t), t), :]
        slot_row = lambda e: slot_ref[e, pl.ds(j, 1), :]
        aff_row = lambda e: aff_ref[e, pl.ds(j, 1), :]

        @pl.when(over_ref[bi * n_j + j] == 0)
        def _():
            w_iota = lax.broadcasted_iota(jnp.int32, (ROUTE_WIN, t), 0).astype(F32)
            bases = []
            for e in range(n_e):
                base = pl.multiple_of(base_ref[(bi * n_e + e) * n_j + j], BF16_ROWS)
                hit = slot_row(e) == w_iota + base.astype(F32)
                p_scr[e * ROUTE_WIN:(e + 1) * ROUTE_WIN, :] = jnp.where(hit, 1.0, 0.0).astype(BF16)
                gate_scr[e, pl.ds(base, ROUTE_WIN), :] += jnp.sum(jnp.where(hit, aff_row(e), 0.0),
                                                                  axis=1, keepdims=True)
                bases.append(base)
            picked = _dot(p_scr[...], h)
            for e in range(n_e):
                rows = pl.ds(bases[e], ROUTE_WIN)
                xs_out[e, 0, rows, :] += picked[e * ROUTE_WIN:(e + 1) * ROUTE_WIN, :].astype(BF16)

        @pl.when(over_ref[bi * n_j + j] != 0)
        def _():
            c_iota = lax.broadcasted_iota(jnp.int32, (cap, t), 0).astype(F32)
            for e in range(n_e):
                hit = slot_row(e) == c_iota
                xs_out[e, 0] += _dot(jnp.where(hit, 1.0, 0.0).astype(BF16), h).astype(BF16)
                gate_scr[e] += jnp.sum(jnp.where(hit, aff_row(e), 0.0), axis=1, keepdims=True)

        return carry

    lax.fori_loop(0, ROUTE_INNER, chunk, 0)

    @pl.when(jo == pl.num_programs(1) - 1)
    def _():
        eye = lax.broadcasted_iota(jnp.int32, (cap, cap), 0) == lax.broadcasted_iota(jnp.int32, (cap, cap), 1)
        for e in range(n_e):
            gate_out[e, 0] = jnp.sum(jnp.where(eye, gate_scr[e], 0.0), axis=0, keepdims=True)


def _dispatch(base_flat, over_flat, h2b, slot3d, aff3d, n_e, cap):
    b, s, d = h2b.shape
    t = ROUTE_CHUNK
    rows = ROUTE_INNER * t
    grid_spec = pltpu.PrefetchScalarGridSpec(
        num_scalar_prefetch=2,
        grid=(b, s // rows),
        in_specs=[
            pl.BlockSpec((1, rows, d), lambda bi, jo, *_: (bi, jo, 0)),
            pl.BlockSpec((n_e, s // t, t), lambda bi, jo, *_: (bi, 0, 0)),
            pl.BlockSpec((n_e, s // t, t), lambda bi, jo, *_: (bi, 0, 0)),
        ],
        out_specs=[
            pl.BlockSpec((n_e, 1, cap, d), lambda bi, jo, *_: (0, bi, 0, 0)),
            pl.BlockSpec((n_e, 1, 1, cap), lambda bi, jo, *_: (0, bi, 0, 0)),
        ],
        scratch_shapes=[pltpu.VMEM((n_e * ROUTE_WIN, t), BF16), pltpu.VMEM((n_e, cap, 1), F32)],
    )
    return pl.pallas_call(
        functools.partial(_dispatch_kernel, cap=cap),
        grid_spec=grid_spec,
        out_shape=[
            jax.ShapeDtypeStruct((n_e, b, cap, d), BF16),
            jax.ShapeDtypeStruct((n_e, b, 1, cap), F32),
        ],
        compiler_params=_cparams(("parallel", "arbitrary")),
        name="dispatch",
    )(base_flat, over_flat, h2b, slot3d, aff3d)


FFN_ROWS = 512
FFN_CHUNK = 512


def _experts_kernel(xs_ref, gate_ref, wg_ref, wu_ref, wd_ref, y_out, wg_s, wu_s, wd_s, hid_s):
    e = pl.program_id(0)
    f = pl.program_id(1)
    n_chunks = wg_s.shape[1]

    def stage():
        slot = e % 2
        wg_s[slot, f] = wg_ref[0].astype(BF16)
        wu_s[slot, f] = wu_ref[0].astype(BF16)
        wd_s[slot, f] = wd_ref[0].astype(BF16)

    @pl.when(e == 0)
    def _():
        stage()
        y_out[0] = jnp.zeros(y_out.shape[1:], y_out.dtype)

    @pl.when(e > 0)
    def _():
        stage()
        slot = (e - 1) % 2
        xs = xs_ref[0]
        for c in range(n_chunks):
            g = _dot(xs, wg_s[slot, c])
            u = _dot(xs, wu_s[slot, c])
            hid_s[:, c * FFN_CHUNK:(c + 1) * FFN_CHUNK] = (g * _sigmoid(g) * u).astype(BF16)
        y = _dot(hid_s[:, 0:FFN_CHUNK], wd_s[slot, 0])
        for c in range(1, n_chunks):
            y = y + _dot(hid_s[:, c * FFN_CHUNK:(c + 1) * FFN_CHUNK], wd_s[slot, c])
        rows = xs.shape[0]
        eye = lax.broadcasted_iota(jnp.int32, (rows, rows), 0) == lax.broadcasted_iota(jnp.int32, (rows, rows), 1)
        gate_col = jnp.sum(jnp.where(eye, gate_ref[0], 0.0), axis=1, keepdims=True)
        y_out[0] = (y * gate_col).astype(BF16)


def _experts(xs, gate, w_gate, w_up, w_down):
    e, n, d = xs.shape
    ff = w_gate.shape[2]
    n_chunks = ff // FFN_CHUNK
    assert n // FFN_ROWS == n_chunks
    prev = lambda ei, fi: (jnp.maximum(ei - 1, 0), fi, 0)
    cur = lambda ei: jnp.minimum(ei, e - 1)
    return pl.pallas_call(
        _experts_kernel,
        grid=(e + 1, n_chunks),
        in_specs=[
            pl.BlockSpec((1, FFN_ROWS, d), prev),
            pl.BlockSpec((1, 1, FFN_ROWS), lambda ei, fi: (jnp.maximum(ei - 1, 0), 0, fi)),
            pl.BlockSpec((1, d, FFN_CHUNK), lambda ei, fi: (cur(ei), 0, fi)),
            pl.BlockSpec((1, d, FFN_CHUNK), lambda ei, fi: (cur(ei), 0, fi)),
            pl.BlockSpec((1, FFN_CHUNK, d), lambda ei, fi: (cur(ei), fi, 0)),
        ],
        out_specs=pl.BlockSpec((1, FFN_ROWS, d), lambda ei, fi: (jnp.where(ei == 0, e, ei - 1), fi, 0)),
        out_shape=jax.ShapeDtypeStruct((e + 1, n, d), BF16),
        scratch_shapes=[
            pltpu.VMEM((2, n_chunks, d, FFN_CHUNK), BF16),
            pltpu.VMEM((2, n_chunks, d, FFN_CHUNK), BF16),
            pltpu.VMEM((2, n_chunks, FFN_CHUNK, d), BF16),
            pltpu.VMEM((FFN_ROWS, ff), BF16),
        ],
        compiler_params=_cparams(("arbitrary", "arbitrary")),
        name="experts",
    )(xs, gate, w_gate, w_up, w_down)


def _combine_kernel(base_ref, over_ref, slot_t_ref, base_t_ref, y_ref, h2_ref, g3_ref, b3_ref, out_ref, ywin_scr,
                    *, cap):
    bi = pl.program_id(0)
    jo = pl.program_id(1)
    n_j = pl.num_programs(1) * ROUTE_INNER
    n_e = y_ref.shape[0]
    rows = slot_t_ref.shape[1]
    t = ROUTE_CHUNK

    def spread_cols(per_expert):
        width = n_e * per_expert
        col = lax.broadcasted_iota(jnp.int32, (rows, width), 1) // per_expert
        return jnp.where(lax.broadcasted_iota(jnp.int32, (rows, width), 0) == bi * n_e + col, 1.0, 0.0).astype(BF16)

    def lane_in_group(per_expert):
        return (lax.broadcasted_iota(jnp.int32, (1, n_e * per_expert), 1) % per_expert).astype(F32)

    win_spread = spread_cols(ROUTE_WIN)
    win_lane = lane_in_group(ROUTE_WIN)

    def chunk(ji, carry):
        j = jo * ROUTE_INNER + ji
        tok = pl.ds(pl.multiple_of(ji * t, t), t)

        def finish(ff):
            out_ref[0, tok, :] = _layer_norm(DEEPNORM_ALPHA * h2_ref[0, tok, :] + ff, g3_ref[...], b3_ref[...])

        slot_bf = slot_t_ref[tok, :].astype(BF16)

        @pl.when(over_ref[bi * n_j + j] == 0)
        def _():
            slot_wide = _dot(slot_bf, win_spread)
            base_wide = _dot(jnp.broadcast_to(base_t_ref[ji], (SUBLANES, rows)).astype(BF16), win_spread)[0:1, :]
            onehot = jnp.where(slot_wide - base_wide == win_lane, 1.0, 0.0).astype(BF16)
            for e in range(n_e):
                base = pl.multiple_of(base_ref[(bi * n_e + e) * n_j + j], BF16_ROWS)
                ywin_scr[e * ROUTE_WIN:(e + 1) * ROUTE_WIN, :] = y_ref[e, 0, pl.ds(base, ROUTE_WIN), :]
            finish(_dot(onehot, ywin_scr[...]))

        @pl.when(over_ref[bi * n_j + j] != 0)
        def _():
            slot_wide = _dot(slot_bf, spread_cols(cap))
            onehot = jnp.where(slot_wide == lane_in_group(cap), 1.0, 0.0).astype(BF16)
            finish(_dot(onehot, y_ref[:, 0].reshape(n_e * cap, y_ref.shape[3])))

        return carry

    lax.fori_loop(0, ROUTE_INNER, chunk, 0)


def _combine(base_flat, over_flat, slot_t, base_t, y4, h2, g3, b3, cap):
    b, s, d = h2.shape
    e = y4.shape[0] - 1
    rows = ROUTE_INNER * ROUTE_CHUNK
    const = lambda bi, jo, *_: (0, 0)
    grid_spec = pltpu.PrefetchScalarGridSpec(
        num_scalar_prefetch=2,
        grid=(b, s // rows),
        in_specs=[
            pl.BlockSpec((rows, slot_t.shape[1]), lambda bi, jo, *_: (jo, 0)),
            pl.BlockSpec((ROUTE_INNER, 1, base_t.shape[2]), lambda bi, jo, *_: (jo, 0, 0)),
            pl.BlockSpec((e, 1, cap, d), lambda bi, jo, *_: (0, bi, 0, 0)),
            pl.BlockSpec((1, rows, d), lambda bi, jo, *_: (bi, jo, 0)),
            pl.BlockSpec(g3.shape, const),
            pl.BlockSpec(b3.shape, const),
        ],
        out_specs=pl.BlockSpec((1, rows, d), lambda bi, jo, *_: (bi, jo, 0)),
        scratch_shapes=[pltpu.VMEM((e * ROUTE_WIN, d), BF16)],
    )
    return pl.pallas_call(
        functools.partial(_combine_kernel, cap=cap),
        grid_spec=grid_spec,
        out_shape=jax.ShapeDtypeStruct((b, s, d), F32),
        compiler_params=_cparams(("parallel", "parallel")),
        name="combine",
    )(base_flat, over_flat, slot_t, base_t, y4, h2, g3, b3)


def _extend_weights(w_in, w_uq, w_uk, w_uv):
    half = MLA_ROPE_DIM // 2
    d = w_in.shape[0]
    w_t = w_in.T.astype(BF16)
    kr = w_t[SPLIT_KV:SPLIT_KR]
    z = lambda n: jnp.zeros((n, d), BF16)
    kr_full = jnp.concatenate([z(MLA_NOPE_DIM), kr, z(HEAD_PAD - MLA_QK_DIM)], axis=0)
    kr_swap = jnp.concatenate([z(MLA_NOPE_DIM), -kr[half:], kr[:half], z(HEAD_PAD - MLA_QK_DIM)], axis=0)
    w_in_ext = jnp.concatenate([w_t[:SPLIT_KV], kr_full, kr_swap, w_t[SPLIT_KR:]], axis=0).T

    wq = w_uq.reshape(MLA_Q_RANK, MLA_HEADS, MLA_QK_DIM)
    zq = lambda n: jnp.zeros((MLA_Q_RANK, MLA_HEADS, n), w_uq.dtype)
    wq_full = jnp.concatenate([wq, zq(HEAD_PAD - MLA_QK_DIM)], axis=2)
    wq_swap = jnp.concatenate([zq(MLA_NOPE_DIM), -wq[:, :, MLA_NOPE_DIM + half:],
                               wq[:, :, MLA_NOPE_DIM:MLA_NOPE_DIM + half], zq(HEAD_PAD - MLA_QK_DIM)], axis=2)
    wq_ext = jnp.concatenate([wq_full.reshape(MLA_Q_RANK, QK_PAD), wq_swap.reshape(MLA_Q_RANK, QK_PAD)], axis=1)

    wk = w_uk.reshape(MLA_KV_RANK, MLA_HEADS, MLA_NOPE_DIM)
    wk_full = jnp.concatenate([wk, jnp.zeros((MLA_KV_RANK, MLA_HEADS, HEAD_PAD - MLA_NOPE_DIM), w_uk.dtype)], axis=2)
    wv = w_uv.reshape(MLA_KV_RANK, MLA_HEADS, MLA_V_DIM)
    wv_full = jnp.concatenate([wv, jnp.zeros((MLA_KV_RANK, MLA_HEADS, HEAD_PAD - MLA_V_DIM), w_uv.dtype)], axis=2)
    wkv_ext = jnp.concatenate([wk_full.reshape(MLA_KV_RANK, QK_PAD), wv_full.reshape(MLA_KV_RANK, QK_PAD)], axis=1)
    return w_in_ext.astype(BF16), wq_ext.astype(BF16), wkv_ext.astype(BF16)


def _rope_freq_column():
    inv_freq = ROPE_BASE ** (-jnp.arange(0, MLA_ROPE_DIM, 2, dtype=F32) / MLA_ROPE_DIM)
    return inv_freq[:, None]


def _value_one_lanes():
    one_hot = (jnp.arange(HEAD_PAD) == MLA_V_DIM).astype(F32)
    return jnp.tile(one_hot, MLA_HEADS)[None, :]


def kernel(x, mem, positions, w_in, q_norm_g, w_uq, kv_norm_g, w_uk, w_uv, conv_w, conv_b, conv_ln_g, conv_ln_b,
           w_o, ln1_g, ln1_b, xa_w_q, xa_w_k, xa_w_v, xa_w_o, ln2_g, ln2_b, w_router, w_gate, w_up, w_down,
           ln3_g, ln3_b):
    assert w_in.shape[0] == DEPTH == 1
    b, s, d = x.shape
    cap = CAPACITY_FACTOR * s // N_EXPERTS
    tm = 512

    w_in_ext, wq_ext, wkv_ext = _extend_weights(w_in[0], w_uq[0], w_uk[0], w_uv[0])
    posf = positions.astype(F32)[:, None, :]
    q, k, v, u_pre = _front(x, posf, w_in_ext, q_norm_g, wq_ext, kv_norm_g, wkv_ext, _rope_freq_column(),
                             _value_one_lanes(), 1024)
    attn = _attention(q, k, v, 1024)
    kx, vx = _mem_kv(mem, xa_w_k[0].astype(BF16), xa_w_v[0].astype(BF16))
    h2, h2b, aff = _mid(u_pre, attn, x, conv_w[0], conv_b, conv_ln_g, conv_ln_b, w_o[0].astype(BF16), ln1_g, ln1_b,
                        (xa_w_q[0] * MEM_Q_SCALE).astype(BF16), kx, vx, xa_w_o[0].astype(BF16), ln2_g, ln2_b,
                        _split_router(w_router[0]), tm)

    aff2d = aff.reshape(b * N_EXPERTS, s)
    slot, slot_t, off_tab = _topk(aff2d, cap)
    n_j = s // ROUTE_CHUNK
    base, over = _route_tables(off_tab[:, :n_j + 1], b, N_EXPERTS, cap)
    base_flat, over_flat = base.reshape(-1), over.reshape(-1)
    by_chunk = lambda a: a.reshape(b * N_EXPERTS, n_j, ROUTE_CHUNK)
    xs, gate = _dispatch(base_flat, over_flat, h2b, by_chunk(slot), by_chunk(aff2d), N_EXPERTS, cap)
    y = _experts(xs.reshape(N_EXPERTS, b * cap, d), gate.reshape(N_EXPERTS, 1, b * cap),
                 w_gate[0], w_up[0], w_down[0])
    base_t = base.T.astype(F32).reshape(n_j, 1, b * N_EXPERTS)
    return _combine(base_flat, over_flat, slot_t, base_t, y.reshape(N_EXPERTS + 1, b, cap, d), h2, ln3_g, ln3_b, cap)
```

```python
import functools
import math

import jax
import jax.numpy as jnp
from jax import lax
from jax.experimental import pallas as pl
from jax.experimental.pallas import tpu as pltpu

F32 = jnp.float32
BF16 = jnp.bfloat16

D_MODEL = 1024
MLA_HEADS = 8
MLA_NOPE_DIM = 64
MLA_ROPE_DIM = 32
MLA_QK_DIM = MLA_NOPE_DIM + MLA_ROPE_DIM
MLA_V_DIM = 64
MLA_Q_RANK = 256
MLA_KV_RANK = 128
MLA_WIDTH = MLA_HEADS * MLA_V_DIM
CONV_CH = D_MODEL - MLA_WIDTH
CONV_WIDTH = 31
CONV_PAD = (CONV_WIDTH - 1) // 2
ROPE_BASE = 10000.0
MEM_HEADS = 4
MEM_HEAD_DIM = D_MODEL // MEM_HEADS
MEM_Q_SCALE = 1.0 / math.sqrt(MEM_HEAD_DIM)
assert math.frexp(MEM_Q_SCALE)[0] == 0.5
N_EXPERTS = 16
EXPERT_FF = 2048
CAPACITY_FACTOR = 2
NORM_EPS = 1e-5
DEPTH = 1
DEEPNORM_ALPHA = (2.0 * DEPTH) ** 0.25
SPLIT_Q = MLA_Q_RANK
SPLIT_KV = SPLIT_Q + MLA_KV_RANK
SPLIT_KR = SPLIT_KV + MLA_ROPE_DIM

HEAD_PAD = 128
QK_PAD = MLA_HEADS * HEAD_PAD
OFF_CQ = 0
OFF_CKV = OFF_CQ + MLA_Q_RANK
OFF_KR = OFF_CKV + MLA_KV_RANK
OFF_KRS = OFF_KR + HEAD_PAD
OFF_A = OFF_KRS + HEAD_PAD
OFF_G = OFF_A + CONV_CH
IN_EXT = OFF_G + CONV_CH

SUBLANES = 8
LANES = 128
BF16_ROWS = 16
ROUTE_CHUNK = 256
ROUTE_WIN = 64
ROUTE_INNER = 4
HALO = 16
VMEM_LIMIT = 56 * 1024 * 1024
FRONT_ROWS = 1024
ATTN_ROWS = 1024
MID_ROWS = 512
MEM_LEN = 256


def _cparams(sem):
    return pltpu.CompilerParams(dimension_semantics=sem, vmem_limit_bytes=VMEM_LIMIT)


def _layer_norm(v, g, b):
    mu = jnp.mean(v, axis=-1, keepdims=True)
    d = v - mu
    var = jnp.mean(d * d, axis=-1, keepdims=True)
    return d * lax.rsqrt(var + NORM_EPS) * g + b


def _rms_norm(v, g):
    return v * lax.rsqrt(jnp.mean(v * v, axis=-1, keepdims=True) + NORM_EPS) * g


def _sigmoid(v):
    return 1.0 / (1.0 + jnp.exp(-v))


def _dot(a, b):
    return jnp.dot(a, b, preferred_element_type=F32)


def _dot_nt(a, b, precision=None):
    return lax.dot_general(a, b, (((1,), (1,)), ((), ())), preferred_element_type=F32, precision=precision)


def _front_kernel(x_ref, pos_ref, win_ref, qg_ref, wq_ref, kvg_ref, wkv_ref, invf_ref, vone_ref,
                  q_out, k_out, v_out, u_out):
    x = x_ref[0].astype(BF16)
    hc = _dot(x, win_ref[...])
    cqn = _rms_norm(hc[:, OFF_CQ:OFF_CKV], qg_ref[...])
    qq = _dot(cqn.astype(BF16), wq_ref[...])
    ckvn = _rms_norm(hc[:, OFF_CKV:OFF_KR], kvg_ref[...])
    kv = _dot(ckvn.astype(BF16), wkv_ref[...])
    tm = x_ref.shape[1]
    ang = invf_ref[...] * pos_ref[0]
    cos_h, sin_h = jnp.cos(ang), jnp.sin(ang)
    ones = lambda n: jnp.ones((n, tm), F32)
    zeros = lambda n: jnp.zeros((n, tm), F32)
    cos = jnp.concatenate([ones(MLA_NOPE_DIM), cos_h, cos_h, ones(HEAD_PAD - MLA_QK_DIM)], axis=0).T
    sin = jnp.concatenate([zeros(MLA_NOPE_DIM), sin_h, sin_h, zeros(HEAD_PAD - MLA_QK_DIM)], axis=0).T
    krot = hc[:, OFF_KR:OFF_KRS] * cos + hc[:, OFF_KRS:OFF_A] * sin
    scale = math.log2(math.e) / math.sqrt(MLA_QK_DIM)
    for h in range(MLA_HEADS):
        lo, hi = h * HEAD_PAD, (h + 1) * HEAD_PAD
        qh = (qq[:, lo:hi] * cos + qq[:, QK_PAD + lo:QK_PAD + hi] * sin) * scale
        q_out[0, :, lo:hi] = qh.astype(BF16)
        k_out[0, :, lo:hi] = (kv[:, lo:hi] + krot).astype(BF16)
    v_out[0] = (kv[:, QK_PAD:] + vone_ref[...]).astype(BF16)
    u_out[0] = hc[:, OFF_A:OFF_G] * _sigmoid(hc[:, OFF_G:IN_EXT])


def _front(x, posf, w_in_ext, qg, wq_ext, kvg, wkv_ext, invf, vone, tm):
    b, s, d = x.shape
    const = lambda bi, i: (0, 0)
    return pl.pallas_call(
        _front_kernel,
        grid=(b, s // tm),
        in_specs=[
            pl.BlockSpec((1, tm, d), lambda bi, i: (bi, i, 0)),
            pl.BlockSpec((1, 1, tm), lambda bi, i: (bi, 0, i)),
            pl.BlockSpec(w_in_ext.shape, const),
            pl.BlockSpec(qg.shape, const),
            pl.BlockSpec(wq_ext.shape, const),
            pl.BlockSpec(kvg.shape, const),
            pl.BlockSpec(wkv_ext.shape, const),
            pl.BlockSpec(invf.shape, const),
            pl.BlockSpec(vone.shape, const),
        ],
        out_specs=[
            pl.BlockSpec((1, tm, QK_PAD), lambda bi, i: (bi, i, 0)),
            pl.BlockSpec((1, tm, QK_PAD), lambda bi, i: (bi, i, 0)),
            pl.BlockSpec((1, tm, QK_PAD), lambda bi, i: (bi, i, 0)),
            pl.BlockSpec((1, tm, CONV_CH), lambda bi, i: (bi, i, 0)),
        ],
        out_shape=[
            jax.ShapeDtypeStruct((b, s, QK_PAD), BF16),
            jax.ShapeDtypeStruct((b, s, QK_PAD), BF16),
            jax.ShapeDtypeStruct((b, s, QK_PAD), BF16),
            jax.ShapeDtypeStruct((b, s, CONV_CH), F32),
        ],
        compiler_params=_cparams(("parallel", "parallel")),
        name="front",
    )(x, posf, w_in_ext, qg, wq_ext, kvg, wkv_ext, invf, vone)


def _attn_kernel(q_ref, k_ref, v_ref, o_ref):
    for h in range(MLA_HEADS):
        lo, hi = h * HEAD_PAD, (h + 1) * HEAD_PAD
        sc = _dot_nt(q_ref[0, :, lo:hi], k_ref[0, :, lo:hi])
        p = jnp.exp2(sc - jnp.max(sc, axis=-1, keepdims=True))
        pv = _dot(p.astype(BF16), v_ref[0, :, lo:hi])
        o = pv[:, :MLA_V_DIM] / pv[:, MLA_V_DIM:MLA_V_DIM + 1]
        o_ref[0, :, h * MLA_V_DIM:(h + 1) * MLA_V_DIM] = o.astype(BF16)


def _attention(q, k, v, tq):
    b, s, _ = q.shape
    return pl.pallas_call(
        _attn_kernel,
        grid=(b, s // tq),
        in_specs=[
            pl.BlockSpec((1, tq, QK_PAD), lambda bi, i: (bi, i, 0)),
            pl.BlockSpec((1, s, QK_PAD), lambda bi, i: (bi, 0, 0)),
            pl.BlockSpec((1, s, QK_PAD), lambda bi, i: (bi, 0, 0)),
        ],
        out_specs=pl.BlockSpec((1, tq, MLA_WIDTH), lambda bi, i: (bi, i, 0)),
        out_shape=jax.ShapeDtypeStruct((b, s, MLA_WIDTH), BF16),
        compiler_params=_cparams(("parallel", "parallel")),
        name="attn",
    )(q, k, v)


def _mem_kv_kernel(mem_ref, wk_ref, wv_ref, k_out, v_out):
    m = mem_ref[0].astype(BF16)
    k_out[0] = _dot(m, wk_ref[...]).astype(BF16)
    v_out[0] = _dot(m, wv_ref[...]).astype(BF16)


def _mem_kv(mem, wk, wv):
    b, m, d = mem.shape
    const = lambda bi: (0, 0)
    return pl.pallas_call(
        _mem_kv_kernel,
        grid=(b,),
        in_specs=[
            pl.BlockSpec((1, m, d), lambda bi: (bi, 0, 0)),
            pl.BlockSpec(wk.shape, const),
            pl.BlockSpec(wv.shape, const),
        ],
        out_specs=[pl.BlockSpec((1, m, d), lambda bi: (bi, 0, 0))] * 2,
        out_shape=[jax.ShapeDtypeStruct((b, m, d), BF16)] * 2,
        compiler_params=_cparams(("parallel",)),
        name="mem_kv",
    )(mem, wk, wv)


CONV_SUB = 64


def _mid_kernel(ucur_ref, uprev_ref, unext_ref, attn_ref, x_ref, cw_ref, cb_ref, cg_ref, cbeta_ref, wo_ref,
                g1_ref, b1_ref, wq_ref, kx_ref, vx_ref, xwo_ref, g2_ref, b2_ref, wr_ref,
                h2_out, h2b_out, aff_out, win_ref, shift_ref, u_scr, o_scr):
    tm = ucur_ref.shape[1]
    i = pl.program_id(1)
    last = pl.num_programs(1) - 1
    win_ref[0:HALO, :] = jnp.where(i > 0, uprev_ref[0], 0.0)
    win_ref[HALO:HALO + tm, :] = ucur_ref[0]
    win_ref[HALO + tm:2 * HALO + tm, :] = jnp.where(i < last, unext_ref[0], 0.0)
    span = tm + 2 * HALO - SUBLANES
    shift_ref[0] = win_ref[...]
    for j in range(1, SUBLANES):
        shift_ref[j, 0:span, :] = win_ref[j:j + span, :]

    for r in range(tm // CONV_SUB):
        acc = jnp.broadcast_to(cb_ref[...], (CONV_SUB, CONV_CH))
        for t in range(CONV_WIDTH):
            off = HALO - CONV_PAD + t
            row = r * CONV_SUB + (off // SUBLANES) * SUBLANES
            acc = acc + shift_ref[off % SUBLANES, row:row + CONV_SUB, :] * cw_ref[t:t + 1, :]
        y = _layer_norm(acc, cg_ref[...], cbeta_ref[...])
        u_scr[r * CONV_SUB:(r + 1) * CONV_SUB, :] = (y * _sigmoid(y)).astype(BF16)
    mix = _dot(attn_ref[0], wo_ref[0:MLA_WIDTH, :]) + _dot(u_scr[...], wo_ref[MLA_WIDTH:, :])
    h1 = _layer_norm(DEEPNORM_ALPHA * x_ref[0] + mix, g1_ref[...], b1_ref[...])
    q = _dot(h1.astype(BF16), wq_ref[...]).astype(BF16)
    for h in range(MEM_HEADS):
        lo, hi = h * MEM_HEAD_DIM, (h + 1) * MEM_HEAD_DIM
        sc = _dot_nt(q[:, lo:hi], kx_ref[0, :, lo:hi])
        p = jnp.exp(sc - jnp.max(sc, axis=-1, keepdims=True))
        l = jnp.sum(p, axis=-1, keepdims=True)
        o_scr[:, lo:hi] = (_dot(p.astype(BF16), vx_ref[0, :, lo:hi]) / l).astype(BF16)
    xa = _dot(o_scr[...], xwo_ref[...])
    h2 = _layer_norm(DEEPNORM_ALPHA * h1 + xa, g2_ref[...], b2_ref[...])
    h2_out[0] = h2
    h2_hi = h2.astype(BF16)
    h2b_out[0] = h2_hi
    h2_lo = (h2 - h2_hi.astype(F32)).astype(BF16)
    hi_terms = _dot(h2_hi, wr_ref[...])
    logits = hi_terms[:, :LANES] + hi_terms[:, LANES:] + _dot(h2_lo, wr_ref[:, :LANES])
    lt = logits.T[0:aff_out.shape[1], :]
    ex = jnp.exp(lt - jnp.max(lt, axis=0, keepdims=True))
    aff_out[0] = ex / jnp.sum(ex, axis=0, keepdims=True)


def _mid(u_pre, attn, x, conv_w, conv_b, conv_g, conv_beta, w_o, g1, b1, wq, kx, vx, xwo, g2, b2, wr_split, tm):
    b, s, d = x.shape
    m = kx.shape[1]
    nh = tm // HALO
    const = lambda bi, i: (0, 0)
    tile = lambda w: pl.BlockSpec((1, tm, w), lambda bi, i: (bi, i, 0))
    whole = lambda arr: pl.BlockSpec(arr.shape, const)
    return pl.pallas_call(
        _mid_kernel,
        grid=(b, s // tm),
        in_specs=[
            tile(CONV_CH),
            pl.BlockSpec((1, HALO, CONV_CH), lambda bi, i: (bi, jnp.maximum(i * nh - 1, 0), 0)),
            pl.BlockSpec((1, HALO, CONV_CH), lambda bi, i: (bi, jnp.minimum((i + 1) * nh, s // HALO - 1), 0)),
            tile(MLA_WIDTH),
            tile(d),
            whole(conv_w), whole(conv_b), whole(conv_g), whole(conv_beta), whole(w_o), whole(g1), whole(b1),
            whole(wq),
            pl.BlockSpec((1, m, d), lambda bi, i: (bi, 0, 0)),
            pl.BlockSpec((1, m, d), lambda bi, i: (bi, 0, 0)),
            whole(xwo), whole(g2), whole(b2), whole(wr_split),
        ],
        out_specs=[
            tile(d),
            tile(d),
            pl.BlockSpec((1, N_EXPERTS, tm), lambda bi, i: (bi, 0, i)),
        ],
        out_shape=[
            jax.ShapeDtypeStruct((b, s, d), F32),
            jax.ShapeDtypeStruct((b, s, d), BF16),
            jax.ShapeDtypeStruct((b, N_EXPERTS, s), F32),
        ],
        scratch_shapes=[
            pltpu.VMEM((tm + 2 * HALO, CONV_CH), F32),
            pltpu.VMEM((SUBLANES, tm + 2 * HALO, CONV_CH), F32),
            pltpu.VMEM((tm, CONV_CH), BF16),
            pltpu.VMEM((tm, d), BF16),
        ],
        compiler_params=_cparams(("parallel", "parallel")),
        name="mid",
    )(u_pre, u_pre, u_pre, attn, x, conv_w, conv_b, conv_g, conv_beta, w_o, g1, b1, wq, kx, vx, xwo, g2, b2, wr_split)


def _split_router(w_router):
    hi = w_router.astype(BF16)
    lo = (w_router - hi.astype(F32)).astype(BF16)
    pad = jnp.zeros((w_router.shape[0], LANES - w_router.shape[1]), BF16)
    return jnp.concatenate([hi, pad, lo, pad], axis=1)


BISECT_STEPS_PER_CHECK = 4
BISECT_MAX_CHECKS = 320


def _topk_kernel(aff_ref, slot_out, slot_t_out, off_out, *, cap):
    aff = aff_ref[...]
    rows, s = aff.shape
    capf = jnp.float32(cap)

    def not_done(carry):
        return jnp.logical_and(carry[2] > 0, carry[3] < BISECT_MAX_CHECKS)

    def halve(_, bounds):
        lo, hi = bounds
        mid = 0.5 * (lo + hi)
        take = jnp.sum(jnp.where(aff >= mid, 1.0, 0.0), axis=1, keepdims=True) >= capf
        return jnp.where(take, mid, lo), jnp.where(take, hi, mid)

    def bisect(carry):
        lo, hi = lax.fori_loop(0, BISECT_STEPS_PER_CHECK, halve, carry[:2])
        smallest_in = jnp.min(jnp.where(aff >= lo, aff, jnp.inf), axis=1, keepdims=True)
        largest_in = jnp.max(jnp.where(aff < hi, aff, -jnp.inf), axis=1, keepdims=True)
        open_rows = jnp.sum(jnp.where(smallest_in == largest_in, 0.0, 1.0))
        return lo, hi, open_rows.astype(jnp.int32), carry[3] + 1

    _, hi, _, _ = lax.while_loop(not_done, bisect, (jnp.zeros((rows, 1), F32), jnp.full((rows, 1), 2.0, F32),
                                                    jnp.int32(1), jnp.int32(0)))
    th = jnp.max(jnp.where(aff < hi, aff, -jnp.inf), axis=1, keepdims=True)
    gt = aff > th
    eq = aff == th
    n_gt = jnp.sum(jnp.where(gt, 1.0, 0.0), axis=1, keepdims=True)
    tri = jnp.where(lax.broadcasted_iota(jnp.int32, (s, s), 0) < lax.broadcasted_iota(jnp.int32, (s, s), 1),
                    1.0, 0.0).astype(BF16)
    tie_rank = _dot(jnp.where(eq, 1.0, 0.0).astype(BF16), tri)
    sel = jnp.logical_or(gt, jnp.logical_and(eq, tie_rank < (capf - n_gt)))
    pos = _dot(jnp.where(sel, 1.0, 0.0).astype(BF16), tri)
    slot = jnp.where(sel, pos, -1.0)
    slot_out[...] = slot
    slot_t_out[...] = slot.T
    lanes = off_out.shape[1]
    before = jnp.where(lax.broadcasted_iota(jnp.int32, (s, lanes), 0)
                       < ROUTE_CHUNK * lax.broadcasted_iota(jnp.int32, (s, lanes), 1), 1.0, 0.0).astype(BF16)
    off_out[...] = _dot(jnp.where(sel, 1.0, 0.0).astype(BF16), before)


def _topk(aff2d, cap):
    rows, s = aff2d.shape
    return pl.pallas_call(
        functools.partial(_topk_kernel, cap=cap),
        out_shape=[
            jax.ShapeDtypeStruct((rows, s), F32),
            jax.ShapeDtypeStruct((s, rows), F32),
            jax.ShapeDtypeStruct((rows, LANES), F32),
        ],
        compiler_params=pltpu.CompilerParams(vmem_limit_bytes=VMEM_LIMIT),
        name="topk",
    )(aff2d)


def _route_tables(off_tab, n_b, n_e, cap):
    n_j = off_tab.shape[1] - 1
    off = off_tab.astype(jnp.int32)
    base = jnp.minimum((off[:, :n_j] // BF16_ROWS) * BF16_ROWS, cap - ROUTE_WIN)
    over = jnp.any((off[:, 1:] - base > ROUTE_WIN).reshape(n_b, n_e, n_j), axis=1)
    return base, over.astype(jnp.int32)


def _dispatch_kernel(base_ref, over_ref, h2b_ref, slot_ref, aff_ref, xs_out, gate_out, p_scr, gate_scr, *, cap):
    bi = pl.program_id(0)
    jo = pl.program_id(1)
    n_j = pl.num_programs(1) * ROUTE_INNER
    n_e, _, t = slot_ref.shape

    @pl.when(jo == 0)
    def _():
        xs_out[...] = jnp.zeros(xs_out.shape, xs_out.dtype)
        gate_scr[...] = jnp.zeros(gate_scr.shape, gate_scr.dtype)

    def chunk(ji, carry):
        j = jo * ROUTE_INNER + ji
        h = h2b_ref[0, pl.ds(pl.multiple_of(ji * t, t), t), :]
        slot_row = lambda e: slot_ref[e, pl.ds(j, 1), :]
        aff_row = lambda e: aff_ref[e, pl.ds(j, 1), :]

        @pl.when(over_ref[bi * n_j + j] == 0)
        def _():
            w_iota = lax.broadcasted_iota(jnp.int32, (ROUTE_WIN, t), 0).astype(F32)
            bases = []
            for e in range(n_e):
                base = pl.multiple_of(base_ref[(bi * n_e + e) * n_j + j], BF16_ROWS)
                hit = slot_row(e) == w_iota + base.astype(F32)
                p_scr[e * ROUTE_WIN:(e + 1) * ROUTE_WIN, :] = jnp.where(hit, 1.0, 0.0).astype(BF16)
                gate_scr[e, pl.ds(base, ROUTE_WIN), :] += jnp.sum(jnp.where(hit, aff_row(e), 0.0),
                                                                  axis=1, keepdims=True)
                bases.append(base)
            picked = _dot(p_scr[...], h)
            for e in range(n_e):
                rows = pl.ds(bases[e], ROUTE_WIN)
                xs_out[e, 0, rows, :] += picked[e * ROUTE_WIN:(e + 1) * ROUTE_WIN, :].astype(BF16)

        @pl.when(over_ref[bi * n_j + j] != 0)
        def _():
            c_iota = lax.broadcasted_iota(jnp.int32, (cap, t), 0).astype(F32)
            for e in range(n_e):
                hit = slot_row(e) == c_iota
                xs_out[e, 0] += _dot(jnp.where(hit, 1.0, 0.0).astype(BF16), h).astype(BF16)
                gate_scr[e] += jnp.sum(jnp.where(hit, aff_row(e), 0.0), axis=1, keepdims=True)

        return carry

    lax.fori_loop(0, ROUTE_INNER, chunk, 0)

    @pl.when(jo == pl.num_programs(1) - 1)
    def _():
        eye = lax.broadcasted_iota(jnp.int32, (cap, cap), 0) == lax.broadcasted_iota(jnp.int32, (cap, cap), 1)
        for e in range(n_e):
            gate_out[e, 0] = jnp.sum(jnp.where(eye, gate_scr[e], 0.0), axis=0, keepdims=True)


def _dispatch(base_flat, over_flat, h2b, slot3d, aff3d, n_e, cap):
    b, s, d = h2b.shape
    t = ROUTE_CHUNK
    rows = ROUTE_INNER * t
    grid_spec = pltpu.PrefetchScalarGridSpec(
        num_scalar_prefetch=2,
        grid=(b, s // rows),
        in_specs=[
            pl.BlockSpec((1, rows, d), lambda bi, jo, *_: (bi, jo, 0)),
            pl.BlockSpec((n_e, s // t, t), lambda bi, jo, *_: (bi, 0, 0)),
            pl.BlockSpec((n_e, s // t, t), lambda bi, jo, *_: (bi, 0, 0)),
        ],
        out_specs=[
            pl.BlockSpec((n_e, 1, cap, d), lambda bi, jo, *_: (0, bi, 0, 0)),
            pl.BlockSpec((n_e, 1, 1, cap), lambda bi, jo, *_: (0, bi, 0, 0)),
        ],
        scratch_shapes=[pltpu.VMEM((n_e * ROUTE_WIN, t), BF16), pltpu.VMEM((n_e, cap, 1), F32)],
    )
    return pl.pallas_call(
        functools.partial(_dispatch_kernel, cap=cap),
        grid_spec=grid_spec,
        out_shape=[
            jax.ShapeDtypeStruct((n_e, b, cap, d), BF16),
            jax.ShapeDtypeStruct((n_e, b, 1, cap), F32),
        ],
        compiler_params=_cparams(("parallel", "arbitrary")),
        name="dispatch",
    )(base_flat, over_flat, h2b, slot3d, aff3d)


FFN_ROWS = 512
FFN_CHUNK = 512


def _experts_kernel(xs_ref, gate_ref, wg_ref, wu_ref, wd_ref, y_out, wg_s, wu_s, wd_s, hid_s):
    e = pl.program_id(0)
    f = pl.program_id(1)
    n_chunks = wg_s.shape[1]

    def stage():
        slot = e % 2
        wg_s[slot, f] = wg_ref[0].astype(BF16)
        wu_s[slot, f] = wu_ref[0].astype(BF16)
        wd_s[slot, f] = wd_ref[0].astype(BF16)

    @pl.when(e == 0)
    def _():
        stage()
        y_out[0] = jnp.zeros(y_out.shape[1:], y_out.dtype)

    @pl.when(e > 0)
    def _():
        stage()
        slot = (e - 1) % 2
        xs = xs_ref[0]
        for c in range(n_chunks):
            g = _dot(xs, wg_s[slot, c])
            u = _dot(xs, wu_s[slot, c])
            hid_s[:, c * FFN_CHUNK:(c + 1) * FFN_CHUNK] = (g * _sigmoid(g) * u).astype(BF16)
        y = _dot(hid_s[:, 0:FFN_CHUNK], wd_s[slot, 0])
        for c in range(1, n_chunks):
            y = y + _dot(hid_s[:, c * FFN_CHUNK:(c + 1) * FFN_CHUNK], wd_s[slot, c])
        rows = xs.shape[0]
        eye = lax.broadcasted_iota(jnp.int32, (rows, rows), 0) == lax.broadcasted_iota(jnp.int32, (rows, rows), 1)
        gate_col = jnp.sum(jnp.where(eye, gate_ref[0], 0.0), axis=1, keepdims=True)
        y_out[0] = (y * gate_col).astype(BF16)


def _experts(xs, gate, w_gate, w_up, w_down):
    e, n, d = xs.shape
    ff = w_gate.shape[2]
    n_chunks = ff // FFN_CHUNK
    assert n // FFN_ROWS == n_chunks
    prev = lambda ei, fi: (jnp.maximum(ei - 1, 0), fi, 0)
    cur = lambda ei: jnp.minimum(ei, e - 1)
    return pl.pallas_call(
        _experts_kernel,
        grid=(e + 1, n_chunks),
        in_specs=[
            pl.BlockSpec((1, FFN_ROWS, d), prev),
            pl.BlockSpec((1, 1, FFN_ROWS), lambda ei, fi: (jnp.maximum(ei - 1, 0), 0, fi)),
            pl.BlockSpec((1, d, FFN_CHUNK), lambda ei, fi: (cur(ei), 0, fi)),
            pl.BlockSpec((1, d, FFN_CHUNK), lambda ei, fi: (cur(ei), 0, fi)),
            pl.BlockSpec((1, FFN_CHUNK, d), lambda ei, fi: (cur(ei), fi, 0)),
        ],
        out_specs=pl.BlockSpec((1, FFN_ROWS, d), lambda ei, fi: (jnp.where(ei == 0, e, ei - 1), fi, 0)),
        out_shape=jax.ShapeDtypeStruct((e + 1, n, d), BF16),
        scratch_shapes=[
            pltpu.VMEM((2, n_chunks, d, FFN_CHUNK), BF16),
            pltpu.VMEM((2, n_chunks, d, FFN_CHUNK), BF16),
            pltpu.VMEM((2, n_chunks, FFN_CHUNK, d), BF16),
            pltpu.VMEM((FFN_ROWS, ff), BF16),
        ],
        compiler_params=_cparams(("arbitrary", "arbitrary")),
        name="experts",
    )(xs, gate, w_gate, w_up, w_down)


def _combine_kernel(base_ref, over_ref, slot_t_ref, base_t_ref, y_ref, h2_ref, g3_ref, b3_ref, out_ref, ywin_scr,
                    *, cap):
    bi = pl.program_id(0)
    jo = pl.program_id(1)
    n_j = pl.num_programs(1) * ROUTE_INNER
    n_e = y_ref.shape[0]
    rows = slot_t_ref.shape[1]
    t = ROUTE_CHUNK

    def spread_cols(per_expert):
        width = n_e * per_expert
        col = lax.broadcasted_iota(jnp.int32, (rows, width), 1) // per_expert
        return jnp.where(lax.broadcasted_iota(jnp.int32, (rows, width), 0) == bi * n_e + col, 1.0, 0.0).astype(BF16)

    def lane_in_group(per_expert):
        return (lax.broadcasted_iota(jnp.int32, (1, n_e * per_expert), 1) % per_expert).astype(F32)

    win_spread = spread_cols(ROUTE_WIN)
    win_lane = lane_in_group(ROUTE_WIN)

    def chunk(ji, carry):
        j = jo * ROUTE_INNER + ji
        tok = pl.ds(pl.multiple_of(ji * t, t), t)

        def finish(ff):
            out_ref[0, tok, :] = _layer_norm(DEEPNORM_ALPHA * h2_ref[0, tok, :] + ff, g3_ref[...], b3_ref[...])

        slot_bf = slot_t_ref[tok, :].astype(BF16)

        @pl.when(over_ref[bi * n_j + j] == 0)
        def _():
            slot_wide = _dot(slot_bf, win_spread)
            base_wide = _dot(jnp.broadcast_to(base_t_ref[ji], (SUBLANES, rows)).astype(BF16), win_spread)[0:1, :]
            onehot = jnp.where(slot_wide - base_wide == win_lane, 1.0, 0.0).astype(BF16)
            for e in range(n_e):
                base = pl.multiple_of(base_ref[(bi * n_e + e) * n_j + j], BF16_ROWS)
                ywin_scr[e * ROUTE_WIN:(e + 1) * ROUTE_WIN, :] = y_ref[e, 0, pl.ds(base, ROUTE_WIN), :]
            finish(_dot(onehot, ywin_scr[...]))

        @pl.when(over_ref[bi * n_j + j] != 0)
        def _():
            slot_wide = _dot(slot_bf, spread_cols(cap))
            onehot = jnp.where(slot_wide == lane_in_group(cap), 1.0, 0.0).astype(BF16)
            finish(_dot(onehot, y_ref[:, 0].reshape(n_e * cap, y_ref.shape[3])))

        return carry

    lax.fori_loop(0, ROUTE_INNER, chunk, 0)


def _combine(base_flat, over_flat, slot_t, base_t, y4, h2, g3, b3, cap):
    b, s, d = h2.shape
    e = y4.shape[0] - 1
    rows = ROUTE_INNER * ROUTE_CHUNK
    const = lambda bi, jo, *_: (0, 0)
    grid_spec = pltpu.PrefetchScalarGridSpec(
        num_scalar_prefetch=2,
        grid=(b, s // rows),
        in_specs=[
            pl.BlockSpec((rows, slot_t.shape[1]), lambda bi, jo, *_: (jo, 0)),
            pl.BlockSpec((ROUTE_INNER, 1, base_t.shape[2]), lambda bi, jo, *_: (jo, 0, 0)),
            pl.BlockSpec((e, 1, cap, d), lambda bi, jo, *_: (0, bi, 0, 0)),
            pl.BlockSpec((1, rows, d), lambda bi, jo, *_: (bi, jo, 0)),
            pl.BlockSpec(g3.shape, const),
            pl.BlockSpec(b3.shape, const),
        ],
        out_specs=pl.BlockSpec((1, rows, d), lambda bi, jo, *_: (bi, jo, 0)),
        scratch_shapes=[pltpu.VMEM((e * ROUTE_WIN, d), BF16)],
    )
    return pl.pallas_call(
        functools.partial(_combine_kernel, cap=cap),
        grid_spec=grid_spec,
        out_shape=jax.ShapeDtypeStruct((b, s, d), F32),
        compiler_params=_cparams(("parallel", "parallel")),
        name="combine",
    )(base_flat, over_flat, slot_t, base_t, y4, h2, g3, b3)


def _extend_weights(w_in, w_uq, w_uk, w_uv):
    half = MLA_ROPE_DIM // 2
    d = w_in.shape[0]
    w_t = w_in.T.astype(BF16)
    kr = w_t[SPLIT_KV:SPLIT_KR]
    z = lambda n: jnp.zeros((n, d), BF16)
    kr_full = jnp.concatenate([z(MLA_NOPE_DIM), kr, z(HEAD_PAD - MLA_QK_DIM)], axis=0)
    kr_swap = jnp.concatenate([z(MLA_NOPE_DIM), -kr[half:], kr[:half], z(HEAD_PAD - MLA_QK_DIM)], axis=0)
    w_in_ext = jnp.concatenate([w_t[:SPLIT_KV], kr_full, kr_swap, w_t[SPLIT_KR:]], axis=0).T

    wq = w_uq.reshape(MLA_Q_RANK, MLA_HEADS, MLA_QK_DIM)
    zq = lambda n: jnp.zeros((MLA_Q_RANK, MLA_HEADS, n), w_uq.dtype)
    wq_full = jnp.concatenate([wq, zq(HEAD_PAD - MLA_QK_DIM)], axis=2)
    wq_swap = jnp.concatenate([zq(MLA_NOPE_DIM), -wq[:, :, MLA_NOPE_DIM + half:],
                               wq[:, :, MLA_NOPE_DIM:MLA_NOPE_DIM + half], zq(HEAD_PAD - MLA_QK_DIM)], axis=2)
    wq_ext = jnp.concatenate([wq_full.reshape(MLA_Q_RANK, QK_PAD), wq_swap.reshape(MLA_Q_RANK, QK_PAD)], axis=1)

    wk = w_uk.reshape(MLA_KV_RANK, MLA_HEADS, MLA_NOPE_DIM)
    wk_full = jnp.concatenate([wk, jnp.zeros((MLA_KV_RANK, MLA_HEADS, HEAD_PAD - MLA_NOPE_DIM), w_uk.dtype)], axis=2)
    wv = w_uv.reshape(MLA_KV_RANK, MLA_HEADS, MLA_V_DIM)
    wv_full = jnp.concatenate([wv, jnp.zeros((MLA_KV_RANK, MLA_HEADS, HEAD_PAD - MLA_V_DIM), w_uv.dtype)], axis=2)
    wkv_ext = jnp.concatenate([wk_full.reshape(MLA_KV_RANK, QK_PAD), wv_full.reshape(MLA_KV_RANK, QK_PAD)], axis=1)
    return w_in_ext.astype(BF16), wq_ext.astype(BF16), wkv_ext.astype(BF16)


def _rope_freq_column():
    inv_freq = ROPE_BASE ** (-jnp.arange(0, MLA_ROPE_DIM, 2, dtype=F32) / MLA_ROPE_DIM)
    return inv_freq[:, None]


def _value_one_lanes():
    one_hot = (jnp.arange(HEAD_PAD) == MLA_V_DIM).astype(F32)
    return jnp.tile(one_hot, MLA_HEADS)[None, :]


def kernel(x, mem, positions, w_in, q_norm_g, w_uq, kv_norm_g, w_uk, w_uv, conv_w, conv_b, conv_ln_g, conv_ln_b,
           w_o, ln1_g, ln1_b, xa_w_q, xa_w_k, xa_w_v, xa_w_o, ln2_g, ln2_b, w_router, w_gate, w_up, w_down,
           ln3_g, ln3_b):
    assert w_in.shape[0] == DEPTH == 1
    b, s, d = x.shape
    cap = CAPACITY_FACTOR * s // N_EXPERTS
    assert s % max(FRONT_ROWS, ATTN_ROWS, MID_ROWS, ROUTE_INNER * ROUTE_CHUNK) == 0 and cap >= ROUTE_WIN
    assert mem.shape == (b, MEM_LEN, d) and d == D_MODEL

    w_in_ext, wq_ext, wkv_ext = _extend_weights(w_in[0], w_uq[0], w_uk[0], w_uv[0])
    posf = positions.astype(F32)[:, None, :]
    q, k, v, u_pre = _front(x, posf, w_in_ext, q_norm_g, wq_ext, kv_norm_g, wkv_ext, _rope_freq_column(),
                             _value_one_lanes(), FRONT_ROWS)
    attn = _attention(q, k, v, ATTN_ROWS)
    kx, vx = _mem_kv(mem, xa_w_k[0].astype(BF16), xa_w_v[0].astype(BF16))
    h2, h2b, aff = _mid(u_pre, attn, x, conv_w[0], conv_b, conv_ln_g, conv_ln_b, w_o[0].astype(BF16), ln1_g, ln1_b,
                        (xa_w_q[0] * MEM_Q_SCALE).astype(BF16), kx, vx, xa_w_o[0].astype(BF16), ln2_g, ln2_b,
                        _split_router(w_router[0]), MID_ROWS)

    aff2d = aff.reshape(b * N_EXPERTS, s)
    slot, slot_t, off_tab = _topk(aff2d, cap)
    n_j = s // ROUTE_CHUNK
    base, over = _route_tables(off_tab[:, :n_j + 1], b, N_EXPERTS, cap)
    base_flat, over_flat = base.reshape(-1), over.reshape(-1)
    by_chunk = lambda a: a.reshape(b * N_EXPERTS, n_j, ROUTE_CHUNK)
    xs, gate = _dispatch(base_flat, over_flat, h2b, by_chunk(slot), by_chunk(aff2d), N_EXPERTS, cap)
    y = _experts(xs.reshape(N_EXPERTS, b * cap, d), gate.reshape(N_EXPERTS, 1, b * cap),
                 w_gate[0], w_up[0], w_down[0])
    base_t = base.T.astype(F32).reshape(n_j, 1, b * N_EXPERTS)
    return _combine(base_flat, over_flat, slot_t, base_t, y.reshape(N_EXPERTS + 1, b, cap, d), h2, ln3_g, ln3_b, cap)
```

```python
import functools
import math

import jax
import jax.numpy as jnp
from jax import lax
from jax.experimental import pallas as pl
from jax.experimental.pallas import tpu as pltpu

F32 = jnp.float32
BF16 = jnp.bfloat16

D_MODEL = 1024
MLA_HEADS = 8
MLA_NOPE_DIM = 64
MLA_ROPE_DIM = 32
MLA_QK_DIM = MLA_NOPE_DIM + MLA_ROPE_DIM
MLA_V_DIM = 64
MLA_Q_RANK = 256
MLA_KV_RANK = 128
MLA_WIDTH = MLA_HEADS * MLA_V_DIM
CONV_CH = D_MODEL - MLA_WIDTH
CONV_WIDTH = 31
CONV_PAD = (CONV_WIDTH - 1) // 2
ROPE_BASE = 10000.0
MEM_HEADS = 4
MEM_HEAD_DIM = D_MODEL // MEM_HEADS
MEM_Q_SCALE = 1.0 / math.sqrt(MEM_HEAD_DIM)
assert math.frexp(MEM_Q_SCALE)[0] == 0.5
N_EXPERTS = 16
EXPERT_FF = 2048
CAPACITY_FACTOR = 2
NORM_EPS = 1e-5
DEPTH = 1
DEEPNORM_ALPHA = (2.0 * DEPTH) ** 0.25
SPLIT_Q = MLA_Q_RANK
SPLIT_KV = SPLIT_Q + MLA_KV_RANK
SPLIT_KR = SPLIT_KV + MLA_ROPE_DIM

HEAD_PAD = 128
QK_PAD = MLA_HEADS * HEAD_PAD
OFF_CQ = 0
OFF_CKV = OFF_CQ + MLA_Q_RANK
OFF_KR = OFF_CKV + MLA_KV_RANK
OFF_A = OFF_KR + HEAD_PAD
OFF_G = OFF_A + CONV_CH
IN_EXT = OFF_G + CONV_CH

SUBLANES = 8
LANES = 128
BF16_ROWS = 16
ROUTE_CHUNK = 256
ROUTE_WIN = 64
ROUTE_INNER = 4
HALO = 16
VMEM_LIMIT = 56 * 1024 * 1024
FRONT_ROWS = 1024
ATTN_ROWS = 1024
MID_ROWS = 512
MEM_LEN = 256


def _cparams(sem):
    return pltpu.CompilerParams(dimension_semantics=sem, vmem_limit_bytes=VMEM_LIMIT)


def _layer_norm(v, g, b):
    mu = jnp.mean(v, axis=-1, keepdims=True)
    d = v - mu
    var = jnp.mean(d * d, axis=-1, keepdims=True)
    return d * lax.rsqrt(var + NORM_EPS) * g + b


def _rms_norm(v, g):
    return v * lax.rsqrt(jnp.mean(v * v, axis=-1, keepdims=True) + NORM_EPS) * g


def _sigmoid(v):
    return 1.0 / (1.0 + jnp.exp(-v))


def _dot(a, b):
    return jnp.dot(a, b, preferred_element_type=F32)


def _dot_nt(a, b, precision=None):
    return lax.dot_general(a, b, (((1,), (1,)), ((), ())), preferred_element_type=F32, precision=precision)


def _front_kernel(x_ref, pos_ref, win_ref, qg_ref, wq_ref, kvg_ref, wkv_ref, invf_ref, vone_ref,
                  q_out, k_out, v_out, u_out):
    x = x_ref[0].astype(BF16)
    hc = _dot(x, win_ref[...])
    cqn = _rms_norm(hc[:, OFF_CQ:OFF_CKV], qg_ref[...])
    qq = _dot(cqn.astype(BF16), wq_ref[...])
    ckvn = _rms_norm(hc[:, OFF_CKV:OFF_KR], kvg_ref[...])
    kv = _dot(ckvn.astype(BF16), wkv_ref[...])
    tm = x_ref.shape[1]
    ang = invf_ref[...] * pos_ref[0]
    cos_h, sin_h = jnp.cos(ang), jnp.sin(ang)
    ones = lambda n: jnp.ones((n, tm), F32)
    zeros = lambda n: jnp.zeros((n, tm), F32)
    cos = jnp.concatenate([ones(MLA_NOPE_DIM), cos_h, cos_h, ones(HEAD_PAD - MLA_QK_DIM)], axis=0).T
    sin = jnp.concatenate([zeros(MLA_NOPE_DIM), sin_h, sin_h, zeros(HEAD_PAD - MLA_QK_DIM)], axis=0).T
    kr = hc[:, OFF_KR:OFF_A]
    kr_partner = pltpu.roll(kr, HEAD_PAD - MLA_ROPE_DIM, axis=1)
    rope_lanes = lax.broadcasted_iota(jnp.int32, (1, HEAD_PAD), 1) < MLA_QK_DIM
    krot = jnp.where(rope_lanes, kr * cos + kr_partner * sin, 0.0)
    scale = math.log2(math.e) / math.sqrt(MLA_QK_DIM)
    for h in range(MLA_HEADS):
        lo, hi = h * HEAD_PAD, (h + 1) * HEAD_PAD
        qh = (qq[:, lo:hi] * cos + qq[:, QK_PAD + lo:QK_PAD + hi] * sin) * scale
        q_out[0, :, lo:hi] = qh.astype(BF16)
        k_out[0, :, lo:hi] = (kv[:, lo:hi] + krot).astype(BF16)
    v_out[0] = (kv[:, QK_PAD:] + vone_ref[...]).astype(BF16)
    u_out[0] = hc[:, OFF_A:OFF_G] * _sigmoid(hc[:, OFF_G:IN_EXT])


def _front(x, posf, w_in_ext, qg, wq_ext, kvg, wkv_ext, invf, vone, tm):
    b, s, d = x.shape
    const = lambda bi, i: (0, 0)
    return pl.pallas_call(
        _front_kernel,
        grid=(b, s // tm),
        in_specs=[
            pl.BlockSpec((1, tm, d), lambda bi, i: (bi, i, 0)),
            pl.BlockSpec((1, 1, tm), lambda bi, i: (bi, 0, i)),
            pl.BlockSpec(w_in_ext.shape, const),
            pl.BlockSpec(qg.shape, const),
            pl.BlockSpec(wq_ext.shape, const),
            pl.BlockSpec(kvg.shape, const),
            pl.BlockSpec(wkv_ext.shape, const),
            pl.BlockSpec(invf.shape, const),
            pl.BlockSpec(vone.shape, const),
        ],
        out_specs=[
            pl.BlockSpec((1, tm, QK_PAD), lambda bi, i: (bi, i, 0)),
            pl.BlockSpec((1, tm, QK_PAD), lambda bi, i: (bi, i, 0)),
            pl.BlockSpec((1, tm, QK_PAD), lambda bi, i: (bi, i, 0)),
            pl.BlockSpec((1, tm, CONV_CH), lambda bi, i: (bi, i, 0)),
        ],
        out_shape=[
            jax.ShapeDtypeStruct((b, s, QK_PAD), BF16),
            jax.ShapeDtypeStruct((b, s, QK_PAD), BF16),
            jax.ShapeDtypeStruct((b, s, QK_PAD), BF16),
            jax.ShapeDtypeStruct((b, s, CONV_CH), F32),
        ],
        compiler_params=_cparams(("parallel", "parallel")),
        name="front",
    )(x, posf, w_in_ext, qg, wq_ext, kvg, wkv_ext, invf, vone)


def _attn_kernel(q_ref, k_ref, v_ref, o_ref):
    for h in range(MLA_HEADS):
        lo, hi = h * HEAD_PAD, (h + 1) * HEAD_PAD
        sc = _dot_nt(q_ref[0, :, lo:hi], k_ref[0, :, lo:hi])
        p = jnp.exp2(sc - jnp.max(sc, axis=-1, keepdims=True))
        pv = _dot(p.astype(BF16), v_ref[0, :, lo:hi])
        o = pv[:, :MLA_V_DIM] / pv[:, MLA_V_DIM:MLA_V_DIM + 1]
        o_ref[0, :, h * MLA_V_DIM:(h + 1) * MLA_V_DIM] = o.astype(BF16)


def _attention(q, k, v, tq):
    b, s, _ = q.shape
    return pl.pallas_call(
        _attn_kernel,
        grid=(b, s // tq),
        in_specs=[
            pl.BlockSpec((1, tq, QK_PAD), lambda bi, i: (bi, i, 0)),
            pl.BlockSpec((1, s, QK_PAD), lambda bi, i: (bi, 0, 0)),
            pl.BlockSpec((1, s, QK_PAD), lambda bi, i: (bi, 0, 0)),
        ],
        out_specs=pl.BlockSpec((1, tq, MLA_WIDTH), lambda bi, i: (bi, i, 0)),
        out_shape=jax.ShapeDtypeStruct((b, s, MLA_WIDTH), BF16),
        compiler_params=_cparams(("parallel", "parallel")),
        name="attn",
    )(q, k, v)


def _mem_kv_kernel(mem_ref, wk_ref, wv_ref, k_out, v_out):
    m = mem_ref[0].astype(BF16)
    k_out[0] = _dot(m, wk_ref[...]).astype(BF16)
    v_out[0] = _dot(m, wv_ref[...]).astype(BF16)


def _mem_kv(mem, wk, wv):
    b, m, d = mem.shape
    const = lambda bi: (0, 0)
    return pl.pallas_call(
        _mem_kv_kernel,
        grid=(b,),
        in_specs=[
            pl.BlockSpec((1, m, d), lambda bi: (bi, 0, 0)),
            pl.BlockSpec(wk.shape, const),
            pl.BlockSpec(wv.shape, const),
        ],
        out_specs=[pl.BlockSpec((1, m, d), lambda bi: (bi, 0, 0))] * 2,
        out_shape=[jax.ShapeDtypeStruct((b, m, d), BF16)] * 2,
        compiler_params=_cparams(("parallel",)),
        name="mem_kv",
    )(mem, wk, wv)


CONV_SUB = 64


def _mid_kernel(ucur_ref, uprev_ref, unext_ref, attn_ref, x_ref, cw_ref, cb_ref, cg_ref, cbeta_ref, wo_ref,
                g1_ref, b1_ref, wq_ref, kx_ref, vx_ref, xwo_ref, g2_ref, b2_ref, wr_ref,
                h2_out, h2b_out, aff_out, win_ref, shift_ref, u_scr, o_scr):
    tm = ucur_ref.shape[1]
    i = pl.program_id(1)
    last = pl.num_programs(1) - 1
    win_ref[0:HALO, :] = jnp.where(i > 0, uprev_ref[0], 0.0)
    win_ref[HALO:HALO + tm, :] = ucur_ref[0]
    win_ref[HALO + tm:2 * HALO + tm, :] = jnp.where(i < last, unext_ref[0], 0.0)
    span = tm + 2 * HALO - SUBLANES
    shift_ref[0] = win_ref[...]
    for j in range(1, SUBLANES):
        shift_ref[j, 0:span, :] = win_ref[j:j + span, :]

    for r in range(tm // CONV_SUB):
        acc = jnp.broadcast_to(cb_ref[...], (CONV_SUB, CONV_CH))
        for t in range(CONV_WIDTH):
            off = HALO - CONV_PAD + t
            row = r * CONV_SUB + (off // SUBLANES) * SUBLANES
            acc = acc + shift_ref[off % SUBLANES, row:row + CONV_SUB, :] * cw_ref[t:t + 1, :]
        y = _layer_norm(acc, cg_ref[...], cbeta_ref[...])
        u_scr[r * CONV_SUB:(r + 1) * CONV_SUB, :] = (y * _sigmoid(y)).astype(BF16)
    mix = _dot(attn_ref[0], wo_ref[0:MLA_WIDTH, :]) + _dot(u_scr[...], wo_ref[MLA_WIDTH:, :])
    h1 = _layer_norm(DEEPNORM_ALPHA * x_ref[0] + mix, g1_ref[...], b1_ref[...])
    q = _dot(h1.astype(BF16), wq_ref[...]).astype(BF16)
    for h in range(MEM_HEADS):
        lo, hi = h * MEM_HEAD_DIM, (h + 1) * MEM_HEAD_DIM
        sc = _dot_nt(q[:, lo:hi], kx_ref[0, :, lo:hi])
        p = jnp.exp(sc - jnp.max(sc, axis=-1, keepdims=True))
        l = jnp.sum(p, axis=-1, keepdims=True)
        o_scr[:, lo:hi] = (_dot(p.astype(BF16), vx_ref[0, :, lo:hi]) / l).astype(BF16)
    xa = _dot(o_scr[...], xwo_ref[...])
    h2 = _layer_norm(DEEPNORM_ALPHA * h1 + xa, g2_ref[...], b2_ref[...])
    h2_out[0] = h2
    h2_hi = h2.astype(BF16)
    h2b_out[0] = h2_hi
    h2_lo = (h2 - h2_hi.astype(F32)).astype(BF16)
    hi_terms = _dot(h2_hi, wr_ref[...])
    logits = hi_terms[:, :LANES] + hi_terms[:, LANES:] + _dot(h2_lo, wr_ref[:, :LANES])
    lt = logits.T[0:aff_out.shape[1], :]
    ex = jnp.exp(lt - jnp.max(lt, axis=0, keepdims=True))
    aff_out[0] = ex / jnp.sum(ex, axis=0, keepdims=True)


def _mid(u_pre, attn, x, conv_w, conv_b, conv_g, conv_beta, w_o, g1, b1, wq, kx, vx, xwo, g2, b2, wr_split, tm):
    b, s, d = x.shape
    m = kx.shape[1]
    nh = tm // HALO
    const = lambda bi, i: (0, 0)
    tile = lambda w: pl.BlockSpec((1, tm, w), lambda bi, i: (bi, i, 0))
    whole = lambda arr: pl.BlockSpec(arr.shape, const)
    return pl.pallas_call(
        _mid_kernel,
        grid=(b, s // tm),
        in_specs=[
            tile(CONV_CH),
            pl.BlockSpec((1, HALO, CONV_CH), lambda bi, i: (bi, jnp.maximum(i * nh - 1, 0), 0)),
            pl.BlockSpec((1, HALO, CONV_CH), lambda bi, i: (bi, jnp.minimum((i + 1) * nh, s // HALO - 1), 0)),
            tile(MLA_WIDTH),
            tile(d),
            whole(conv_w), whole(conv_b), whole(conv_g), whole(conv_beta), whole(w_o), whole(g1), whole(b1),
            whole(wq),
            pl.BlockSpec((1, m, d), lambda bi, i: (bi, 0, 0)),
            pl.BlockSpec((1, m, d), lambda bi, i: (bi, 0, 0)),
            whole(xwo), whole(g2), whole(b2), whole(wr_split),
        ],
        out_specs=[
            tile(d),
            tile(d),
            pl.BlockSpec((1, N_EXPERTS, tm), lambda bi, i: (bi, 0, i)),
        ],
        out_shape=[
            jax.ShapeDtypeStruct((b, s, d), F32),
            jax.ShapeDtypeStruct((b, s, d), BF16),
            jax.ShapeDtypeStruct((b, N_EXPERTS, s), F32),
        ],
        scratch_shapes=[
            pltpu.VMEM((tm + 2 * HALO, CONV_CH), F32),
            pltpu.VMEM((SUBLANES, tm + 2 * HALO, CONV_CH), F32),
            pltpu.VMEM((tm, CONV_CH), BF16),
            pltpu.VMEM((tm, d), BF16),
        ],
        compiler_params=_cparams(("parallel", "parallel")),
        name="mid",
    )(u_pre, u_pre, u_pre, attn, x, conv_w, conv_b, conv_g, conv_beta, w_o, g1, b1, wq, kx, vx, xwo, g2, b2, wr_split)


def _split_router(w_router):
    hi = w_router.astype(BF16)
    lo = (w_router - hi.astype(F32)).astype(BF16)
    pad = jnp.zeros((w_router.shape[0], LANES - w_router.shape[1]), BF16)
    return jnp.concatenate([hi, pad, lo, pad], axis=1)


BISECT_STEPS_PER_CHECK = 4
BISECT_MAX_CHECKS = 320


def _topk_kernel(aff_ref, slot_out, slot_t_out, off_out, *, cap):
    aff = aff_ref[...]
    rows, s = aff.shape
    capf = jnp.float32(cap)

    def not_done(carry):
        return jnp.logical_and(carry[2] > 0, carry[3] < BISECT_MAX_CHECKS)

    def halve(_, bounds):
        lo, hi = bounds
        mid = 0.5 * (lo + hi)
        take = jnp.sum(jnp.where(aff >= mid, 1.0, 0.0), axis=1, keepdims=True) >= capf
        return jnp.where(take, mid, lo), jnp.where(take, hi, mid)

    def bisect(carry):
        lo, hi = lax.fori_loop(0, BISECT_STEPS_PER_CHECK, halve, carry[:2])
        smallest_in = jnp.min(jnp.where(aff >= lo, aff, jnp.inf), axis=1, keepdims=True)
        largest_in = jnp.max(jnp.where(aff < hi, aff, -jnp.inf), axis=1, keepdims=True)
        open_rows = jnp.sum(jnp.where(smallest_in == largest_in, 0.0, 1.0))
        return lo, hi, open_rows.astype(jnp.int32), carry[3] + 1

    _, hi, _, _ = lax.while_loop(not_done, bisect, (jnp.zeros((rows, 1), F32), jnp.full((rows, 1), 2.0, F32),
                                                    jnp.int32(1), jnp.int32(0)))
    th = jnp.max(jnp.where(aff < hi, aff, -jnp.inf), axis=1, keepdims=True)
    gt = aff > th
    eq = aff == th
    n_gt = jnp.sum(jnp.where(gt, 1.0, 0.0), axis=1, keepdims=True)
    tri = jnp.where(lax.broadcasted_iota(jnp.int32, (s, s), 0) < lax.broadcasted_iota(jnp.int32, (s, s), 1),
                    1.0, 0.0).astype(BF16)
    tie_rank = _dot(jnp.where(eq, 1.0, 0.0).astype(BF16), tri)
    sel = jnp.logical_or(gt, jnp.logical_and(eq, tie_rank < (capf - n_gt)))
    pos = _dot(jnp.where(sel, 1.0, 0.0).astype(BF16), tri)
    slot = jnp.where(sel, pos, -1.0)
    slot_out[...] = slot
    slot_t_out[...] = slot.T
    lanes = off_out.shape[1]
    before = jnp.where(lax.broadcasted_iota(jnp.int32, (s, lanes), 0)
                       < ROUTE_CHUNK * lax.broadcasted_iota(jnp.int32, (s, lanes), 1), 1.0, 0.0).astype(BF16)
    off_out[...] = _dot(jnp.where(sel, 1.0, 0.0).astype(BF16), before)


def _topk(aff2d, cap):
    rows, s = aff2d.shape
    return pl.pallas_call(
        functools.partial(_topk_kernel, cap=cap),
        out_shape=[
            jax.ShapeDtypeStruct((rows, s), F32),
            jax.ShapeDtypeStruct((s, rows), F32),
            jax.ShapeDtypeStruct((rows, LANES), F32),
        ],
        compiler_params=pltpu.CompilerParams(vmem_limit_bytes=VMEM_LIMIT),
        name="topk",
    )(aff2d)


def _route_tables(off_tab, n_b, n_e, cap):
    n_j = off_tab.shape[1] - 1
    off = off_tab.astype(jnp.int32)
    base = jnp.minimum((off[:, :n_j] // BF16_ROWS) * BF16_ROWS, cap - ROUTE_WIN)
    over = jnp.any((off[:, 1:] - base > ROUTE_WIN).reshape(n_b, n_e, n_j), axis=1)
    return base, over.astype(jnp.int32)


def _dispatch_kernel(base_ref, over_ref, h2b_ref, slot_ref, aff_ref, xs_out, gate_out, p_scr, gate_scr, *, cap):
    bi = pl.program_id(0)
    jo = pl.program_id(1)
    n_j = pl.num_programs(1) * ROUTE_INNER
    n_e, _, t = slot_ref.shape

    @pl.when(jo == 0)
    def _():
        xs_out[...] = jnp.zeros(xs_out.shape, xs_out.dtype)
        gate_scr[...] = jnp.zeros(gate_scr.shape, gate_scr.dtype)

    def chunk(ji, carry):
        j = jo * ROUTE_INNER + ji
        h = h2b_ref[0, pl.ds(pl.multiple_of(ji * t, t), t), :]
        slot_row = lambda e: slot_ref[e, pl.ds(j, 1), :]
        aff_row = lambda e: aff_ref[e, pl.ds(j, 1), :]

        @pl.when(over_ref[bi * n_j + j] == 0)
        def _():
            w_iota = lax.broadcasted_iota(jnp.int32, (ROUTE_WIN, t), 0).astype(F32)
            bases = []
            for e in range(n_e):
                base = pl.multiple_of(base_ref[(bi * n_e + e) * n_j + j], BF16_ROWS)
                hit = slot_row(e) == w_iota + base.astype(F32)
                p_scr[e * ROUTE_WIN:(e + 1) * ROUTE_WIN, :] = jnp.where(hit, 1.0, 0.0).astype(BF16)
                gate_scr[e, pl.ds(base, ROUTE_WIN), :] += jnp.sum(jnp.where(hit, aff_row(e), 0.0),
                                                                  axis=1, keepdims=True)
                bases.append(base)
            picked = _dot(p_scr[...], h)
            for e in range(n_e):
                rows = pl.ds(bases[e], ROUTE_WIN)
                xs_out[e, 0, rows, :] += picked[e * ROUTE_WIN:(e + 1) * ROUTE_WIN, :].astype(BF16)

        @pl.when(over_ref[bi * n_j + j] != 0)
        def _():
            c_iota = lax.broadcasted_iota(jnp.int32, (cap, t), 0).astype(F32)
            for e in range(n_e):
                hit = slot_row(e) == c_iota
                xs_out[e, 0] += _dot(jnp.where(hit, 1.0, 0.0).astype(BF16), h).astype(BF16)
                gate_scr[e] += jnp.sum(jnp.where(hit, aff_row(e), 0.0), axis=1, keepdims=True)

        return carry

    lax.fori_loop(0, ROUTE_INNER, chunk, 0)

    @pl.when(jo == pl.num_programs(1) - 1)
    def _():
        eye = lax.broadcasted_iota(jnp.int32, (cap, cap), 0) == lax.broadcasted_iota(jnp.int32, (cap, cap), 1)
        for e in range(n_e):
            gate_out[e, 0] = jnp.sum(jnp.where(eye, gate_scr[e], 0.0), axis=0, keepdims=True)


def _dispatch(base_flat, over_flat, h2b, slot3d, aff3d, n_e, cap):
    b, s, d = h2b.shape
    t = ROUTE_CHUNK
    rows = ROUTE_INNER * t
    grid_spec = pltpu.PrefetchScalarGridSpec(
        num_scalar_prefetch=2,
        grid=(b, s // rows),
        in_specs=[
            pl.BlockSpec((1, rows, d), lambda bi, jo, *_: (bi, jo, 0)),
            pl.BlockSpec((n_e, s // t, t), lambda bi, jo, *_: (bi, 0, 0)),
            pl.BlockSpec((n_e, s // t, t), lambda bi, jo, *_: (bi, 0, 0)),
        ],
        out_specs=[
            pl.BlockSpec((n_e, 1, cap, d), lambda bi, jo, *_: (0, bi, 0, 0)),
            pl.BlockSpec((n_e, 1, 1, cap), lambda bi, jo, *_: (0, bi, 0, 0)),
        ],
        scratch_shapes=[pltpu.VMEM((n_e * ROUTE_WIN, t), BF16), pltpu.VMEM((n_e, cap, 1), F32)],
    )
    return pl.pallas_call(
        functools.partial(_dispatch_kernel, cap=cap),
        grid_spec=grid_spec,
        out_shape=[
            jax.ShapeDtypeStruct((n_e, b, cap, d), BF16),
            jax.ShapeDtypeStruct((n_e, b, 1, cap), F32),
        ],
        compiler_params=_cparams(("parallel", "arbitrary")),
        name="dispatch",
    )(base_flat, over_flat, h2b, slot3d, aff3d)


FFN_ROWS = 512
FFN_CHUNK = 512


def _experts_kernel(xs_ref, gate_ref, wg_ref, wu_ref, wd_ref, y_out, wg_s, wu_s, wd_s, hid_s):
    e = pl.program_id(0)
    f = pl.program_id(1)
    n_chunks = wg_s.shape[1]

    def stage():
        slot = e % 2
        wg_s[slot, f] = wg_ref[0].astype(BF16)
        wu_s[slot, f] = wu_ref[0].astype(BF16)
        wd_s[slot, f] = wd_ref[0].astype(BF16)

    @pl.when(e == 0)
    def _():
        stage()
        y_out[0] = jnp.zeros(y_out.shape[1:], y_out.dtype)

    @pl.when(e > 0)
    def _():
        stage()
        slot = (e - 1) % 2
        xs = xs_ref[0]
        for c in range(n_chunks):
            g = _dot(xs, wg_s[slot, c])
            u = _dot(xs, wu_s[slot, c])
            hid_s[:, c * FFN_CHUNK:(c + 1) * FFN_CHUNK] = (g * _sigmoid(g) * u).astype(BF16)
        y = _dot(hid_s[:, 0:FFN_CHUNK], wd_s[slot, 0])
        for c in range(1, n_chunks):
            y = y + _dot(hid_s[:, c * FFN_CHUNK:(c + 1) * FFN_CHUNK], wd_s[slot, c])
        rows = xs.shape[0]
        eye = lax.broadcasted_iota(jnp.int32, (rows, rows), 0) == lax.broadcasted_iota(jnp.int32, (rows, rows), 1)
        gate_col = jnp.sum(jnp.where(eye, gate_ref[0], 0.0), axis=1, keepdims=True)
        y_out[0] = (y * gate_col).astype(BF16)


def _experts(xs, gate, w_gate, w_up, w_down):
    e, n, d = xs.shape
    ff = w_gate.shape[2]
    n_chunks = ff // FFN_CHUNK
    assert n // FFN_ROWS == n_chunks
    prev = lambda ei, fi: (jnp.maximum(ei - 1, 0), fi, 0)
    cur = lambda ei: jnp.minimum(ei, e - 1)
    return pl.pallas_call(
        _experts_kernel,
        grid=(e + 1, n_chunks),
        in_specs=[
            pl.BlockSpec((1, FFN_ROWS, d), prev),
            pl.BlockSpec((1, 1, FFN_ROWS), lambda ei, fi: (jnp.maximum(ei - 1, 0), 0, fi)),
            pl.BlockSpec((1, d, FFN_CHUNK), lambda ei, fi: (cur(ei), 0, fi)),
            pl.BlockSpec((1, d, FFN_CHUNK), lambda ei, fi: (cur(ei), 0, fi)),
            pl.BlockSpec((1, FFN_CHUNK, d), lambda ei, fi: (cur(ei), fi, 0)),
        ],
        out_specs=pl.BlockSpec((1, FFN_ROWS, d), lambda ei, fi: (jnp.where(ei == 0, e, ei - 1), fi, 0)),
        out_shape=jax.ShapeDtypeStruct((e + 1, n, d), BF16),
        scratch_shapes=[
            pltpu.VMEM((2, n_chunks, d, FFN_CHUNK), BF16),
            pltpu.VMEM((2, n_chunks, d, FFN_CHUNK), BF16),
            pltpu.VMEM((2, n_chunks, FFN_CHUNK, d), BF16),
            pltpu.VMEM((FFN_ROWS, ff), BF16),
        ],
        compiler_params=_cparams(("arbitrary", "arbitrary")),
        name="experts",
    )(xs, gate, w_gate, w_up, w_down)


def _combine_kernel(base_ref, over_ref, slot_t_ref, base_t_ref, y_ref, h2_ref, g3_ref, b3_ref, out_ref, ywin_scr,
                    *, cap):
    bi = pl.program_id(0)
    jo = pl.program_id(1)
    n_j = pl.num_programs(1) * ROUTE_INNER
    n_e = y_ref.shape[0]
    rows = slot_t_ref.shape[1]
    t = ROUTE_CHUNK

    def spread_cols(per_expert):
        width = n_e * per_expert
        col = lax.broadcasted_iota(jnp.int32, (rows, width), 1) // per_expert
        return jnp.where(lax.broadcasted_iota(jnp.int32, (rows, width), 0) == bi * n_e + col, 1.0, 0.0).astype(BF16)

    def lane_in_group(per_expert):
        return (lax.broadcasted_iota(jnp.int32, (1, n_e * per_expert), 1) % per_expert).astype(F32)

    win_spread = spread_cols(ROUTE_WIN)
    win_lane = lane_in_group(ROUTE_WIN)

    def chunk(ji, carry):
        j = jo * ROUTE_INNER + ji
        tok = pl.ds(pl.multiple_of(ji * t, t), t)

        def finish(ff):
            out_ref[0, tok, :] = _layer_norm(DEEPNORM_ALPHA * h2_ref[0, tok, :] + ff, g3_ref[...], b3_ref[...])

        slot_bf = slot_t_ref[tok, :].astype(BF16)

        @pl.when(over_ref[bi * n_j + j] == 0)
        def _():
            slot_wide = _dot(slot_bf, win_spread)
            base_wide = _dot(jnp.broadcast_to(base_t_ref[ji], (SUBLANES, rows)).astype(BF16), win_spread)[0:1, :]
            onehot = jnp.where(slot_wide - base_wide == win_lane, 1.0, 0.0).astype(BF16)
            for e in range(n_e):
                base = pl.multiple_of(base_ref[(bi * n_e + e) * n_j + j], BF16_ROWS)
                ywin_scr[e * ROUTE_WIN:(e + 1) * ROUTE_WIN, :] = y_ref[e, 0, pl.ds(base, ROUTE_WIN), :]
            finish(_dot(onehot, ywin_scr[...]))

        @pl.when(over_ref[bi * n_j + j] != 0)
        def _():
            slot_wide = _dot(slot_bf, spread_cols(cap))
            onehot = jnp.where(slot_wide == lane_in_group(cap), 1.0, 0.0).astype(BF16)
            finish(_dot(onehot, y_ref[:, 0].reshape(n_e * cap, y_ref.shape[3])))

        return carry

    lax.fori_loop(0, ROUTE_INNER, chunk, 0)


def _combine(base_flat, over_flat, slot_t, base_t, y4, h2, g3, b3, cap):
    b, s, d = h2.shape
    e = y4.shape[0] - 1
    rows = ROUTE_INNER * ROUTE_CHUNK
    const = lambda bi, jo, *_: (0, 0)
    grid_spec = pltpu.PrefetchScalarGridSpec(
        num_scalar_prefetch=2,
        grid=(b, s // rows),
        in_specs=[
            pl.BlockSpec((rows, slot_t.shape[1]), lambda bi, jo, *_: (jo, 0)),
            pl.BlockSpec((ROUTE_INNER, 1, base_t.shape[2]), lambda bi, jo, *_: (jo, 0, 0)),
            pl.BlockSpec((e, 1, cap, d), lambda bi, jo, *_: (0, bi, 0, 0)),
            pl.BlockSpec((1, rows, d), lambda bi, jo, *_: (bi, jo, 0)),
            pl.BlockSpec(g3.shape, const),
            pl.BlockSpec(b3.shape, const),
        ],
        out_specs=pl.BlockSpec((1, rows, d), lambda bi, jo, *_: (bi, jo, 0)),
        scratch_shapes=[pltpu.VMEM((e * ROUTE_WIN, d), BF16)],
    )
    return pl.pallas_call(
        functools.partial(_combine_kernel, cap=cap),
        grid_spec=grid_spec,
        out_shape=jax.ShapeDtypeStruct((b, s, d), F32),
        compiler_params=_cparams(("parallel", "parallel")),
        name="combine",
    )(base_flat, over_flat, slot_t, base_t, y4, h2, g3, b3)


def _extend_weights(w_in, w_uq, w_uk, w_uv):
    half = MLA_ROPE_DIM // 2
    d = w_in.shape[0]
    w_t = w_in.T.astype(BF16)
    kr = w_t[SPLIT_KV:SPLIT_KR]
    z = lambda n: jnp.zeros((n, d), BF16)
    assert HEAD_PAD - MLA_QK_DIM == MLA_ROPE_DIM
    kr_tile = jnp.concatenate([z(MLA_NOPE_DIM), kr, -kr[half:], kr[:half]], axis=0)
    w_in_ext = jnp.concatenate([w_t[:SPLIT_KV], kr_tile, w_t[SPLIT_KR:]], axis=0).T

    wq = w_uq.reshape(MLA_Q_RANK, MLA_HEADS, MLA_QK_DIM)
    zq = lambda n: jnp.zeros((MLA_Q_RANK, MLA_HEADS, n), w_uq.dtype)
    wq_full = jnp.concatenate([wq, zq(HEAD_PAD - MLA_QK_DIM)], axis=2)
    wq_swap = jnp.concatenate([zq(MLA_NOPE_DIM), -wq[:, :, MLA_NOPE_DIM + half:],
                               wq[:, :, MLA_NOPE_DIM:MLA_NOPE_DIM + half], zq(HEAD_PAD - MLA_QK_DIM)], axis=2)
    wq_ext = jnp.concatenate([wq_full.reshape(MLA_Q_RANK, QK_PAD), wq_swap.reshape(MLA_Q_RANK, QK_PAD)], axis=1)

    wk = w_uk.reshape(MLA_KV_RANK, MLA_HEADS, MLA_NOPE_DIM)
    wk_full = jnp.concatenate([wk, jnp.zeros((MLA_KV_RANK, MLA_HEADS, HEAD_PAD - MLA_NOPE_DIM), w_uk.dtype)], axis=2)
    wv = w_uv.reshape(MLA_KV_RANK, MLA_HEADS, MLA_V_DIM)
    wv_full = jnp.concatenate([wv, jnp.zeros((MLA_KV_RANK, MLA_HEADS, HEAD_PAD - MLA_V_DIM), w_uv.dtype)], axis=2)
    wkv_ext = jnp.concatenate([wk_full.reshape(MLA_KV_RANK, QK_PAD), wv_full.reshape(MLA_KV_RANK, QK_PAD)], axis=1)
    return w_in_ext.astype(BF16), wq_ext.astype(BF16), wkv_ext.astype(BF16)


def _rope_freq_column():
    inv_freq = ROPE_BASE ** (-jnp.arange(0, MLA_ROPE_DIM, 2, dtype=F32) / MLA_ROPE_DIM)
    return inv_freq[:, None]


def _value_one_lanes():
    one_hot = (jnp.arange(HEAD_PAD) == MLA_V_DIM).astype(F32)
    return jnp.tile(one_hot, MLA_HEADS)[None, :]


def kernel(x, mem, positions, w_in, q_norm_g, w_uq, kv_norm_g, w_uk, w_uv, conv_w, conv_b, conv_ln_g, conv_ln_b,
           w_o, ln1_g, ln1_b, xa_w_q, xa_w_k, xa_w_v, xa_w_o, ln2_g, ln2_b, w_router, w_gate, w_up, w_down,
           ln3_g, ln3_b):
    assert w_in.shape[0] == DEPTH == 1
    b, s, d = x.shape
    cap = CAPACITY_FACTOR * s // N_EXPERTS
    assert s % max(FRONT_ROWS, ATTN_ROWS, MID_ROWS, ROUTE_INNER * ROUTE_CHUNK) == 0 and cap >= ROUTE_WIN
    assert mem.shape == (b, MEM_LEN, d) and d == D_MODEL

    w_in_ext, wq_ext, wkv_ext = _extend_weights(w_in[0], w_uq[0], w_uk[0], w_uv[0])
    posf = positions.astype(F32)[:, None, :]
    q, k, v, u_pre = _front(x, posf, w_in_ext, q_norm_g, wq_ext, kv_norm_g, wkv_ext, _rope_freq_column(),
                             _value_one_lanes(), FRONT_ROWS)
    attn = _attention(q, k, v, ATTN_ROWS)
    kx, vx = _mem_kv(mem, xa_w_k[0].astype(BF16), xa_w_v[0].astype(BF16))
    h2, h2b, aff = _mid(u_pre, attn, x, conv_w[0], conv_b, conv_ln_g, conv_ln_b, w_o[0].astype(BF16), ln1_g, ln1_b,
                        (xa_w_q[0] * MEM_Q_SCALE).astype(BF16), kx, vx, xa_w_o[0].astype(BF16), ln2_g, ln2_b,
                        _split_router(w_router[0]), MID_ROWS)

    aff2d = aff.reshape(b * N_EXPERTS, s)
    slot, slot_t, off_tab = _topk(aff2d, cap)
    n_j = s // ROUTE_CHUNK
    base, over = _route_tables(off_tab[:, :n_j + 1], b, N_EXPERTS, cap)
    base_flat, over_flat = base.reshape(-1), over.reshape(-1)
    by_chunk = lambda a: a.reshape(b * N_EXPERTS, n_j, ROUTE_CHUNK)
    xs, gate = _dispatch(base_flat, over_flat, h2b, by_chunk(slot), by_chunk(aff2d), N_EXPERTS, cap)
    y = _experts(xs.reshape(N_EXPERTS, b * cap, d), gate.reshape(N_EXPERTS, 1, b * cap),
                 w_gate[0], w_up[0], w_down[0])
    base_t = base.T.astype(F32).reshape(n_j, 1, b * N_EXPERTS)
    return _combine(base_flat, over_flat, slot_t, base_t, y.reshape(N_EXPERTS + 1, b, cap, d), h2, ln3_g, ln3_b, cap)
```

```python
import functools
import math

import jax
import jax.numpy as jnp
from jax import lax
from jax.experimental import pallas as pl
from jax.experimental.pallas import tpu as pltpu

F32 = jnp.float32
BF16 = jnp.bfloat16

D_MODEL = 1024
MLA_HEADS = 8
MLA_NOPE_DIM = 64
MLA_ROPE_DIM = 32
MLA_QK_DIM = MLA_NOPE_DIM + MLA_ROPE_DIM
MLA_V_DIM = 64
MLA_Q_RANK = 256
MLA_KV_RANK = 128
MLA_WIDTH = MLA_HEADS * MLA_V_DIM
CONV_CH = D_MODEL - MLA_WIDTH
CONV_WIDTH = 31
CONV_PAD = (CONV_WIDTH - 1) // 2
ROPE_BASE = 10000.0
MEM_HEADS = 4
MEM_HEAD_DIM = D_MODEL // MEM_HEADS
MEM_Q_SCALE = 1.0 / math.sqrt(MEM_HEAD_DIM)
assert math.frexp(MEM_Q_SCALE)[0] == 0.5
N_EXPERTS = 16
EXPERT_FF = 2048
CAPACITY_FACTOR = 2
NORM_EPS = 1e-5
DEPTH = 1
DEEPNORM_ALPHA = (2.0 * DEPTH) ** 0.25
SPLIT_Q = MLA_Q_RANK
SPLIT_KV = SPLIT_Q + MLA_KV_RANK
SPLIT_KR = SPLIT_KV + MLA_ROPE_DIM

HEAD_PAD = 128
QK_PAD = MLA_HEADS * HEAD_PAD
OFF_CQ = 0
OFF_CKV = OFF_CQ + MLA_Q_RANK
OFF_KR = OFF_CKV + MLA_KV_RANK
OFF_A = OFF_KR + HEAD_PAD
OFF_G = OFF_A + CONV_CH
IN_EXT = OFF_G + CONV_CH

SUBLANES = 8
LANES = 128
BF16_ROWS = 16
ROUTE_CHUNK = 256
ROUTE_WIN = 64
ROUTE_INNER = 4
HALO = 16
VMEM_LIMIT = 56 * 1024 * 1024
FRONT_ROWS = 1024
ATTN_ROWS = 1024
MID_ROWS = 512
MEM_LEN = 256


def _cparams(sem):
    return pltpu.CompilerParams(dimension_semantics=sem, vmem_limit_bytes=VMEM_LIMIT)


def _layer_norm(v, g, b):
    mu = jnp.mean(v, axis=-1, keepdims=True)
    d = v - mu
    var = jnp.mean(d * d, axis=-1, keepdims=True)
    return d * lax.rsqrt(var + NORM_EPS) * g + b


def _rms_norm(v, g):
    return v * lax.rsqrt(jnp.mean(v * v, axis=-1, keepdims=True) + NORM_EPS) * g


def _sigmoid(v):
    return 1.0 / (1.0 + jnp.exp(-v))


def _dot(a, b):
    return jnp.dot(a, b, preferred_element_type=F32)


def _dot_nt(a, b, precision=None):
    return lax.dot_general(a, b, (((1,), (1,)), ((), ())), preferred_element_type=F32, precision=precision)


def _front_kernel(x_ref, pos_ref, win_ref, qg_ref, wq_ref, kvg_ref, wkv_ref, invf_ref, vone_ref,
                  q_out, k_out, v_out, u_out):
    x = x_ref[0].astype(BF16)
    hc = _dot(x, win_ref[...])
    cqn = _rms_norm(hc[:, OFF_CQ:OFF_CKV], qg_ref[...])
    qq = _dot(cqn.astype(BF16), wq_ref[...])
    ckvn = _rms_norm(hc[:, OFF_CKV:OFF_KR], kvg_ref[...])
    kv = _dot(ckvn.astype(BF16), wkv_ref[...])
    tm = x_ref.shape[1]
    ang = invf_ref[...] * pos_ref[0]
    cos_h, sin_h = jnp.cos(ang), jnp.sin(ang)
    ones = lambda n: jnp.ones((n, tm), F32)
    zeros = lambda n: jnp.zeros((n, tm), F32)
    cos = jnp.concatenate([ones(MLA_NOPE_DIM), cos_h, cos_h, ones(HEAD_PAD - MLA_QK_DIM)], axis=0).T
    sin = jnp.concatenate([zeros(MLA_NOPE_DIM), sin_h, sin_h, zeros(HEAD_PAD - MLA_QK_DIM)], axis=0).T
    kr = hc[:, OFF_KR:OFF_A]
    kr_partner = pltpu.roll(kr, HEAD_PAD - MLA_ROPE_DIM, axis=1)
    rope_lanes = lax.broadcasted_iota(jnp.int32, (1, HEAD_PAD), 1) < MLA_QK_DIM
    krot = jnp.where(rope_lanes, kr * cos + kr_partner * sin, 0.0)
    scale = math.log2(math.e) / math.sqrt(MLA_QK_DIM)
    for h in range(MLA_HEADS):
        lo, hi = h * HEAD_PAD, (h + 1) * HEAD_PAD
        qh = (qq[:, lo:hi] * cos + pltpu.roll(qq[:, lo:hi], HEAD_PAD - MLA_ROPE_DIM, axis=1) * sin) * scale
        q_out[0, :, lo:hi] = qh.astype(BF16)
        k_out[0, :, lo:hi] = (kv[:, lo:hi] + krot).astype(BF16)
    v_out[0] = (kv[:, QK_PAD:] + vone_ref[...]).astype(BF16)
    u_out[0] = hc[:, OFF_A:OFF_G] * _sigmoid(hc[:, OFF_G:IN_EXT])


def _front(x, posf, w_in_ext, qg, wq_ext, kvg, wkv_ext, invf, vone, tm):
    b, s, d = x.shape
    const = lambda bi, i: (0, 0)
    return pl.pallas_call(
        _front_kernel,
        grid=(b, s // tm),
        in_specs=[
            pl.BlockSpec((1, tm, d), lambda bi, i: (bi, i, 0)),
            pl.BlockSpec((1, 1, tm), lambda bi, i: (bi, 0, i)),
            pl.BlockSpec(w_in_ext.shape, const),
            pl.BlockSpec(qg.shape, const),
            pl.BlockSpec(wq_ext.shape, const),
            pl.BlockSpec(kvg.shape, const),
            pl.BlockSpec(wkv_ext.shape, const),
            pl.BlockSpec(invf.shape, const),
            pl.BlockSpec(vone.shape, const),
        ],
        out_specs=[
            pl.BlockSpec((1, tm, QK_PAD), lambda bi, i: (bi, i, 0)),
            pl.BlockSpec((1, tm, QK_PAD), lambda bi, i: (bi, i, 0)),
            pl.BlockSpec((1, tm, QK_PAD), lambda bi, i: (bi, i, 0)),
            pl.BlockSpec((1, tm, CONV_CH), lambda bi, i: (bi, i, 0)),
        ],
        out_shape=[
            jax.ShapeDtypeStruct((b, s, QK_PAD), BF16),
            jax.ShapeDtypeStruct((b, s, QK_PAD), BF16),
            jax.ShapeDtypeStruct((b, s, QK_PAD), BF16),
            jax.ShapeDtypeStruct((b, s, CONV_CH), F32),
        ],
        compiler_params=_cparams(("parallel", "parallel")),
        name="front",
    )(x, posf, w_in_ext, qg, wq_ext, kvg, wkv_ext, invf, vone)


def _attn_kernel(q_ref, k_ref, v_ref, o_ref):
    for h in range(MLA_HEADS):
        lo, hi = h * HEAD_PAD, (h + 1) * HEAD_PAD
        sc = _dot_nt(q_ref[0, :, lo:hi], k_ref[0, :, lo:hi])
        p = jnp.exp2(sc - jnp.max(sc, axis=-1, keepdims=True))
        pv = _dot(p.astype(BF16), v_ref[0, :, lo:hi])
        o = pv[:, :MLA_V_DIM] / pv[:, MLA_V_DIM:MLA_V_DIM + 1]
        o_ref[0, :, h * MLA_V_DIM:(h + 1) * MLA_V_DIM] = o.astype(BF16)


def _attention(q, k, v, tq):
    b, s, _ = q.shape
    return pl.pallas_call(
        _attn_kernel,
        grid=(b, s // tq),
        in_specs=[
            pl.BlockSpec((1, tq, QK_PAD), lambda bi, i: (bi, i, 0)),
            pl.BlockSpec((1, s, QK_PAD), lambda bi, i: (bi, 0, 0)),
            pl.BlockSpec((1, s, QK_PAD), lambda bi, i: (bi, 0, 0)),
        ],
        out_specs=pl.BlockSpec((1, tq, MLA_WIDTH), lambda bi, i: (bi, i, 0)),
        out_shape=jax.ShapeDtypeStruct((b, s, MLA_WIDTH), BF16),
        compiler_params=_cparams(("parallel", "parallel")),
        name="attn",
    )(q, k, v)


def _mem_kv_kernel(mem_ref, wk_ref, wv_ref, k_out, v_out):
    m = mem_ref[0].astype(BF16)
    k_out[0] = _dot(m, wk_ref[...]).astype(BF16)
    v_out[0] = _dot(m, wv_ref[...]).astype(BF16)


def _mem_kv(mem, wk, wv):
    b, m, d = mem.shape
    const = lambda bi: (0, 0)
    return pl.pallas_call(
        _mem_kv_kernel,
        grid=(b,),
        in_specs=[
            pl.BlockSpec((1, m, d), lambda bi: (bi, 0, 0)),
            pl.BlockSpec(wk.shape, const),
            pl.BlockSpec(wv.shape, const),
        ],
        out_specs=[pl.BlockSpec((1, m, d), lambda bi: (bi, 0, 0))] * 2,
        out_shape=[jax.ShapeDtypeStruct((b, m, d), BF16)] * 2,
        compiler_params=_cparams(("parallel",)),
        name="mem_kv",
    )(mem, wk, wv)


CONV_SUB = 64


def _mid_kernel(ucur_ref, uprev_ref, unext_ref, attn_ref, x_ref, cw_ref, cb_ref, cg_ref, cbeta_ref, wo_ref,
                g1_ref, b1_ref, wq_ref, kx_ref, vx_ref, xwo_ref, g2_ref, b2_ref, wr_ref,
                h2_out, h2b_out, aff_out, win_ref, shift_ref, u_scr, o_scr):
    tm = ucur_ref.shape[1]
    i = pl.program_id(1)
    last = pl.num_programs(1) - 1
    win_ref[0:HALO, :] = jnp.where(i > 0, uprev_ref[0], 0.0)
    win_ref[HALO:HALO + tm, :] = ucur_ref[0]
    win_ref[HALO + tm:2 * HALO + tm, :] = jnp.where(i < last, unext_ref[0], 0.0)
    span = tm + 2 * HALO - SUBLANES
    shift_ref[0] = win_ref[...]
    for j in range(1, SUBLANES):
        shift_ref[j, 0:span, :] = win_ref[j:j + span, :]

    for r in range(tm // CONV_SUB):
        acc = jnp.broadcast_to(cb_ref[...], (CONV_SUB, CONV_CH))
        for t in range(CONV_WIDTH):
            off = HALO - CONV_PAD + t
            row = r * CONV_SUB + (off // SUBLANES) * SUBLANES
            acc = acc + shift_ref[off % SUBLANES, row:row + CONV_SUB, :] * cw_ref[t:t + 1, :]
        y = _layer_norm(acc, cg_ref[...], cbeta_ref[...])
        u_scr[r * CONV_SUB:(r + 1) * CONV_SUB, :] = (y * _sigmoid(y)).astype(BF16)
    mix = _dot(attn_ref[0], wo_ref[0:MLA_WIDTH, :]) + _dot(u_scr[...], wo_ref[MLA_WIDTH:, :])
    h1 = _layer_norm(DEEPNORM_ALPHA * x_ref[0] + mix, g1_ref[...], b1_ref[...])
    q = _dot(h1.astype(BF16), wq_ref[...]).astype(BF16)
    for h in range(MEM_HEADS):
        lo, hi = h * MEM_HEAD_DIM, (h + 1) * MEM_HEAD_DIM
        sc = _dot_nt(q[:, lo:hi], kx_ref[0, :, lo:hi])
        p = jnp.exp(sc - jnp.max(sc, axis=-1, keepdims=True))
        l = jnp.sum(p, axis=-1, keepdims=True)
        o_scr[:, lo:hi] = (_dot(p.astype(BF16), vx_ref[0, :, lo:hi]) / l).astype(BF16)
    xa = _dot(o_scr[...], xwo_ref[...])
    h2 = _layer_norm(DEEPNORM_ALPHA * h1 + xa, g2_ref[...], b2_ref[...])
    h2_out[0] = h2
    h2_hi = h2.astype(BF16)
    h2b_out[0] = h2_hi
    h2_lo = (h2 - h2_hi.astype(F32)).astype(BF16)
    hi_terms = _dot(h2_hi, wr_ref[...])
    logits = hi_terms[:, :LANES] + hi_terms[:, LANES:] + _dot(h2_lo, wr_ref[:, :LANES])
    lt = logits.T[0:aff_out.shape[1], :]
    ex = jnp.exp(lt - jnp.max(lt, axis=0, keepdims=True))
    aff_out[0] = ex / jnp.sum(ex, axis=0, keepdims=True)


def _mid(u_pre, attn, x, conv_w, conv_b, conv_g, conv_beta, w_o, g1, b1, wq, kx, vx, xwo, g2, b2, wr_split, tm):
    b, s, d = x.shape
    m = kx.shape[1]
    nh = tm // HALO
    const = lambda bi, i: (0, 0)
    tile = lambda w: pl.BlockSpec((1, tm, w), lambda bi, i: (bi, i, 0))
    whole = lambda arr: pl.BlockSpec(arr.shape, const)
    return pl.pallas_call(
        _mid_kernel,
        grid=(b, s // tm),
        in_specs=[
            tile(CONV_CH),
            pl.BlockSpec((1, HALO, CONV_CH), lambda bi, i: (bi, jnp.maximum(i * nh - 1, 0), 0)),
            pl.BlockSpec((1, HALO, CONV_CH), lambda bi, i: (bi, jnp.minimum((i + 1) * nh, s // HALO - 1), 0)),
            tile(MLA_WIDTH),
            tile(d),
            whole(conv_w), whole(conv_b), whole(conv_g), whole(conv_beta), whole(w_o), whole(g1), whole(b1),
            whole(wq),
            pl.BlockSpec((1, m, d), lambda bi, i: (bi, 0, 0)),
            pl.BlockSpec((1, m, d), lambda bi, i: (bi, 0, 0)),
            whole(xwo), whole(g2), whole(b2), whole(wr_split),
        ],
        out_specs=[
            tile(d),
            tile(d),
            pl.BlockSpec((1, N_EXPERTS, tm), lambda bi, i: (bi, 0, i)),
        ],
        out_shape=[
            jax.ShapeDtypeStruct((b, s, d), F32),
            jax.ShapeDtypeStruct((b, s, d), BF16),
            jax.ShapeDtypeStruct((b, N_EXPERTS, s), F32),
        ],
        scratch_shapes=[
            pltpu.VMEM((tm + 2 * HALO, CONV_CH), F32),
            pltpu.VMEM((SUBLANES, tm + 2 * HALO, CONV_CH), F32),
            pltpu.VMEM((tm, CONV_CH), BF16),
            pltpu.VMEM((tm, d), BF16),
        ],
        compiler_params=_cparams(("parallel", "parallel")),
        name="mid",
    )(u_pre, u_pre, u_pre, attn, x, conv_w, conv_b, conv_g, conv_beta, w_o, g1, b1, wq, kx, vx, xwo, g2, b2, wr_split)


def _split_router(w_router):
    hi = w_router.astype(BF16)
    lo = (w_router - hi.astype(F32)).astype(BF16)
    pad = jnp.zeros((w_router.shape[0], LANES - w_router.shape[1]), BF16)
    return jnp.concatenate([hi, pad, lo, pad], axis=1)


BISECT_STEPS_PER_CHECK = 4
BISECT_MAX_CHECKS = 320


def _topk_kernel(aff_ref, slot_out, slot_t_out, off_out, *, cap):
    aff = aff_ref[...]
    rows, s = aff.shape
    capf = jnp.float32(cap)

    def not_done(carry):
        return jnp.logical_and(carry[2] > 0, carry[3] < BISECT_MAX_CHECKS)

    def halve(_, bounds):
        lo, hi = bounds
        mid = 0.5 * (lo + hi)
        take = jnp.sum(jnp.where(aff >= mid, 1.0, 0.0), axis=1, keepdims=True) >= capf
        return jnp.where(take, mid, lo), jnp.where(take, hi, mid)

    def bisect(carry):
        lo, hi = lax.fori_loop(0, BISECT_STEPS_PER_CHECK, halve, carry[:2])
        smallest_in = jnp.min(jnp.where(aff >= lo, aff, jnp.inf), axis=1, keepdims=True)
        largest_in = jnp.max(jnp.where(aff < hi, aff, -jnp.inf), axis=1, keepdims=True)
        open_rows = jnp.sum(jnp.where(smallest_in == largest_in, 0.0, 1.0))
        return lo, hi, open_rows.astype(jnp.int32), carry[3] + 1

    _, hi, _, _ = lax.while_loop(not_done, bisect, (jnp.zeros((rows, 1), F32), jnp.full((rows, 1), 2.0, F32),
                                                    jnp.int32(1), jnp.int32(0)))
    th = jnp.max(jnp.where(aff < hi, aff, -jnp.inf), axis=1, keepdims=True)
    gt = aff > th
    eq = aff == th
    n_gt = jnp.sum(jnp.where(gt, 1.0, 0.0), axis=1, keepdims=True)
    tri = jnp.where(lax.broadcasted_iota(jnp.int32, (s, s), 0) < lax.broadcasted_iota(jnp.int32, (s, s), 1),
                    1.0, 0.0).astype(BF16)
    tie_rank = _dot(jnp.where(eq, 1.0, 0.0).astype(BF16), tri)
    sel = jnp.logical_or(gt, jnp.logical_and(eq, tie_rank < (capf - n_gt)))
    pos = _dot(jnp.where(sel, 1.0, 0.0).astype(BF16), tri)
    slot = jnp.where(sel, pos, -1.0)
    slot_out[...] = slot
    slot_t_out[...] = slot.T
    lanes = off_out.shape[1]
    before = jnp.where(lax.broadcasted_iota(jnp.int32, (s, lanes), 0)
                       < ROUTE_CHUNK * lax.broadcasted_iota(jnp.int32, (s, lanes), 1), 1.0, 0.0).astype(BF16)
    off_out[...] = _dot(jnp.where(sel, 1.0, 0.0).astype(BF16), before)


def _topk(aff2d, cap):
    rows, s = aff2d.shape
    return pl.pallas_call(
        functools.partial(_topk_kernel, cap=cap),
        out_shape=[
            jax.ShapeDtypeStruct((rows, s), F32),
            jax.ShapeDtypeStruct((s, rows), F32),
            jax.ShapeDtypeStruct((rows, LANES), F32),
        ],
        compiler_params=pltpu.CompilerParams(vmem_limit_bytes=VMEM_LIMIT),
        name="topk",
    )(aff2d)


def _route_tables(off_tab, n_b, n_e, cap):
    n_j = off_tab.shape[1] - 1
    off = off_tab.astype(jnp.int32)
    base = jnp.minimum((off[:, :n_j] // BF16_ROWS) * BF16_ROWS, cap - ROUTE_WIN)
    over = jnp.any((off[:, 1:] - base > ROUTE_WIN).reshape(n_b, n_e, n_j), axis=1)
    return base, over.astype(jnp.int32)


def _dispatch_kernel(base_ref, over_ref, h2b_ref, slot_ref, aff_ref, xs_out, gate_out, p_scr, gate_scr, *, cap):
    bi = pl.program_id(0)
    jo = pl.program_id(1)
    n_j = pl.num_programs(1) * ROUTE_INNER
    n_e, _, t = slot_ref.shape

    @pl.when(jo == 0)
    def _():
        xs_out[...] = jnp.zeros(xs_out.shape, xs_out.dtype)
        gate_scr[...] = jnp.zeros(gate_scr.shape, gate_scr.dtype)

    def chunk(ji, carry):
        j = jo * ROUTE_INNER + ji
        h = h2b_ref[0, pl.ds(pl.multiple_of(ji * t, t), t), :]
        slot_row = lambda e: slot_ref[e, pl.ds(j, 1), :]
        aff_row = lambda e: aff_ref[e, pl.ds(j, 1), :]

        @pl.when(over_ref[bi * n_j + j] == 0)
        def _():
            w_iota = lax.broadcasted_iota(jnp.int32, (ROUTE_WIN, t), 0).astype(F32)
            bases = []
            for e in range(n_e):
                base = pl.multiple_of(base_ref[(bi * n_e + e) * n_j + j], BF16_ROWS)
                hit = slot_row(e) == w_iota + base.astype(F32)
                p_scr[e * ROUTE_WIN:(e + 1) * ROUTE_WIN, :] = jnp.where(hit, 1.0, 0.0).astype(BF16)
                gate_scr[e, pl.ds(base, ROUTE_WIN), :] += jnp.sum(jnp.where(hit, aff_row(e), 0.0),
                                                                  axis=1, keepdims=True)
                bases.append(base)
            picked = _dot(p_scr[...], h)
            for e in range(n_e):
                rows = pl.ds(bases[e], ROUTE_WIN)
                xs_out[e, 0, rows, :] += picked[e * ROUTE_WIN:(e + 1) * ROUTE_WIN, :].astype(BF16)

        @pl.when(over_ref[bi * n_j + j] != 0)
        def _():
            c_iota = lax.broadcasted_iota(jnp.int32, (cap, t), 0).astype(F32)
            for e in range(n_e):
                hit = slot_row(e) == c_iota
                xs_out[e, 0] += _dot(jnp.where(hit, 1.0, 0.0).astype(BF16), h).astype(BF16)
                gate_scr[e] += jnp.sum(jnp.where(hit, aff_row(e), 0.0), axis=1, keepdims=True)

        return carry

    lax.fori_loop(0, ROUTE_INNER, chunk, 0)

    @pl.when(jo == pl.num_programs(1) - 1)
    def _():
        eye = lax.broadcasted_iota(jnp.int32, (cap, cap), 0) == lax.broadcasted_iota(jnp.int32, (cap, cap), 1)
        for e in range(n_e):
            gate_out[e, 0] = jnp.sum(jnp.where(eye, gate_scr[e], 0.0), axis=0, keepdims=True)


def _dispatch(base_flat, over_flat, h2b, slot3d, aff3d, n_e, cap):
    b, s, d = h2b.shape
    t = ROUTE_CHUNK
    rows = ROUTE_INNER * t
    grid_spec = pltpu.PrefetchScalarGridSpec(
        num_scalar_prefetch=2,
        grid=(b, s // rows),
        in_specs=[
            pl.BlockSpec((1, rows, d), lambda bi, jo, *_: (bi, jo, 0)),
            pl.BlockSpec((n_e, s // t, t), lambda bi, jo, *_: (bi, 0, 0)),
            pl.BlockSpec((n_e, s // t, t), lambda bi, jo, *_: (bi, 0, 0)),
        ],
        out_specs=[
            pl.BlockSpec((n_e, 1, cap, d), lambda bi, jo, *_: (0, bi, 0, 0)),
            pl.BlockSpec((n_e, 1, 1, cap), lambda bi, jo, *_: (0, bi, 0, 0)),
        ],
        scratch_shapes=[pltpu.VMEM((n_e * ROUTE_WIN, t), BF16), pltpu.VMEM((n_e, cap, 1), F32)],
    )
    return pl.pallas_call(
        functools.partial(_dispatch_kernel, cap=cap),
        grid_spec=grid_spec,
        out_shape=[
            jax.ShapeDtypeStruct((n_e, b, cap, d), BF16),
            jax.ShapeDtypeStruct((n_e, b, 1, cap), F32),
        ],
        compiler_params=_cparams(("parallel", "arbitrary")),
        name="dispatch",
    )(base_flat, over_flat, h2b, slot3d, aff3d)


FFN_ROWS = 512
FFN_CHUNK = 512


def _experts_kernel(xs_ref, gate_ref, wg_ref, wu_ref, wd_ref, y_out, wg_s, wu_s, wd_s, hid_s):
    e = pl.program_id(0)
    f = pl.program_id(1)
    n_chunks = wg_s.shape[1]

    def stage():
        slot = e % 2
        wg_s[slot, f] = wg_ref[0].astype(BF16)
        wu_s[slot, f] = wu_ref[0].astype(BF16)
        wd_s[slot, f] = wd_ref[0].astype(BF16)

    @pl.when(e == 0)
    def _():
        stage()
        y_out[0] = jnp.zeros(y_out.shape[1:], y_out.dtype)

    @pl.when(e > 0)
    def _():
        stage()
        slot = (e - 1) % 2
        xs = xs_ref[0]
        for c in range(n_chunks):
            g = _dot(xs, wg_s[slot, c])
            u = _dot(xs, wu_s[slot, c])
            hid_s[:, c * FFN_CHUNK:(c + 1) * FFN_CHUNK] = (g * _sigmoid(g) * u).astype(BF16)
        y = _dot(hid_s[:, 0:FFN_CHUNK], wd_s[slot, 0])
        for c in range(1, n_chunks):
            y = y + _dot(hid_s[:, c * FFN_CHUNK:(c + 1) * FFN_CHUNK], wd_s[slot, c])
        rows = xs.shape[0]
        eye = lax.broadcasted_iota(jnp.int32, (rows, rows), 0) == lax.broadcasted_iota(jnp.int32, (rows, rows), 1)
        gate_col = jnp.sum(jnp.where(eye, gate_ref[0], 0.0), axis=1, keepdims=True)
        y_out[0] = (y * gate_col).astype(BF16)


def _experts(xs, gate, w_gate, w_up, w_down):
    e, n, d = xs.shape
    ff = w_gate.shape[2]
    n_chunks = ff // FFN_CHUNK
    assert n // FFN_ROWS == n_chunks
    prev = lambda ei, fi: (jnp.maximum(ei - 1, 0), fi, 0)
    cur = lambda ei: jnp.minimum(ei, e - 1)
    return pl.pallas_call(
        _experts_kernel,
        grid=(e + 1, n_chunks),
        in_specs=[
            pl.BlockSpec((1, FFN_ROWS, d), prev),
            pl.BlockSpec((1, 1, FFN_ROWS), lambda ei, fi: (jnp.maximum(ei - 1, 0), 0, fi)),
            pl.BlockSpec((1, d, FFN_CHUNK), lambda ei, fi: (cur(ei), 0, fi)),
            pl.BlockSpec((1, d, FFN_CHUNK), lambda ei, fi: (cur(ei), 0, fi)),
            pl.BlockSpec((1, FFN_CHUNK, d), lambda ei, fi: (cur(ei), fi, 0)),
        ],
        out_specs=pl.BlockSpec((1, FFN_ROWS, d), lambda ei, fi: (jnp.where(ei == 0, e, ei - 1), fi, 0)),
        out_shape=jax.ShapeDtypeStruct((e + 1, n, d), BF16),
        scratch_shapes=[
            pltpu.VMEM((2, n_chunks, d, FFN_CHUNK), BF16),
            pltpu.VMEM((2, n_chunks, d, FFN_CHUNK), BF16),
            pltpu.VMEM((2, n_chunks, FFN_CHUNK, d), BF16),
            pltpu.VMEM((FFN_ROWS, ff), BF16),
        ],
        compiler_params=_cparams(("arbitrary", "arbitrary")),
        name="experts",
    )(xs, gate, w_gate, w_up, w_down)


def _combine_kernel(base_ref, over_ref, slot_t_ref, base_t_ref, y_ref, h2_ref, g3_ref, b3_ref, out_ref, ywin_scr,
                    *, cap):
    bi = pl.program_id(0)
    jo = pl.program_id(1)
    n_j = pl.num_programs(1) * ROUTE_INNER
    n_e = y_ref.shape[0]
    rows = slot_t_ref.shape[1]
    t = ROUTE_CHUNK

    def spread_cols(per_expert):
        width = n_e * per_expert
        col = lax.broadcasted_iota(jnp.int32, (rows, width), 1) // per_expert
        return jnp.where(lax.broadcasted_iota(jnp.int32, (rows, width), 0) == bi * n_e + col, 1.0, 0.0).astype(BF16)

    def lane_in_group(per_expert):
        return (lax.broadcasted_iota(jnp.int32, (1, n_e * per_expert), 1) % per_expert).astype(F32)

    win_spread = spread_cols(ROUTE_WIN)
    win_lane = lane_in_group(ROUTE_WIN)

    def chunk(ji, carry):
        j = jo * ROUTE_INNER + ji
        tok = pl.ds(pl.multiple_of(ji * t, t), t)

        def finish(ff):
            out_ref[0, tok, :] = _layer_norm(DEEPNORM_ALPHA * h2_ref[0, tok, :] + ff, g3_ref[...], b3_ref[...])

        slot_bf = slot_t_ref[tok, :].astype(BF16)

        @pl.when(over_ref[bi * n_j + j] == 0)
        def _():
            slot_wide = _dot(slot_bf, win_spread)
            base_wide = _dot(jnp.broadcast_to(base_t_ref[ji], (SUBLANES, rows)).astype(BF16), win_spread)[0:1, :]
            onehot = jnp.where(slot_wide - base_wide == win_lane, 1.0, 0.0).astype(BF16)
            for e in range(n_e):
                base = pl.multiple_of(base_ref[(bi * n_e + e) * n_j + j], BF16_ROWS)
                ywin_scr[e * ROUTE_WIN:(e + 1) * ROUTE_WIN, :] = y_ref[e, 0, pl.ds(base, ROUTE_WIN), :]
            finish(_dot(onehot, ywin_scr[...]))

        @pl.when(over_ref[bi * n_j + j] != 0)
        def _():
            slot_wide = _dot(slot_bf, spread_cols(cap))
            onehot = jnp.where(slot_wide == lane_in_group(cap), 1.0, 0.0).astype(BF16)
            finish(_dot(onehot, y_ref[:, 0].reshape(n_e * cap, y_ref.shape[3])))

        return carry

    lax.fori_loop(0, ROUTE_INNER, chunk, 0)


def _combine(base_flat, over_flat, slot_t, base_t, y4, h2, g3, b3, cap):
    b, s, d = h2.shape
    e = y4.shape[0] - 1
    rows = ROUTE_INNER * ROUTE_CHUNK
    const = lambda bi, jo, *_: (0, 0)
    grid_spec = pltpu.PrefetchScalarGridSpec(
        num_scalar_prefetch=2,
        grid=(b, s // rows),
        in_specs=[
            pl.BlockSpec((rows, slot_t.shape[1]), lambda bi, jo, *_: (jo, 0)),
            pl.BlockSpec((ROUTE_INNER, 1, base_t.shape[2]), lambda bi, jo, *_: (jo, 0, 0)),
            pl.BlockSpec((e, 1, cap, d), lambda bi, jo, *_: (0, bi, 0, 0)),
            pl.BlockSpec((1, rows, d), lambda bi, jo, *_: (bi, jo, 0)),
            pl.BlockSpec(g3.shape, const),
            pl.BlockSpec(b3.shape, const),
        ],
        out_specs=pl.BlockSpec((1, rows, d), lambda bi, jo, *_: (bi, jo, 0)),
        scratch_shapes=[pltpu.VMEM((e * ROUTE_WIN, d), BF16)],
    )
    return pl.pallas_call(
        functools.partial(_combine_kernel, cap=cap),
        grid_spec=grid_spec,
        out_shape=jax.ShapeDtypeStruct((b, s, d), F32),
        compiler_params=_cparams(("parallel", "parallel")),
        name="combine",
    )(base_flat, over_flat, slot_t, base_t, y4, h2, g3, b3)


def _extend_weights(w_in, w_uq, w_uk, w_uv):
    half = MLA_ROPE_DIM // 2
    d = w_in.shape[0]
    w_t = w_in.T.astype(BF16)
    kr = w_t[SPLIT_KV:SPLIT_KR]
    z = lambda n: jnp.zeros((n, d), BF16)
    assert HEAD_PAD - MLA_QK_DIM == MLA_ROPE_DIM
    kr_tile = jnp.concatenate([z(MLA_NOPE_DIM), kr, -kr[half:], kr[:half]], axis=0)
    w_in_ext = jnp.concatenate([w_t[:SPLIT_KV], kr_tile, w_t[SPLIT_KR:]], axis=0).T

    wq = w_uq.reshape(MLA_Q_RANK, MLA_HEADS, MLA_QK_DIM)
    wq_ext = jnp.concatenate([wq, -wq[:, :, MLA_NOPE_DIM + half:], wq[:, :, MLA_NOPE_DIM:MLA_NOPE_DIM + half]],
                             axis=2).reshape(MLA_Q_RANK, QK_PAD)

    wk = w_uk.reshape(MLA_KV_RANK, MLA_HEADS, MLA_NOPE_DIM)
    wk_full = jnp.concatenate([wk, jnp.zeros((MLA_KV_RANK, MLA_HEADS, HEAD_PAD - MLA_NOPE_DIM), w_uk.dtype)], axis=2)
    wv = w_uv.reshape(MLA_KV_RANK, MLA_HEADS, MLA_V_DIM)
    wv_full = jnp.concatenate([wv, jnp.zeros((MLA_KV_RANK, MLA_HEADS, HEAD_PAD - MLA_V_DIM), w_uv.dtype)], axis=2)
    wkv_ext = jnp.concatenate([wk_full.reshape(MLA_KV_RANK, QK_PAD), wv_full.reshape(MLA_KV_RANK, QK_PAD)], axis=1)
    return w_in_ext.astype(BF16), wq_ext.astype(BF16), wkv_ext.astype(BF16)


def _rope_freq_column():
    inv_freq = ROPE_BASE ** (-jnp.arange(0, MLA_ROPE_DIM, 2, dtype=F32) / MLA_ROPE_DIM)
    return inv_freq[:, None]


def _value_one_lanes():
    one_hot = (jnp.arange(HEAD_PAD) == MLA_V_DIM).astype(F32)
    return jnp.tile(one_hot, MLA_HEADS)[None, :]


def kernel(x, mem, positions, w_in, q_norm_g, w_uq, kv_norm_g, w_uk, w_uv, conv_w, conv_b, conv_ln_g, conv_ln_b,
           w_o, ln1_g, ln1_b, xa_w_q, xa_w_k, xa_w_v, xa_w_o, ln2_g, ln2_b, w_router, w_gate, w_up, w_down,
           ln3_g, ln3_b):
    assert w_in.shape[0] == DEPTH == 1
    b, s, d = x.shape
    cap = CAPACITY_FACTOR * s // N_EXPERTS
    assert s % max(FRONT_ROWS, ATTN_ROWS, MID_ROWS, ROUTE_INNER * ROUTE_CHUNK) == 0 and cap >= ROUTE_WIN
    assert mem.shape == (b, MEM_LEN, d) and d == D_MODEL

    w_in_ext, wq_ext, wkv_ext = _extend_weights(w_in[0], w_uq[0], w_uk[0], w_uv[0])
    posf = positions.astype(F32)[:, None, :]
    q, k, v, u_pre = _front(x, posf, w_in_ext, q_norm_g, wq_ext, kv_norm_g, wkv_ext, _rope_freq_column(),
                             _value_one_lanes(), FRONT_ROWS)
    attn = _attention(q, k, v, ATTN_ROWS)
    kx, vx = _mem_kv(mem, xa_w_k[0].astype(BF16), xa_w_v[0].astype(BF16))
    h2, h2b, aff = _mid(u_pre, attn, x, conv_w[0], conv_b, conv_ln_g, conv_ln_b, w_o[0].astype(BF16), ln1_g, ln1_b,
                        (xa_w_q[0] * MEM_Q_SCALE).astype(BF16), kx, vx, xa_w_o[0].astype(BF16), ln2_g, ln2_b,
                        _split_router(w_router[0]), MID_ROWS)

    aff2d = aff.reshape(b * N_EXPERTS, s)
    slot, slot_t, off_tab = _topk(aff2d, cap)
    n_j = s // ROUTE_CHUNK
    base, over = _route_tables(off_tab[:, :n_j + 1], b, N_EXPERTS, cap)
    base_flat, over_flat = base.reshape(-1), over.reshape(-1)
    by_chunk = lambda a: a.reshape(b * N_EXPERTS, n_j, ROUTE_CHUNK)
    xs, gate = _dispatch(base_flat, over_flat, h2b, by_chunk(slot), by_chunk(aff2d), N_EXPERTS, cap)
    y = _experts(xs.reshape(N_EXPERTS, b * cap, d), gate.reshape(N_EXPERTS, 1, b * cap),
                 w_gate[0], w_up[0], w_down[0])
    base_t = base.T.astype(F32).reshape(n_j, 1, b * N_EXPERTS)
    return _combine(base_flat, over_flat, slot_t, base_t, y.reshape(N_EXPERTS + 1, b, cap, d), h2, ln3_g, ln3_b, cap)
```

```python
import functools
import math

import jax
import jax.numpy as jnp
from jax import lax
from jax.experimental import pallas as pl
from jax.experimental.pallas import tpu as pltpu

F32 = jnp.float32
BF16 = jnp.bfloat16

D_MODEL = 1024
MLA_HEADS = 8
MLA_NOPE_DIM = 64
MLA_ROPE_DIM = 32
MLA_QK_DIM = MLA_NOPE_DIM + MLA_ROPE_DIM
MLA_V_DIM = 64
MLA_Q_RANK = 256
MLA_KV_RANK = 128
MLA_WIDTH = MLA_HEADS * MLA_V_DIM
CONV_CH = D_MODEL - MLA_WIDTH
CONV_WIDTH = 31
CONV_PAD = (CONV_WIDTH - 1) // 2
ROPE_BASE = 10000.0
MEM_HEADS = 4
MEM_HEAD_DIM = D_MODEL // MEM_HEADS
MEM_Q_SCALE = 1.0 / math.sqrt(MEM_HEAD_DIM)
assert math.frexp(MEM_Q_SCALE)[0] == 0.5
N_EXPERTS = 16
EXPERT_FF = 2048
CAPACITY_FACTOR = 2
NORM_EPS = 1e-5
DEPTH = 1
DEEPNORM_ALPHA = (2.0 * DEPTH) ** 0.25
SPLIT_Q = MLA_Q_RANK
SPLIT_KV = SPLIT_Q + MLA_KV_RANK
SPLIT_KR = SPLIT_KV + MLA_ROPE_DIM

HEAD_PAD = 128
QK_PAD = MLA_HEADS * HEAD_PAD
OFF_CQ = 0
OFF_CKV = OFF_CQ + MLA_Q_RANK
OFF_KR = OFF_CKV + MLA_KV_RANK
OFF_A = OFF_KR + HEAD_PAD
OFF_G = OFF_A + CONV_CH
IN_EXT = OFF_G + CONV_CH

SUBLANES = 8
LANES = 128
BF16_ROWS = 16
ROUTE_CHUNK = 256
ROUTE_WIN = 64
ROUTE_INNER = 4
HALO = 16
VMEM_LIMIT = 56 * 1024 * 1024
FRONT_ROWS = 1024
ATTN_ROWS = 1024
MID_ROWS = 512
MEM_LEN = 256


def _cparams(sem):
    return pltpu.CompilerParams(dimension_semantics=sem, vmem_limit_bytes=VMEM_LIMIT)


def _layer_norm(v, g, b):
    mu = jnp.mean(v, axis=-1, keepdims=True)
    d = v - mu
    var = jnp.mean(d * d, axis=-1, keepdims=True)
    return d * lax.rsqrt(var + NORM_EPS) * g + b


def _rms_norm(v, g):
    return v * lax.rsqrt(jnp.mean(v * v, axis=-1, keepdims=True) + NORM_EPS) * g


def _sigmoid(v):
    return 1.0 / (1.0 + jnp.exp(-v))


def _dot(a, b):
    return jnp.dot(a, b, preferred_element_type=F32)


def _dot_nt(a, b, precision=None):
    return lax.dot_general(a, b, (((1,), (1,)), ((), ())), preferred_element_type=F32, precision=precision)


def _front_kernel(x_ref, pos_ref, win_ref, qg_ref, wq_ref, kvg_ref, wkv_ref, invf_ref, vone_ref,
                  q_out, k_out, v_out, u_out):
    x = x_ref[0].astype(BF16)
    hc = _dot(x, win_ref[...])
    cqn = _rms_norm(hc[:, OFF_CQ:OFF_CKV], qg_ref[...])
    qq = _dot(cqn.astype(BF16), wq_ref[...])
    ckvn = _rms_norm(hc[:, OFF_CKV:OFF_KR], kvg_ref[...])
    kv = _dot(ckvn.astype(BF16), wkv_ref[...])
    tm = x_ref.shape[1]
    ang = invf_ref[...] * pos_ref[0]
    cos_h, sin_h = jnp.cos(ang), jnp.sin(ang)
    ones = lambda n: jnp.ones((n, tm), F32)
    zeros = lambda n: jnp.zeros((n, tm), F32)
    cos = jnp.concatenate([ones(MLA_NOPE_DIM), cos_h, cos_h, ones(HEAD_PAD - MLA_QK_DIM)], axis=0).T
    sin = jnp.concatenate([zeros(MLA_NOPE_DIM), sin_h, sin_h, zeros(HEAD_PAD - MLA_QK_DIM)], axis=0).T
    kr = hc[:, OFF_KR:OFF_A]
    kr_partner = pltpu.roll(kr, HEAD_PAD - MLA_ROPE_DIM, axis=1)
    rope_lanes = lax.broadcasted_iota(jnp.int32, (1, HEAD_PAD), 1) < MLA_QK_DIM
    krot = jnp.where(rope_lanes, kr * cos + kr_partner * sin, 0.0)
    scale = math.log2(math.e) / math.sqrt(MLA_QK_DIM)
    for h in range(MLA_HEADS):
        lo, hi = h * HEAD_PAD, (h + 1) * HEAD_PAD
        qh = (qq[:, lo:hi] * cos + pltpu.roll(qq[:, lo:hi], HEAD_PAD - MLA_ROPE_DIM, axis=1) * sin) * scale
        q_out[0, :, lo:hi] = qh.astype(BF16)
        k_out[0, :, lo:hi] = (kv[:, lo:hi] + krot).astype(BF16)
    v_out[0] = (kv[:, QK_PAD:] + vone_ref[...]).astype(BF16)
    u_out[0] = hc[:, OFF_A:OFF_G] * _sigmoid(hc[:, OFF_G:IN_EXT])


def _front(x, posf, w_in_ext, qg, wq_ext, kvg, wkv_ext, invf, vone, tm):
    b, s, d = x.shape
    const = lambda bi, i: (0, 0)
    return pl.pallas_call(
        _front_kernel,
        grid=(b, s // tm),
        in_specs=[
            pl.BlockSpec((1, tm, d), lambda bi, i: (bi, i, 0)),
            pl.BlockSpec((1, 1, tm), lambda bi, i: (bi, 0, i)),
            pl.BlockSpec(w_in_ext.shape, const),
            pl.BlockSpec(qg.shape, const),
            pl.BlockSpec(wq_ext.shape, const),
            pl.BlockSpec(kvg.shape, const),
            pl.BlockSpec(wkv_ext.shape, const),
            pl.BlockSpec(invf.shape, const),
            pl.BlockSpec(vone.shape, const),
        ],
        out_specs=[
            pl.BlockSpec((1, tm, QK_PAD), lambda bi, i: (bi, i, 0)),
            pl.BlockSpec((1, tm, QK_PAD), lambda bi, i: (bi, i, 0)),
            pl.BlockSpec((1, tm, QK_PAD), lambda bi, i: (bi, i, 0)),
            pl.BlockSpec((1, tm, CONV_CH), lambda bi, i: (bi, i, 0)),
        ],
        out_shape=[
            jax.ShapeDtypeStruct((b, s, QK_PAD), BF16),
            jax.ShapeDtypeStruct((b, s, QK_PAD), BF16),
            jax.ShapeDtypeStruct((b, s, QK_PAD), BF16),
            jax.ShapeDtypeStruct((b, s, CONV_CH), F32),
        ],
        compiler_params=_cparams(("parallel", "parallel")),
        name="front",
    )(x, posf, w_in_ext, qg, wq_ext, kvg, wkv_ext, invf, vone)


def _attn_kernel(q_ref, k_ref, v_ref, o_ref):
    for h in range(MLA_HEADS):
        lo, hi = h * HEAD_PAD, (h + 1) * HEAD_PAD
        sc = _dot_nt(q_ref[0, :, lo:hi], k_ref[0, :, lo:hi])
        p = jnp.exp2(sc - jnp.max(sc, axis=-1, keepdims=True))
        pv = _dot(p.astype(BF16), v_ref[0, :, lo:hi])
        o = pv[:, :MLA_V_DIM] / pv[:, MLA_V_DIM:MLA_V_DIM + 1]
        o_ref[0, :, h * MLA_V_DIM:(h + 1) * MLA_V_DIM] = o.astype(BF16)


def _attention(q, k, v, tq):
    b, s, _ = q.shape
    return pl.pallas_call(
        _attn_kernel,
        grid=(b, s // tq),
        in_specs=[
            pl.BlockSpec((1, tq, QK_PAD), lambda bi, i: (bi, i, 0)),
            pl.BlockSpec((1, s, QK_PAD), lambda bi, i: (bi, 0, 0)),
            pl.BlockSpec((1, s, QK_PAD), lambda bi, i: (bi, 0, 0)),
        ],
        out_specs=pl.BlockSpec((1, tq, MLA_WIDTH), lambda bi, i: (bi, i, 0)),
        out_shape=jax.ShapeDtypeStruct((b, s, MLA_WIDTH), BF16),
        compiler_params=_cparams(("parallel", "parallel")),
        name="attn",
    )(q, k, v)


def _mem_kv_kernel(mem_ref, wk_ref, wv_ref, k_out, v_out):
    m = mem_ref[0].astype(BF16)
    k_out[0] = _dot(m, wk_ref[...]).astype(BF16)
    v_out[0] = _dot(m, wv_ref[...]).astype(BF16)


def _mem_kv(mem, wk, wv):
    b, m, d = mem.shape
    const = lambda bi: (0, 0)
    return pl.pallas_call(
        _mem_kv_kernel,
        grid=(b,),
        in_specs=[
            pl.BlockSpec((1, m, d), lambda bi: (bi, 0, 0)),
            pl.BlockSpec(wk.shape, const),
            pl.BlockSpec(wv.shape, const),
        ],
        out_specs=[pl.BlockSpec((1, m, d), lambda bi: (bi, 0, 0))] * 2,
        out_shape=[jax.ShapeDtypeStruct((b, m, d), BF16)] * 2,
        compiler_params=_cparams(("parallel",)),
        name="mem_kv",
    )(mem, wk, wv)


CONV_SUB = 64


def _mid_kernel(ucur_ref, uprev_ref, unext_ref, attn_ref, x_ref, cw_ref, cb_ref, cg_ref, cbeta_ref, wo_ref,
                g1_ref, b1_ref, wq_ref, kx_ref, vx_ref, xwo_ref, g2_ref, b2_ref, wr_ref,
                h2_out, h2b_out, aff_out, win_ref, shift_ref, u_scr, o_scr):
    tm = ucur_ref.shape[1]
    i = pl.program_id(1)
    last = pl.num_programs(1) - 1
    win_ref[0:HALO, :] = jnp.where(i > 0, uprev_ref[0], 0.0)
    win_ref[HALO:HALO + tm, :] = ucur_ref[0]
    win_ref[HALO + tm:2 * HALO + tm, :] = jnp.where(i < last, unext_ref[0], 0.0)
    span = tm + 2 * HALO - SUBLANES
    shift_ref[0] = win_ref[...]
    for j in range(1, SUBLANES):
        shift_ref[j, 0:span, :] = win_ref[j:j + span, :]

    for r in range(tm // CONV_SUB):
        acc = jnp.broadcast_to(cb_ref[...], (CONV_SUB, CONV_CH))
        for t in range(CONV_WIDTH):
            off = HALO - CONV_PAD + t
            row = r * CONV_SUB + (off // SUBLANES) * SUBLANES
            acc = acc + shift_ref[off % SUBLANES, row:row + CONV_SUB, :] * cw_ref[t:t + 1, :]
        y = _layer_norm(acc, cg_ref[...], cbeta_ref[...])
        u_scr[r * CONV_SUB:(r + 1) * CONV_SUB, :] = (y * _sigmoid(y)).astype(BF16)
    mix = _dot(attn_ref[0], wo_ref[0:MLA_WIDTH, :]) + _dot(u_scr[...], wo_ref[MLA_WIDTH:, :])
    h1 = _layer_norm(DEEPNORM_ALPHA * x_ref[0] + mix, g1_ref[...], b1_ref[...])
    q = _dot(h1.astype(BF16), wq_ref[...]).astype(BF16)
    for h in range(MEM_HEADS):
        lo, hi = h * MEM_HEAD_DIM, (h + 1) * MEM_HEAD_DIM
        sc = _dot_nt(q[:, lo:hi], kx_ref[0, :, lo:hi])
        p = jnp.exp(sc - jnp.max(sc, axis=-1, keepdims=True))
        l = jnp.sum(p, axis=-1, keepdims=True)
        o_scr[:, lo:hi] = (_dot(p.astype(BF16), vx_ref[0, :, lo:hi]) / l).astype(BF16)
    xa = _dot(o_scr[...], xwo_ref[...])
    h2 = _layer_norm(DEEPNORM_ALPHA * h1 + xa, g2_ref[...], b2_ref[...])
    h2_out[0] = h2
    h2_hi = h2.astype(BF16)
    h2b_out[0] = h2_hi
    h2_lo = (h2 - h2_hi.astype(F32)).astype(BF16)
    hi_terms = _dot(h2_hi, wr_ref[...])
    logits = hi_terms[:, :LANES] + hi_terms[:, LANES:] + _dot(h2_lo, wr_ref[:, :LANES])
    lt = logits.T[0:aff_out.shape[1], :]
    ex = jnp.exp(lt - jnp.max(lt, axis=0, keepdims=True))
    aff_out[0] = ex / jnp.sum(ex, axis=0, keepdims=True)


def _mid(u_pre, attn, x, conv_w, conv_b, conv_g, conv_beta, w_o, g1, b1, wq, kx, vx, xwo, g2, b2, wr_split, tm):
    b, s, d = x.shape
    m = kx.shape[1]
    nh = tm // HALO
    const = lambda bi, i: (0, 0)
    tile = lambda w: pl.BlockSpec((1, tm, w), lambda bi, i: (bi, i, 0))
    whole = lambda arr: pl.BlockSpec(arr.shape, const)
    return pl.pallas_call(
        _mid_kernel,
        grid=(b, s // tm),
        in_specs=[
            tile(CONV_CH),
            pl.BlockSpec((1, HALO, CONV_CH), lambda bi, i: (bi, jnp.maximum(i * nh - 1, 0), 0)),
            pl.BlockSpec((1, HALO, CONV_CH), lambda bi, i: (bi, jnp.minimum((i + 1) * nh, s // HALO - 1), 0)),
            tile(MLA_WIDTH),
            tile(d),
            whole(conv_w), whole(conv_b), whole(conv_g), whole(conv_beta), whole(w_o), whole(g1), whole(b1),
            whole(wq),
            pl.BlockSpec((1, m, d), lambda bi, i: (bi, 0, 0)),
            pl.BlockSpec((1, m, d), lambda bi, i: (bi, 0, 0)),
            whole(xwo), whole(g2), whole(b2), whole(wr_split),
        ],
        out_specs=[
            tile(d),
            tile(d),
            pl.BlockSpec((1, N_EXPERTS, tm), lambda bi, i: (bi, 0, i)),
        ],
        out_shape=[
            jax.ShapeDtypeStruct((b, s, d), F32),
            jax.ShapeDtypeStruct((b, s, d), BF16),
            jax.ShapeDtypeStruct((b, N_EXPERTS, s), F32),
        ],
        scratch_shapes=[
            pltpu.VMEM((tm + 2 * HALO, CONV_CH), F32),
            pltpu.VMEM((SUBLANES, tm + 2 * HALO, CONV_CH), F32),
            pltpu.VMEM((tm, CONV_CH), BF16),
            pltpu.VMEM((tm, d), BF16),
        ],
        compiler_params=_cparams(("parallel", "parallel")),
        name="mid",
    )(u_pre, u_pre, u_pre, attn, x, conv_w, conv_b, conv_g, conv_beta, w_o, g1, b1, wq, kx, vx, xwo, g2, b2, wr_split)


def _split_router(w_router):
    hi = w_router.astype(BF16)
    lo = (w_router - hi.astype(F32)).astype(BF16)
    pad = jnp.zeros((w_router.shape[0], LANES - w_router.shape[1]), BF16)
    return jnp.concatenate([hi, pad, lo, pad], axis=1)


BISECT_STEPS_PER_CHECK = 4
BISECT_MAX_CHECKS = 320


def _topk_kernel(aff_ref, slot_out, slot_t_out, off_out, *, cap):
    aff = aff_ref[...]
    rows, s = aff.shape
    capf = jnp.float32(cap)

    def not_done(carry):
        return jnp.logical_and(carry[2] > 0, carry[3] < BISECT_MAX_CHECKS)

    def halve(_, bounds):
        lo, hi = bounds
        mid = 0.5 * (lo + hi)
        take = jnp.sum(jnp.where(aff >= mid, 1.0, 0.0), axis=1, keepdims=True) >= capf
        return jnp.where(take, mid, lo), jnp.where(take, hi, mid)

    def bisect(carry):
        lo, hi = lax.fori_loop(0, BISECT_STEPS_PER_CHECK, halve, carry[:2])
        smallest_in = jnp.min(jnp.where(aff >= lo, aff, jnp.inf), axis=1, keepdims=True)
        largest_in = jnp.max(jnp.where(aff < hi, aff, -jnp.inf), axis=1, keepdims=True)
        open_rows = jnp.sum(jnp.where(smallest_in == largest_in, 0.0, 1.0))
        return lo, hi, open_rows.astype(jnp.int32), carry[3] + 1

    _, hi, _, _ = lax.while_loop(not_done, bisect, (jnp.zeros((rows, 1), F32), jnp.full((rows, 1), 2.0, F32),
                                                    jnp.int32(1), jnp.int32(0)))
    th = jnp.max(jnp.where(aff < hi, aff, -jnp.inf), axis=1, keepdims=True)
    gt = aff > th
    eq = aff == th
    n_gt = jnp.sum(jnp.where(gt, 1.0, 0.0), axis=1, keepdims=True)
    tri = jnp.where(lax.broadcasted_iota(jnp.int32, (s, s), 0) < lax.broadcasted_iota(jnp.int32, (s, s), 1),
                    1.0, 0.0).astype(BF16)
    tie_rank = _dot(jnp.where(eq, 1.0, 0.0).astype(BF16), tri)
    sel = jnp.logical_or(gt, jnp.logical_and(eq, tie_rank < (capf - n_gt)))
    pos = _dot(jnp.where(sel, 1.0, 0.0).astype(BF16), tri)
    slot = jnp.where(sel, pos, -1.0)
    slot_out[...] = slot
    slot_t_out[...] = slot.T
    lanes = off_out.shape[1]
    before = jnp.where(lax.broadcasted_iota(jnp.int32, (s, lanes), 0)
                       < ROUTE_CHUNK * lax.broadcasted_iota(jnp.int32, (s, lanes), 1), 1.0, 0.0).astype(BF16)
    off_out[...] = _dot(jnp.where(sel, 1.0, 0.0).astype(BF16), before)


def _topk(aff2d, cap):
    rows, s = aff2d.shape
    return pl.pallas_call(
        functools.partial(_topk_kernel, cap=cap),
        out_shape=[
            jax.ShapeDtypeStruct((rows, s), F32),
            jax.ShapeDtypeStruct((s, rows), F32),
            jax.ShapeDtypeStruct((rows, LANES), F32),
        ],
        compiler_params=pltpu.CompilerParams(vmem_limit_bytes=VMEM_LIMIT),
        name="topk",
    )(aff2d)


def _route_tables(off_tab, n_b, n_e, cap):
    n_j = off_tab.shape[1] - 1
    off = off_tab.astype(jnp.int32)
    base = jnp.minimum((off[:, :n_j] // BF16_ROWS) * BF16_ROWS, cap - ROUTE_WIN)
    over = jnp.any((off[:, 1:] - base > ROUTE_WIN).reshape(n_b, n_e, n_j), axis=1)
    return base, over.astype(jnp.int32)


def _dispatch_kernel(base_ref, over_ref, h2b_ref, slot_ref, aff_ref, xs_out, gate_out, p_scr, gate_scr, *, cap):
    bi = pl.program_id(0)
    jo = pl.program_id(1)
    n_j = pl.num_programs(1) * ROUTE_INNER
    n_e, _, t = slot_ref.shape

    @pl.when(jo == 0)
    def _():
        xs_out[...] = jnp.zeros(xs_out.shape, xs_out.dtype)
        gate_scr[...] = jnp.zeros(gate_scr.shape, gate_scr.dtype)

    def chunk(ji, carry):
        j = jo * ROUTE_INNER + ji
        h = h2b_ref[0, pl.ds(pl.multiple_of(ji * t, t), t), :]
        slot_row = lambda e: slot_ref[e, pl.ds(j, 1), :]
        aff_row = lambda e: aff_ref[e, pl.ds(j, 1), :]

        @pl.when(over_ref[bi * n_j + j] == 0)
        def _():
            w_iota = lax.broadcasted_iota(jnp.int32, (ROUTE_WIN, t), 0).astype(F32)
            bases = []
            for e in range(n_e):
                base = pl.multiple_of(base_ref[(bi * n_e + e) * n_j + j], BF16_ROWS)
                hit = slot_row(e) == w_iota + base.astype(F32)
                p_scr[e * ROUTE_WIN:(e + 1) * ROUTE_WIN, :] = jnp.where(hit, 1.0, 0.0).astype(BF16)
                gate_scr[e, pl.ds(base, ROUTE_WIN), :] += jnp.sum(jnp.where(hit, aff_row(e), 0.0),
                                                                  axis=1, keepdims=True)
                bases.append(base)
            picked = _dot(p_scr[...], h)
            for e in range(n_e):
                rows = pl.ds(bases[e], ROUTE_WIN)
                xs_out[e, 0, rows, :] += picked[e * ROUTE_WIN:(e + 1) * ROUTE_WIN, :].astype(BF16)

        @pl.when(over_ref[bi * n_j + j] != 0)
        def _():
            c_iota = lax.broadcasted_iota(jnp.int32, (cap, t), 0).astype(F32)
            for e in range(n_e):
                hit = slot_row(e) == c_iota
                xs_out[e, 0] += _dot(jnp.where(hit, 1.0, 0.0).astype(BF16), h).astype(BF16)
                gate_scr[e] += jnp.sum(jnp.where(hit, aff_row(e), 0.0), axis=1, keepdims=True)

        return carry

    lax.fori_loop(0, ROUTE_INNER, chunk, 0)

    @pl.when(jo == pl.num_programs(1) - 1)
    def _():
        eye = lax.broadcasted_iota(jnp.int32, (cap, cap), 0) == lax.broadcasted_iota(jnp.int32, (cap, cap), 1)
        for e in range(n_e):
            gate_out[e, 0] = jnp.sum(jnp.where(eye, gate_scr[e], 0.0), axis=0, keepdims=True)


def _dispatch(base_flat, over_flat, h2b, slot3d, aff3d, n_e, cap):
    b, s, d = h2b.shape
    t = ROUTE_CHUNK
    rows = ROUTE_INNER * t
    grid_spec = pltpu.PrefetchScalarGridSpec(
        num_scalar_prefetch=2,
        grid=(b, s // rows),
        in_specs=[
            pl.BlockSpec((1, rows, d), lambda bi, jo, *_: (bi, jo, 0)),
            pl.BlockSpec((n_e, s // t, t), lambda bi, jo, *_: (bi, 0, 0)),
            pl.BlockSpec((n_e, s // t, t), lambda bi, jo, *_: (bi, 0, 0)),
        ],
        out_specs=[
            pl.BlockSpec((n_e, 1, cap, d), lambda bi, jo, *_: (0, bi, 0, 0)),
            pl.BlockSpec((n_e, 1, 1, cap), lambda bi, jo, *_: (0, bi, 0, 0)),
        ],
        scratch_shapes=[pltpu.VMEM((n_e * ROUTE_WIN, t), BF16), pltpu.VMEM((n_e, cap, 1), F32)],
    )
    return pl.pallas_call(
        functools.partial(_dispatch_kernel, cap=cap),
        grid_spec=grid_spec,
        out_shape=[
            jax.ShapeDtypeStruct((n_e, b, cap, d), BF16),
            jax.ShapeDtypeStruct((n_e, b, 1, cap), F32),
        ],
        compiler_params=_cparams(("parallel", "arbitrary")),
        name="dispatch",
    )(base_flat, over_flat, h2b, slot3d, aff3d)


FFN_ROWS = 512
FFN_CHUNK = 512


def _experts_kernel(xs_ref, gate_ref, wg_ref, wu_ref, wd_ref, y_out, wg_s, wu_s, wd_s, hid_s):
    e = pl.program_id(0)
    f = pl.program_id(1)
    n_chunks = wg_s.shape[1]

    def stage():
        slot = e % 2
        wg_s[slot, f] = wg_ref[0].astype(BF16)
        wu_s[slot, f] = wu_ref[0].astype(BF16)
        wd_s[slot, f] = wd_ref[0].astype(BF16)

    @pl.when(e == 0)
    def _():
        stage()
        y_out[0] = jnp.zeros(y_out.shape[1:], y_out.dtype)

    @pl.when(e > 0)
    def _():
        stage()
        slot = (e - 1) % 2
        xs = xs_ref[0]
        for c in range(n_chunks):
            g = _dot(xs, wg_s[slot, c])
            u = _dot(xs, wu_s[slot, c])
            hid_s[:, c * FFN_CHUNK:(c + 1) * FFN_CHUNK] = (g * _sigmoid(g) * u).astype(BF16)
        y = _dot(hid_s[:, 0:FFN_CHUNK], wd_s[slot, 0])
        for c in range(1, n_chunks):
            y = y + _dot(hid_s[:, c * FFN_CHUNK:(c + 1) * FFN_CHUNK], wd_s[slot, c])
        rows = xs.shape[0]
        eye = lax.broadcasted_iota(jnp.int32, (rows, rows), 0) == lax.broadcasted_iota(jnp.int32, (rows, rows), 1)
        gate_col = jnp.sum(jnp.where(eye, gate_ref[0], 0.0), axis=1, keepdims=True)
        y_out[0] = (y * gate_col).astype(BF16)


def _experts(xs, gate, w_gate, w_up, w_down):
    e, n, d = xs.shape
    ff = w_gate.shape[2]
    n_chunks = ff // FFN_CHUNK
    assert n // FFN_ROWS == n_chunks
    prev = lambda ei, fi: (jnp.maximum(ei - 1, 0), fi, 0)
    cur = lambda ei: jnp.minimum(ei, e - 1)
    return pl.pallas_call(
        _experts_kernel,
        grid=(e + 1, n_chunks),
        in_specs=[
            pl.BlockSpec((1, FFN_ROWS, d), prev),
            pl.BlockSpec((1, 1, FFN_ROWS), lambda ei, fi: (jnp.maximum(ei - 1, 0), 0, fi)),
            pl.BlockSpec((1, d, FFN_CHUNK), lambda ei, fi: (cur(ei), 0, fi)),
            pl.BlockSpec((1, d, FFN_CHUNK), lambda ei, fi: (cur(ei), 0, fi)),
            pl.BlockSpec((1, FFN_CHUNK, d), lambda ei, fi: (cur(ei), fi, 0)),
        ],
        out_specs=pl.BlockSpec((1, FFN_ROWS, d), lambda ei, fi: (jnp.where(ei == 0, e, ei - 1), fi, 0)),
        out_shape=jax.ShapeDtypeStruct((e + 1, n, d), BF16),
        scratch_shapes=[
            pltpu.VMEM((2, n_chunks, d, FFN_CHUNK), BF16),
            pltpu.VMEM((2, n_chunks, d, FFN_CHUNK), BF16),
            pltpu.VMEM((2, n_chunks, FFN_CHUNK, d), BF16),
            pltpu.VMEM((FFN_ROWS, ff), BF16),
        ],
        compiler_params=_cparams(("arbitrary", "arbitrary")),
        name="experts",
    )(xs, gate, w_gate, w_up, w_down)


def _combine_kernel(base_ref, over_ref, slot_t_ref, base_t_ref, y_hbm, h2_ref, g3_ref, b3_ref, out_ref,
                    ywin_scr, ydense_scr, win_sem, dense_sem, *, cap):
    bi = pl.program_id(0)
    jo = pl.program_id(1)
    n_j = pl.num_programs(1) * ROUTE_INNER
    n_chunks = pl.num_programs(0) * n_j
    n_e = ydense_scr.shape[0]
    rows = slot_t_ref.shape[1]
    t = ROUTE_CHUNK

    def window_copy(b, j, e, half):
        base = pl.multiple_of(base_ref[(b * n_e + e) * n_j + j], BF16_ROWS)
        return pltpu.make_async_copy(y_hbm.at[e, b, pl.ds(base, ROUTE_WIN), :],
                                     ywin_scr.at[half, pl.ds(e * ROUTE_WIN, ROUTE_WIN), :], win_sem.at[half])

    def start_windows(g, half):
        for e in range(n_e):
            window_copy(g // n_j, g % n_j, e, half).start()

    def dense_copy(e):
        return pltpu.make_async_copy(y_hbm.at[e, bi], ydense_scr.at[e], dense_sem.at[0])

    def spread_cols(per_expert):
        width = n_e * per_expert
        col = lax.broadcasted_iota(jnp.int32, (rows, width), 1) // per_expert
        return jnp.where(lax.broadcasted_iota(jnp.int32, (rows, width), 0) == bi * n_e + col, 1.0, 0.0).astype(BF16)

    def lane_in_group(per_expert):
        return (lax.broadcasted_iota(jnp.int32, (1, n_e * per_expert), 1) % per_expert).astype(F32)

    win_spread = spread_cols(ROUTE_WIN)
    win_lane = lane_in_group(ROUTE_WIN)

    def chunk(ji, carry):
        j = jo * ROUTE_INNER + ji
        g = bi * n_j + j
        half = lax.rem(g, 2)
        tok = pl.ds(pl.multiple_of(ji * t, t), t)

        @pl.when(g == 0)
        def _():
            start_windows(g, half)

        @pl.when(g + 1 < n_chunks)
        def _():
            start_windows(g + 1, 1 - half)

        for e in range(n_e):
            window_copy(bi, j, e, half).wait()

        def finish(ff):
            out_ref[0, tok, :] = _layer_norm(DEEPNORM_ALPHA * h2_ref[0, tok, :] + ff, g3_ref[...], b3_ref[...])

        slot_bf = slot_t_ref[tok, :].astype(BF16)

        @pl.when(over_ref[bi * n_j + j] == 0)
        def _():
            slot_wide = _dot(slot_bf, win_spread)
            base_wide = _dot(jnp.broadcast_to(base_t_ref[ji], (SUBLANES, rows)).astype(BF16), win_spread)[0:1, :]
            onehot = jnp.where(slot_wide - base_wide == win_lane, 1.0, 0.0).astype(BF16)
            finish(_dot(onehot, ywin_scr[half]))

        @pl.when(over_ref[bi * n_j + j] != 0)
        def _():
            for e in range(n_e):
                dense_copy(e).start()
            for e in range(n_e):
                dense_copy(e).wait()
            slot_wide = _dot(slot_bf, spread_cols(cap))
            onehot = jnp.where(slot_wide == lane_in_group(cap), 1.0, 0.0).astype(BF16)
            finish(_dot(onehot, ydense_scr[...].reshape(n_e * cap, ydense_scr.shape[2])))

        return carry

    lax.fori_loop(0, ROUTE_INNER, chunk, 0)


def _combine(base_flat, over_flat, slot_t, base_t, y4, h2, g3, b3, cap):
    b, s, d = h2.shape
    e = y4.shape[0] - 1
    rows = ROUTE_INNER * ROUTE_CHUNK
    const = lambda bi, jo, *_: (0, 0)
    grid_spec = pltpu.PrefetchScalarGridSpec(
        num_scalar_prefetch=2,
        grid=(b, s // rows),
        in_specs=[
            pl.BlockSpec((rows, slot_t.shape[1]), lambda bi, jo, *_: (jo, 0)),
            pl.BlockSpec((ROUTE_INNER, 1, base_t.shape[2]), lambda bi, jo, *_: (jo, 0, 0)),
            pl.BlockSpec(memory_space=pl.ANY),
            pl.BlockSpec((1, rows, d), lambda bi, jo, *_: (bi, jo, 0)),
            pl.BlockSpec(g3.shape, const),
            pl.BlockSpec(b3.shape, const),
        ],
        out_specs=pl.BlockSpec((1, rows, d), lambda bi, jo, *_: (bi, jo, 0)),
        scratch_shapes=[
            pltpu.VMEM((2, e * ROUTE_WIN, d), BF16),
            pltpu.VMEM((e, cap, d), BF16),
            pltpu.SemaphoreType.DMA((2,)),
            pltpu.SemaphoreType.DMA((1,)),
        ],
    )
    return pl.pallas_call(
        functools.partial(_combine_kernel, cap=cap),
        grid_spec=grid_spec,
        out_shape=jax.ShapeDtypeStruct((b, s, d), F32),
        compiler_params=_cparams(("arbitrary", "arbitrary")),
        name="combine",
    )(base_flat, over_flat, slot_t, base_t, y4, h2, g3, b3)


def _extend_weights(w_in, w_uq, w_uk, w_uv):
    half = MLA_ROPE_DIM // 2
    d = w_in.shape[0]
    w_t = w_in.T.astype(BF16)
    kr = w_t[SPLIT_KV:SPLIT_KR]
    z = lambda n: jnp.zeros((n, d), BF16)
    assert HEAD_PAD - MLA_QK_DIM == MLA_ROPE_DIM
    kr_tile = jnp.concatenate([z(MLA_NOPE_DIM), kr, -kr[half:], kr[:half]], axis=0)
    w_in_ext = jnp.concatenate([w_t[:SPLIT_KV], kr_tile, w_t[SPLIT_KR:]], axis=0).T

    wq = w_uq.reshape(MLA_Q_RANK, MLA_HEADS, MLA_QK_DIM)
    wq_ext = jnp.concatenate([wq, -wq[:, :, MLA_NOPE_DIM + half:], wq[:, :, MLA_NOPE_DIM:MLA_NOPE_DIM + half]],
                             axis=2).reshape(MLA_Q_RANK, QK_PAD)

    wk = w_uk.reshape(MLA_KV_RANK, MLA_HEADS, MLA_NOPE_DIM)
    wk_full = jnp.concatenate([wk, jnp.zeros((MLA_KV_RANK, MLA_HEADS, HEAD_PAD - MLA_NOPE_DIM), w_uk.dtype)], axis=2)
    wv = w_uv.reshape(MLA_KV_RANK, MLA_HEADS, MLA_V_DIM)
    wv_full = jnp.concatenate([wv, jnp.zeros((MLA_KV_RANK, MLA_HEADS, HEAD_PAD - MLA_V_DIM), w_uv.dtype)], axis=2)
    wkv_ext = jnp.concatenate([wk_full.reshape(MLA_KV_RANK, QK_PAD), wv_full.reshape(MLA_KV_RANK, QK_PAD)], axis=1)
    return w_in_ext.astype(BF16), wq_ext.astype(BF16), wkv_ext.astype(BF16)


def _rope_freq_column():
    inv_freq = ROPE_BASE ** (-jnp.arange(0, MLA_ROPE_DIM, 2, dtype=F32) / MLA_ROPE_DIM)
    return inv_freq[:, None]


def _value_one_lanes():
    one_hot = (jnp.arange(HEAD_PAD) == MLA_V_DIM).astype(F32)
    return jnp.tile(one_hot, MLA_HEADS)[None, :]


def kernel(x, mem, positions, w_in, q_norm_g, w_uq, kv_norm_g, w_uk, w_uv, conv_w, conv_b, conv_ln_g, conv_ln_b,
           w_o, ln1_g, ln1_b, xa_w_q, xa_w_k, xa_w_v, xa_w_o, ln2_g, ln2_b, w_router, w_gate, w_up, w_down,
           ln3_g, ln3_b):
    assert w_in.shape[0] == DEPTH == 1
    b, s, d = x.shape
    cap = CAPACITY_FACTOR * s // N_EXPERTS
    assert s % max(FRONT_ROWS, ATTN_ROWS, MID_ROWS, ROUTE_INNER * ROUTE_CHUNK) == 0 and cap >= ROUTE_WIN
    assert mem.shape == (b, MEM_LEN, d) and d == D_MODEL

    w_in_ext, wq_ext, wkv_ext = _extend_weights(w_in[0], w_uq[0], w_uk[0], w_uv[0])
    posf = positions.astype(F32)[:, None, :]
    q, k, v, u_pre = _front(x, posf, w_in_ext, q_norm_g, wq_ext, kv_norm_g, wkv_ext, _rope_freq_column(),
                             _value_one_lanes(), FRONT_ROWS)
    attn = _attention(q, k, v, ATTN_ROWS)
    kx, vx = _mem_kv(mem, xa_w_k[0].astype(BF16), xa_w_v[0].astype(BF16))
    h2, h2b, aff = _mid(u_pre, attn, x, conv_w[0], conv_b, conv_ln_g, conv_ln_b, w_o[0].astype(BF16), ln1_g, ln1_b,
                        (xa_w_q[0] * MEM_Q_SCALE).astype(BF16), kx, vx, xa_w_o[0].astype(BF16), ln2_g, ln2_b,
                        _split_router(w_router[0]), MID_ROWS)

    aff2d = aff.reshape(b * N_EXPERTS, s)
    slot, slot_t, off_tab = _topk(aff2d, cap)
    n_j = s // ROUTE_CHUNK
    base, over = _route_tables(off_tab[:, :n_j + 1], b, N_EXPERTS, cap)
    base_flat, over_flat = base.reshape(-1), over.reshape(-1)
    by_chunk = lambda a: a.reshape(b * N_EXPERTS, n_j, ROUTE_CHUNK)
    xs, gate = _dispatch(base_flat, over_flat, h2b, by_chunk(slot), by_chunk(aff2d), N_EXPERTS, cap)
    y = _experts(xs.reshape(N_EXPERTS, b * cap, d), gate.reshape(N_EXPERTS, 1, b * cap),
                 w_gate[0], w_up[0], w_down[0])
    base_t = base.T.astype(F32).reshape(n_j, 1, b * N_EXPERTS)
    return _combine(base_flat, over_flat, slot_t, base_t, y.reshape(N_EXPERTS + 1, b, cap, d), h2, ln3_g, ln3_b, cap)
```

```python
import functools
import math

import jax
import jax.numpy as jnp
from jax import lax
from jax.experimental import pallas as pl
from jax.experimental.pallas import tpu as pltpu

F32 = jnp.float32
BF16 = jnp.bfloat16

D_MODEL = 1024
MLA_HEADS = 8
MLA_NOPE_DIM = 64
MLA_ROPE_DIM = 32
MLA_QK_DIM = MLA_NOPE_DIM + MLA_ROPE_DIM
MLA_V_DIM = 64
MLA_Q_RANK = 256
MLA_KV_RANK = 128
MLA_WIDTH = MLA_HEADS * MLA_V_DIM
CONV_CH = D_MODEL - MLA_WIDTH
CONV_WIDTH = 31
CONV_PAD = (CONV_WIDTH - 1) // 2
ROPE_BASE = 10000.0
MEM_HEADS = 4
MEM_HEAD_DIM = D_MODEL // MEM_HEADS
MEM_Q_SCALE = 1.0 / math.sqrt(MEM_HEAD_DIM)
assert math.frexp(MEM_Q_SCALE)[0] == 0.5
N_EXPERTS = 16
EXPERT_FF = 2048
CAPACITY_FACTOR = 2
NORM_EPS = 1e-5
DEPTH = 1
DEEPNORM_ALPHA = (2.0 * DEPTH) ** 0.25
SPLIT_Q = MLA_Q_RANK
SPLIT_KV = SPLIT_Q + MLA_KV_RANK
SPLIT_KR = SPLIT_KV + MLA_ROPE_DIM

HEAD_PAD = 128
QK_PAD = MLA_HEADS * HEAD_PAD
OFF_CQ = 0
OFF_CKV = OFF_CQ + MLA_Q_RANK
OFF_KR = OFF_CKV + MLA_KV_RANK
OFF_A = OFF_KR + HEAD_PAD
OFF_G = OFF_A + CONV_CH
IN_EXT = OFF_G + CONV_CH

SUBLANES = 8
LANES = 128
BF16_ROWS = 16
ROUTE_CHUNK = 256
ROUTE_WIN = 64
ROUTE_INNER = 4
HALO = 16
VMEM_LIMIT = 56 * 1024 * 1024
FRONT_ROWS = 1024
ATTN_ROWS = 1024
MID_ROWS = 1024
MEM_LEN = 256


def _cparams(sem):
    return pltpu.CompilerParams(dimension_semantics=sem, vmem_limit_bytes=VMEM_LIMIT)


def _layer_norm(v, g, b):
    mu = jnp.mean(v, axis=-1, keepdims=True)
    d = v - mu
    var = jnp.mean(d * d, axis=-1, keepdims=True)
    return d * lax.rsqrt(var + NORM_EPS) * g + b


def _rms_norm(v, g):
    return v * lax.rsqrt(jnp.mean(v * v, axis=-1, keepdims=True) + NORM_EPS) * g


def _sigmoid(v):
    return 1.0 / (1.0 + jnp.exp(-v))


def _dot(a, b):
    return jnp.dot(a, b, preferred_element_type=F32)


def _dot_nt(a, b, precision=None):
    return lax.dot_general(a, b, (((1,), (1,)), ((), ())), preferred_element_type=F32, precision=precision)


def _front_kernel(x_ref, pos_ref, win_ref, qg_ref, wq_ref, kvg_ref, wkv_ref, invf_ref, vone_ref,
                  q_out, k_out, v_out, u_out):
    x = x_ref[0].astype(BF16)
    hc = _dot(x, win_ref[...])
    cqn = _rms_norm(hc[:, OFF_CQ:OFF_CKV], qg_ref[...])
    qq = _dot(cqn.astype(BF16), wq_ref[...])
    ckvn = _rms_norm(hc[:, OFF_CKV:OFF_KR], kvg_ref[...])
    kv = _dot(ckvn.astype(BF16), wkv_ref[...])
    tm = x_ref.shape[1]
    ang = invf_ref[...] * pos_ref[0]
    cos_h, sin_h = jnp.cos(ang), jnp.sin(ang)
    ones = lambda n: jnp.ones((n, tm), F32)
    zeros = lambda n: jnp.zeros((n, tm), F32)
    cos = jnp.concatenate([ones(MLA_NOPE_DIM), cos_h, cos_h, ones(HEAD_PAD - MLA_QK_DIM)], axis=0).T
    sin = jnp.concatenate([zeros(MLA_NOPE_DIM), sin_h, sin_h, zeros(HEAD_PAD - MLA_QK_DIM)], axis=0).T
    kr = hc[:, OFF_KR:OFF_A]
    kr_partner = pltpu.roll(kr, HEAD_PAD - MLA_ROPE_DIM, axis=1)
    rope_lanes = lax.broadcasted_iota(jnp.int32, (1, HEAD_PAD), 1) < MLA_QK_DIM
    krot = jnp.where(rope_lanes, kr * cos + kr_partner * sin, 0.0)
    scale = math.log2(math.e) / math.sqrt(MLA_QK_DIM)
    for h in range(MLA_HEADS):
        lo, hi = h * HEAD_PAD, (h + 1) * HEAD_PAD
        qh = (qq[:, lo:hi] * cos + pltpu.roll(qq[:, lo:hi], HEAD_PAD - MLA_ROPE_DIM, axis=1) * sin) * scale
        q_out[0, :, lo:hi] = qh.astype(BF16)
        k_out[0, :, lo:hi] = (kv[:, lo:hi] + krot).astype(BF16)
    v_out[0] = (kv[:, QK_PAD:] + vone_ref[...]).astype(BF16)
    u_out[0] = hc[:, OFF_A:OFF_G] * _sigmoid(hc[:, OFF_G:IN_EXT])


def _front(x, posf, w_in_ext, qg, wq_ext, kvg, wkv_ext, invf, vone, tm):
    b, s, d = x.shape
    const = lambda bi, i: (0, 0)
    return pl.pallas_call(
        _front_kernel,
        grid=(b, s // tm),
        in_specs=[
            pl.BlockSpec((1, tm, d), lambda bi, i: (bi, i, 0)),
            pl.BlockSpec((1, 1, tm), lambda bi, i: (bi, 0, i)),
            pl.BlockSpec(w_in_ext.shape, const),
            pl.BlockSpec(qg.shape, const),
            pl.BlockSpec(wq_ext.shape, const),
            pl.BlockSpec(kvg.shape, const),
            pl.BlockSpec(wkv_ext.shape, const),
            pl.BlockSpec(invf.shape, const),
            pl.BlockSpec(vone.shape, const),
        ],
        out_specs=[
            pl.BlockSpec((1, tm, QK_PAD), lambda bi, i: (bi, i, 0)),
            pl.BlockSpec((1, tm, QK_PAD), lambda bi, i: (bi, i, 0)),
            pl.BlockSpec((1, tm, QK_PAD), lambda bi, i: (bi, i, 0)),
            pl.BlockSpec((1, tm, CONV_CH), lambda bi, i: (bi, i, 0)),
        ],
        out_shape=[
            jax.ShapeDtypeStruct((b, s, QK_PAD), BF16),
            jax.ShapeDtypeStruct((b, s, QK_PAD), BF16),
            jax.ShapeDtypeStruct((b, s, QK_PAD), BF16),
            jax.ShapeDtypeStruct((b, s, CONV_CH), F32),
        ],
        compiler_params=_cparams(("parallel", "parallel")),
        name="front",
    )(x, posf, w_in_ext, qg, wq_ext, kvg, wkv_ext, invf, vone)


def _attn_kernel(q_ref, k_ref, v_ref, o_ref):
    for h in range(MLA_HEADS):
        lo, hi = h * HEAD_PAD, (h + 1) * HEAD_PAD
        sc = _dot_nt(q_ref[0, :, lo:hi], k_ref[0, :, lo:hi])
        p = jnp.exp2(sc - jnp.max(sc, axis=-1, keepdims=True))
        pv = _dot(p.astype(BF16), v_ref[0, :, lo:hi])
        o = pv[:, :MLA_V_DIM] / pv[:, MLA_V_DIM:MLA_V_DIM + 1]
        o_ref[0, :, h * MLA_V_DIM:(h + 1) * MLA_V_DIM] = o.astype(BF16)


def _attention(q, k, v, tq):
    b, s, _ = q.shape
    return pl.pallas_call(
        _attn_kernel,
        grid=(b, s // tq),
        in_specs=[
            pl.BlockSpec((1, tq, QK_PAD), lambda bi, i: (bi, i, 0)),
            pl.BlockSpec((1, s, QK_PAD), lambda bi, i: (bi, 0, 0)),
            pl.BlockSpec((1, s, QK_PAD), lambda bi, i: (bi, 0, 0)),
        ],
        out_specs=pl.BlockSpec((1, tq, MLA_WIDTH), lambda bi, i: (bi, i, 0)),
        out_shape=jax.ShapeDtypeStruct((b, s, MLA_WIDTH), BF16),
        compiler_params=_cparams(("parallel", "parallel")),
        name="attn",
    )(q, k, v)


def _mem_kv_kernel(mem_ref, wk_ref, wv_ref, k_out, v_out):
    m = mem_ref[0].astype(BF16)
    k_out[0] = _dot(m, wk_ref[...]).astype(BF16)
    v_out[0] = _dot(m, wv_ref[...]).astype(BF16)


def _mem_kv(mem, wk, wv):
    b, m, d = mem.shape
    const = lambda bi: (0, 0)
    return pl.pallas_call(
        _mem_kv_kernel,
        grid=(b,),
        in_specs=[
            pl.BlockSpec((1, m, d), lambda bi: (bi, 0, 0)),
            pl.BlockSpec(wk.shape, const),
            pl.BlockSpec(wv.shape, const),
        ],
        out_specs=[pl.BlockSpec((1, m, d), lambda bi: (bi, 0, 0))] * 2,
        out_shape=[jax.ShapeDtypeStruct((b, m, d), BF16)] * 2,
        compiler_params=_cparams(("parallel",)),
        name="mem_kv",
    )(mem, wk, wv)


CONV_SUB = 64
CONV_TILE = 512


def _mid_kernel(ucur_ref, uprev_ref, unext_ref, attn_ref, x_ref, cw_ref, cb_ref, cg_ref, cbeta_ref, wo_ref,
                g1_ref, b1_ref, wq_ref, kx_ref, vx_ref, xwo_ref, g2_ref, b2_ref, wr_ref,
                h2_out, h2b_out, aff_out, win_ref, shift_ref, u_scr, o_scr):
    tm = ucur_ref.shape[1]
    i = pl.program_id(1)
    last = pl.num_programs(1) - 1
    win_ref[0:HALO, :] = jnp.where(i > 0, uprev_ref[0], 0.0)
    win_ref[HALO:HALO + tm, :] = ucur_ref[0]
    win_ref[HALO + tm:2 * HALO + tm, :] = jnp.where(i < last, unext_ref[0], 0.0)
    span = CONV_TILE + 2 * HALO - SUBLANES
    for c in range(tm // CONV_TILE):
        base = c * CONV_TILE
        shift_ref[0] = win_ref[base:base + CONV_TILE + 2 * HALO, :]
        for j in range(1, SUBLANES):
            shift_ref[j, 0:span, :] = win_ref[base + j:base + j + span, :]
        for r in range(CONV_TILE // CONV_SUB):
            acc = jnp.broadcast_to(cb_ref[...], (CONV_SUB, CONV_CH))
            for t in range(CONV_WIDTH):
                off = HALO - CONV_PAD + t
                row = r * CONV_SUB + (off // SUBLANES) * SUBLANES
                acc = acc + shift_ref[off % SUBLANES, row:row + CONV_SUB, :] * cw_ref[t:t + 1, :]
            y = _layer_norm(acc, cg_ref[...], cbeta_ref[...])
            u_scr[base + r * CONV_SUB:base + (r + 1) * CONV_SUB, :] = (y * _sigmoid(y)).astype(BF16)
    mix = _dot(attn_ref[0], wo_ref[0:MLA_WIDTH, :]) + _dot(u_scr[...], wo_ref[MLA_WIDTH:, :])
    h1 = _layer_norm(DEEPNORM_ALPHA * x_ref[0] + mix, g1_ref[...], b1_ref[...])
    q = _dot(h1.astype(BF16), wq_ref[...]).astype(BF16)
    for h in range(MEM_HEADS):
        lo, hi = h * MEM_HEAD_DIM, (h + 1) * MEM_HEAD_DIM
        sc = _dot_nt(q[:, lo:hi], kx_ref[0, :, lo:hi])
        p = jnp.exp(sc - jnp.max(sc, axis=-1, keepdims=True))
        l = jnp.sum(p, axis=-1, keepdims=True)
        o_scr[:, lo:hi] = (_dot(p.astype(BF16), vx_ref[0, :, lo:hi]) / l).astype(BF16)
    xa = _dot(o_scr[...], xwo_ref[...])
    h2 = _layer_norm(DEEPNORM_ALPHA * h1 + xa, g2_ref[...], b2_ref[...])
    h2_out[0] = h2
    h2_hi = h2.astype(BF16)
    h2b_out[0] = h2_hi
    h2_lo = (h2 - h2_hi.astype(F32)).astype(BF16)
    hi_terms = _dot(h2_hi, wr_ref[...])
    logits = hi_terms[:, :LANES] + hi_terms[:, LANES:] + _dot(h2_lo, wr_ref[:, :LANES])
    lt = logits.T[0:aff_out.shape[1], :]
    ex = jnp.exp(lt - jnp.max(lt, axis=0, keepdims=True))
    aff_out[0] = ex / jnp.sum(ex, axis=0, keepdims=True)


def _mid(u_pre, attn, x, conv_w, conv_b, conv_g, conv_beta, w_o, g1, b1, wq, kx, vx, xwo, g2, b2, wr_split, tm):
    b, s, d = x.shape
    m = kx.shape[1]
    nh = tm // HALO
    const = lambda bi, i: (0, 0)
    tile = lambda w: pl.BlockSpec((1, tm, w), lambda bi, i: (bi, i, 0))
    whole = lambda arr: pl.BlockSpec(arr.shape, const)
    once = lambda arr: pl.BlockSpec(arr.shape, const, pipeline_mode=pl.Buffered(1))
    return pl.pallas_call(
        _mid_kernel,
        grid=(b, s // tm),
        in_specs=[
            tile(CONV_CH),
            pl.BlockSpec((1, HALO, CONV_CH), lambda bi, i: (bi, jnp.maximum(i * nh - 1, 0), 0)),
            pl.BlockSpec((1, HALO, CONV_CH), lambda bi, i: (bi, jnp.minimum((i + 1) * nh, s // HALO - 1), 0)),
            tile(MLA_WIDTH),
            tile(d),
            whole(conv_w), whole(conv_b), whole(conv_g), whole(conv_beta), once(w_o), whole(g1), whole(b1),
            once(wq),
            pl.BlockSpec((1, m, d), lambda bi, i: (bi, 0, 0)),
            pl.BlockSpec((1, m, d), lambda bi, i: (bi, 0, 0)),
            once(xwo), whole(g2), whole(b2), whole(wr_split),
        ],
        out_specs=[
            tile(d),
            tile(d),
            pl.BlockSpec((1, N_EXPERTS, tm), lambda bi, i: (bi, 0, i)),
        ],
        out_shape=[
            jax.ShapeDtypeStruct((b, s, d), F32),
            jax.ShapeDtypeStruct((b, s, d), BF16),
            jax.ShapeDtypeStruct((b, N_EXPERTS, s), F32),
        ],
        scratch_shapes=[
            pltpu.VMEM((tm + 2 * HALO, CONV_CH), F32),
            pltpu.VMEM((SUBLANES, CONV_TILE + 2 * HALO, CONV_CH), F32),
            pltpu.VMEM((tm, CONV_CH), BF16),
            pltpu.VMEM((tm, d), BF16),
        ],
        compiler_params=_cparams(("parallel", "parallel")),
        name="mid",
    )(u_pre, u_pre, u_pre, attn, x, conv_w, conv_b, conv_g, conv_beta, w_o, g1, b1, wq, kx, vx, xwo, g2, b2, wr_split)


def _split_router(w_router):
    hi = w_router.astype(BF16)
    lo = (w_router - hi.astype(F32)).astype(BF16)
    pad = jnp.zeros((w_router.shape[0], LANES - w_router.shape[1]), BF16)
    return jnp.concatenate([hi, pad, lo, pad], axis=1)


BISECT_STEPS_PER_CHECK = 4
BISECT_MAX_CHECKS = 320


def _topk_kernel(aff_ref, slot_out, slot_t_out, off_out, *, cap):
    aff = aff_ref[...]
    rows, s = aff.shape
    capf = jnp.float32(cap)

    def not_done(carry):
        return jnp.logical_and(carry[2] > 0, carry[3] < BISECT_MAX_CHECKS)

    def halve(_, bounds):
        lo, hi = bounds
        mid = 0.5 * (lo + hi)
        take = jnp.sum(jnp.where(aff >= mid, 1.0, 0.0), axis=1, keepdims=True) >= capf
        return jnp.where(take, mid, lo), jnp.where(take, hi, mid)

    def bisect(carry):
        lo, hi = lax.fori_loop(0, BISECT_STEPS_PER_CHECK, halve, carry[:2])
        smallest_in = jnp.min(jnp.where(aff >= lo, aff, jnp.inf), axis=1, keepdims=True)
        largest_in = jnp.max(jnp.where(aff < hi, aff, -jnp.inf), axis=1, keepdims=True)
        open_rows = jnp.sum(jnp.where(smallest_in == largest_in, 0.0, 1.0))
        return lo, hi, open_rows.astype(jnp.int32), carry[3] + 1

    _, hi, _, _ = lax.while_loop(not_done, bisect, (jnp.zeros((rows, 1), F32), jnp.full((rows, 1), 2.0, F32),
                                                    jnp.int32(1), jnp.int32(0)))
    th = jnp.max(jnp.where(aff < hi, aff, -jnp.inf), axis=1, keepdims=True)
    gt = aff > th
    eq = aff == th
    n_gt = jnp.sum(jnp.where(gt, 1.0, 0.0), axis=1, keepdims=True)
    tri = jnp.where(lax.broadcasted_iota(jnp.int32, (s, s), 0) < lax.broadcasted_iota(jnp.int32, (s, s), 1),
                    1.0, 0.0).astype(BF16)
    tie_rank = _dot(jnp.where(eq, 1.0, 0.0).astype(BF16), tri)
    sel = jnp.logical_or(gt, jnp.logical_and(eq, tie_rank < (capf - n_gt)))
    pos = _dot(jnp.where(sel, 1.0, 0.0).astype(BF16), tri)
    slot = jnp.where(sel, pos, -1.0)
    slot_out[...] = slot
    slot_t_out[...] = slot.T
    lanes = off_out.shape[1]
    before = jnp.where(lax.broadcasted_iota(jnp.int32, (s, lanes), 0)
                       < ROUTE_CHUNK * lax.broadcasted_iota(jnp.int32, (s, lanes), 1), 1.0, 0.0).astype(BF16)
    off_out[...] = _dot(jnp.where(sel, 1.0, 0.0).astype(BF16), before)


def _topk(aff2d, cap):
    rows, s = aff2d.shape
    return pl.pallas_call(
        functools.partial(_topk_kernel, cap=cap),
        out_shape=[
            jax.ShapeDtypeStruct((rows, s), F32),
            jax.ShapeDtypeStruct((s, rows), F32),
            jax.ShapeDtypeStruct((rows, LANES), F32),
        ],
        compiler_params=pltpu.CompilerParams(vmem_limit_bytes=VMEM_LIMIT),
        name="topk",
    )(aff2d)


def _route_tables(off_tab, n_b, n_e, cap):
    n_j = off_tab.shape[1] - 1
    off = off_tab.astype(jnp.int32)
    base = jnp.minimum((off[:, :n_j] // BF16_ROWS) * BF16_ROWS, cap - ROUTE_WIN)
    over = jnp.any((off[:, 1:] - base > ROUTE_WIN).reshape(n_b, n_e, n_j), axis=1)
    return base, over.astype(jnp.int32)


def _dispatch_kernel(base_ref, over_ref, h2b_ref, slot_ref, aff_ref, xs_out, gate_out, p_scr, gate_scr, *, cap):
    bi = pl.program_id(0)
    jo = pl.program_id(1)
    n_j = pl.num_programs(1) * ROUTE_INNER
    n_e, _, t = slot_ref.shape

    @pl.when(jo == 0)
    def _():
        xs_out[...] = jnp.zeros(xs_out.shape, xs_out.dtype)
        gate_scr[...] = jnp.zeros(gate_scr.shape, gate_scr.dtype)

    def chunk(ji, carry):
        j = jo * ROUTE_INNER + ji
        h = h2b_ref[0, pl.ds(pl.multiple_of(ji * t, t), t), :]
        slot_row = lambda e: slot_ref[e, pl.ds(j, 1), :]
        aff_row = lambda e: aff_ref[e, pl.ds(j, 1), :]

        @pl.when(over_ref[bi * n_j + j] == 0)
        def _():
            w_iota = lax.broadcasted_iota(jnp.int32, (ROUTE_WIN, t), 0).astype(F32)
            bases = []
            for e in range(n_e):
                base = pl.multiple_of(base_ref[(bi * n_e + e) * n_j + j], BF16_ROWS)
                hit = slot_row(e) == w_iota + base.astype(F32)
                p_scr[e * ROUTE_WIN:(e + 1) * ROUTE_WIN, :] = jnp.where(hit, 1.0, 0.0).astype(BF16)
                gate_scr[e, pl.ds(base, ROUTE_WIN), :] += jnp.sum(jnp.where(hit, aff_row(e), 0.0),
                                                                  axis=1, keepdims=True)
                bases.append(base)
            picked = _dot(p_scr[...], h)
            for e in range(n_e):
                rows = pl.ds(bases[e], ROUTE_WIN)
                xs_out[e, 0, rows, :] += picked[e * ROUTE_WIN:(e + 1) * ROUTE_WIN, :].astype(BF16)

        @pl.when(over_ref[bi * n_j + j] != 0)
        def _():
            c_iota = lax.broadcasted_iota(jnp.int32, (cap, t), 0).astype(F32)
            for e in range(n_e):
                hit = slot_row(e) == c_iota
                xs_out[e, 0] += _dot(jnp.where(hit, 1.0, 0.0).astype(BF16), h).astype(BF16)
                gate_scr[e] += jnp.sum(jnp.where(hit, aff_row(e), 0.0), axis=1, keepdims=True)

        return carry

    lax.fori_loop(0, ROUTE_INNER, chunk, 0)

    @pl.when(jo == pl.num_programs(1) - 1)
    def _():
        eye = lax.broadcasted_iota(jnp.int32, (cap, cap), 0) == lax.broadcasted_iota(jnp.int32, (cap, cap), 1)
        for e in range(n_e):
            gate_out[e, 0] = jnp.sum(jnp.where(eye, gate_scr[e], 0.0), axis=0, keepdims=True)


def _dispatch(base_flat, over_flat, h2b, slot3d, aff3d, n_e, cap):
    b, s, d = h2b.shape
    t = ROUTE_CHUNK
    rows = ROUTE_INNER * t
    grid_spec = pltpu.PrefetchScalarGridSpec(
        num_scalar_prefetch=2,
        grid=(b, s // rows),
        in_specs=[
            pl.BlockSpec((1, rows, d), lambda bi, jo, *_: (bi, jo, 0)),
            pl.BlockSpec((n_e, s // t, t), lambda bi, jo, *_: (bi, 0, 0)),
            pl.BlockSpec((n_e, s // t, t), lambda bi, jo, *_: (bi, 0, 0)),
        ],
        out_specs=[
            pl.BlockSpec((n_e, 1, cap, d), lambda bi, jo, *_: (0, bi, 0, 0)),
            pl.BlockSpec((n_e, 1, 1, cap), lambda bi, jo, *_: (0, bi, 0, 0)),
        ],
        scratch_shapes=[pltpu.VMEM((n_e * ROUTE_WIN, t), BF16), pltpu.VMEM((n_e, cap, 1), F32)],
    )
    return pl.pallas_call(
        functools.partial(_dispatch_kernel, cap=cap),
        grid_spec=grid_spec,
        out_shape=[
            jax.ShapeDtypeStruct((n_e, b, cap, d), BF16),
            jax.ShapeDtypeStruct((n_e, b, 1, cap), F32),
        ],
        compiler_params=_cparams(("parallel", "arbitrary")),
        name="dispatch",
    )(base_flat, over_flat, h2b, slot3d, aff3d)


FFN_ROWS = 512
FFN_CHUNK = 512


def _experts_kernel(xs_ref, gate_ref, wg_ref, wu_ref, wd_ref, y_out, wg_s, wu_s, wd_s, hid_s):
    e = pl.program_id(0)
    f = pl.program_id(1)
    n_chunks = wg_s.shape[1]

    def stage():
        slot = e % 2
        wg_s[slot, f] = wg_ref[0].astype(BF16)
        wu_s[slot, f] = wu_ref[0].astype(BF16)
        wd_s[slot, f] = wd_ref[0].astype(BF16)

    @pl.when(e == 0)
    def _():
        stage()
        y_out[0] = jnp.zeros(y_out.shape[1:], y_out.dtype)

    @pl.when(e > 0)
    def _():
        stage()
        slot = (e - 1) % 2
        xs = xs_ref[0]
        for c in range(n_chunks):
            g = _dot(xs, wg_s[slot, c])
            u = _dot(xs, wu_s[slot, c])
            hid_s[:, c * FFN_CHUNK:(c + 1) * FFN_CHUNK] = (g * _sigmoid(g) * u).astype(BF16)
        y = _dot(hid_s[:, 0:FFN_CHUNK], wd_s[slot, 0])
        for c in range(1, n_chunks):
            y = y + _dot(hid_s[:, c * FFN_CHUNK:(c + 1) * FFN_CHUNK], wd_s[slot, c])
        rows = xs.shape[0]
        eye = lax.broadcasted_iota(jnp.int32, (rows, rows), 0) == lax.broadcasted_iota(jnp.int32, (rows, rows), 1)
        gate_col = jnp.sum(jnp.where(eye, gate_ref[0], 0.0), axis=1, keepdims=True)
        y_out[0] = (y * gate_col).astype(BF16)


def _experts(xs, gate, w_gate, w_up, w_down):
    e, n, d = xs.shape
    ff = w_gate.shape[2]
    n_chunks = ff // FFN_CHUNK
    assert n // FFN_ROWS == n_chunks
    prev = lambda ei, fi: (jnp.maximum(ei - 1, 0), fi, 0)
    cur = lambda ei: jnp.minimum(ei, e - 1)
    return pl.pallas_call(
        _experts_kernel,
        grid=(e + 1, n_chunks),
        in_specs=[
            pl.BlockSpec((1, FFN_ROWS, d), prev),
            pl.BlockSpec((1, 1, FFN_ROWS), lambda ei, fi: (jnp.maximum(ei - 1, 0), 0, fi)),
            pl.BlockSpec((1, d, FFN_CHUNK), lambda ei, fi: (cur(ei), 0, fi)),
            pl.BlockSpec((1, d, FFN_CHUNK), lambda ei, fi: (cur(ei), 0, fi)),
            pl.BlockSpec((1, FFN_CHUNK, d), lambda ei, fi: (cur(ei), fi, 0)),
        ],
        out_specs=pl.BlockSpec((1, FFN_ROWS, d), lambda ei, fi: (jnp.where(ei == 0, e, ei - 1), fi, 0)),
        out_shape=jax.ShapeDtypeStruct((e + 1, n, d), BF16),
        scratch_shapes=[
            pltpu.VMEM((2, n_chunks, d, FFN_CHUNK), BF16),
            pltpu.VMEM((2, n_chunks, d, FFN_CHUNK), BF16),
            pltpu.VMEM((2, n_chunks, FFN_CHUNK, d), BF16),
            pltpu.VMEM((FFN_ROWS, ff), BF16),
        ],
        compiler_params=_cparams(("arbitrary", "arbitrary")),
        name="experts",
    )(xs, gate, w_gate, w_up, w_down)


def _combine_kernel(base_ref, over_ref, slot_t_ref, base_t_ref, y_ref, h2_ref, g3_ref, b3_ref, out_ref, ywin_scr,
                    *, cap):
    bi = pl.program_id(0)
    jo = pl.program_id(1)
    n_j = pl.num_programs(1) * ROUTE_INNER
    n_e = y_ref.shape[0]
    rows = slot_t_ref.shape[1]
    t = ROUTE_CHUNK

    def spread_cols(per_expert):
        width = n_e * per_expert
        col = lax.broadcasted_iota(jnp.int32, (rows, width), 1) // per_expert
        return jnp.where(lax.broadcasted_iota(jnp.int32, (rows, width), 0) == bi * n_e + col, 1.0, 0.0).astype(BF16)

    def lane_in_group(per_expert):
        return (lax.broadcasted_iota(jnp.int32, (1, n_e * per_expert), 1) % per_expert).astype(F32)

    win_spread = spread_cols(ROUTE_WIN)
    win_lane = lane_in_group(ROUTE_WIN)

    def chunk(ji, carry):
        j = jo * ROUTE_INNER + ji
        tok = pl.ds(pl.multiple_of(ji * t, t), t)

        def finish(ff):
            out_ref[0, tok, :] = _layer_norm(DEEPNORM_ALPHA * h2_ref[0, tok, :] + ff, g3_ref[...], b3_ref[...])

        slot_bf = slot_t_ref[tok, :].astype(BF16)

        @pl.when(over_ref[bi * n_j + j] == 0)
        def _():
            slot_wide = _dot(slot_bf, win_spread)
            base_wide = _dot(jnp.broadcast_to(base_t_ref[ji], (SUBLANES, rows)).astype(BF16), win_spread)[0:1, :]
            onehot = jnp.where(slot_wide - base_wide == win_lane, 1.0, 0.0).astype(BF16)
            for e in range(n_e):
                base = pl.multiple_of(base_ref[(bi * n_e + e) * n_j + j], BF16_ROWS)
                ywin_scr[e * ROUTE_WIN:(e + 1) * ROUTE_WIN, :] = y_ref[e, 0, pl.ds(base, ROUTE_WIN), :]
            finish(_dot(onehot, ywin_scr[...]))

        @pl.when(over_ref[bi * n_j + j] != 0)
        def _():
            slot_wide = _dot(slot_bf, spread_cols(cap))
            onehot = jnp.where(slot_wide == lane_in_group(cap), 1.0, 0.0).astype(BF16)
            finish(_dot(onehot, y_ref[:, 0].reshape(n_e * cap, y_ref.shape[3])))

        return carry

    lax.fori_loop(0, ROUTE_INNER, chunk, 0)


def _combine(base_flat, over_flat, slot_t, base_t, y4, h2, g3, b3, cap):
    b, s, d = h2.shape
    e = y4.shape[0] - 1
    rows = ROUTE_INNER * ROUTE_CHUNK
    const = lambda bi, jo, *_: (0, 0)
    grid_spec = pltpu.PrefetchScalarGridSpec(
        num_scalar_prefetch=2,
        grid=(b, s // rows),
        in_specs=[
            pl.BlockSpec((rows, slot_t.shape[1]), lambda bi, jo, *_: (jo, 0)),
            pl.BlockSpec((ROUTE_INNER, 1, base_t.shape[2]), lambda bi, jo, *_: (jo, 0, 0)),
            pl.BlockSpec((e, 1, cap, d), lambda bi, jo, *_: (0, bi, 0, 0)),
            pl.BlockSpec((1, rows, d), lambda bi, jo, *_: (bi, jo, 0)),
            pl.BlockSpec(g3.shape, const),
            pl.BlockSpec(b3.shape, const),
        ],
        out_specs=pl.BlockSpec((1, rows, d), lambda bi, jo, *_: (bi, jo, 0)),
        scratch_shapes=[pltpu.VMEM((e * ROUTE_WIN, d), BF16)],
    )
    return pl.pallas_call(
        functools.partial(_combine_kernel, cap=cap),
        grid_spec=grid_spec,
        out_shape=jax.ShapeDtypeStruct((b, s, d), F32),
        compiler_params=_cparams(("parallel", "parallel")),
        name="combine",
    )(base_flat, over_flat, slot_t, base_t, y4, h2, g3, b3)


def _extend_weights(w_in, w_uq, w_uk, w_uv):
    half = MLA_ROPE_DIM // 2
    d = w_in.shape[0]
    w_t = w_in.T.astype(BF16)
    kr = w_t[SPLIT_KV:SPLIT_KR]
    z = lambda n: jnp.zeros((n, d), BF16)
    assert HEAD_PAD - MLA_QK_DIM == MLA_ROPE_DIM
    kr_tile = jnp.concatenate([z(MLA_NOPE_DIM), kr, -kr[half:], kr[:half]], axis=0)
    w_in_ext = jnp.concatenate([w_t[:SPLIT_KV], kr_tile, w_t[SPLIT_KR:]], axis=0).T

    wq = w_uq.reshape(MLA_Q_RANK, MLA_HEADS, MLA_QK_DIM)
    wq_ext = jnp.concatenate([wq, -wq[:, :, MLA_NOPE_DIM + half:], wq[:, :, MLA_NOPE_DIM:MLA_NOPE_DIM + half]],
                             axis=2).reshape(MLA_Q_RANK, QK_PAD)

    wk = w_uk.reshape(MLA_KV_RANK, MLA_HEADS, MLA_NOPE_DIM)
    wk_full = jnp.concatenate([wk, jnp.zeros((MLA_KV_RANK, MLA_HEADS, HEAD_PAD - MLA_NOPE_DIM), w_uk.dtype)], axis=2)
    wv = w_uv.reshape(MLA_KV_RANK, MLA_HEADS, MLA_V_DIM)
    wv_full = jnp.concatenate([wv, jnp.zeros((MLA_KV_RANK, MLA_HEADS, HEAD_PAD - MLA_V_DIM), w_uv.dtype)], axis=2)
    wkv_ext = jnp.concatenate([wk_full.reshape(MLA_KV_RANK, QK_PAD), wv_full.reshape(MLA_KV_RANK, QK_PAD)], axis=1)
    return w_in_ext.astype(BF16), wq_ext.astype(BF16), wkv_ext.astype(BF16)


def _rope_freq_column():
    inv_freq = ROPE_BASE ** (-jnp.arange(0, MLA_ROPE_DIM, 2, dtype=F32) / MLA_ROPE_DIM)
    return inv_freq[:, None]


def _value_one_lanes():
    one_hot = (jnp.arange(HEAD_PAD) == MLA_V_DIM).astype(F32)
    return jnp.tile(one_hot, MLA_HEADS)[None, :]


def kernel(x, mem, positions, w_in, q_norm_g, w_uq, kv_norm_g, w_uk, w_uv, conv_w, conv_b, conv_ln_g, conv_ln_b,
           w_o, ln1_g, ln1_b, xa_w_q, xa_w_k, xa_w_v, xa_w_o, ln2_g, ln2_b, w_router, w_gate, w_up, w_down,
           ln3_g, ln3_b):
    assert w_in.shape[0] == DEPTH == 1
    b, s, d = x.shape
    cap = CAPACITY_FACTOR * s // N_EXPERTS
    assert s % max(FRONT_ROWS, ATTN_ROWS, MID_ROWS, ROUTE_INNER * ROUTE_CHUNK) == 0 and cap >= ROUTE_WIN
    assert mem.shape == (b, MEM_LEN, d) and d == D_MODEL

    w_in_ext, wq_ext, wkv_ext = _extend_weights(w_in[0], w_uq[0], w_uk[0], w_uv[0])
    posf = positions.astype(F32)[:, None, :]
    q, k, v, u_pre = _front(x, posf, w_in_ext, q_norm_g, wq_ext, kv_norm_g, wkv_ext, _rope_freq_column(),
                             _value_one_lanes(), FRONT_ROWS)
    attn = _attention(q, k, v, ATTN_ROWS)
    kx, vx = _mem_kv(mem, xa_w_k[0].astype(BF16), xa_w_v[0].astype(BF16))
    h2, h2b, aff = _mid(u_pre, attn, x, conv_w[0], conv_b, conv_ln_g, conv_ln_b, w_o[0].astype(BF16), ln1_g, ln1_b,
                        (xa_w_q[0] * MEM_Q_SCALE).astype(BF16), kx, vx, xa_w_o[0].astype(BF16), ln2_g, ln2_b,
                        _split_router(w_router[0]), MID_ROWS)

    aff2d = aff.reshape(b * N_EXPERTS, s)
    slot, slot_t, off_tab = _topk(aff2d, cap)
    n_j = s // ROUTE_CHUNK
    base, over = _route_tables(off_tab[:, :n_j + 1], b, N_EXPERTS, cap)
    base_flat, over_flat = base.reshape(-1), over.reshape(-1)
    by_chunk = lambda a: a.reshape(b * N_EXPERTS, n_j, ROUTE_CHUNK)
    xs, gate = _dispatch(base_flat, over_flat, h2b, by_chunk(slot), by_chunk(aff2d), N_EXPERTS, cap)
    y = _experts(xs.reshape(N_EXPERTS, b * cap, d), gate.reshape(N_EXPERTS, 1, b * cap),
                 w_gate[0], w_up[0], w_down[0])
    base_t = base.T.astype(F32).reshape(n_j, 1, b * N_EXPERTS)
    return _combine(base_flat, over_flat, slot_t, base_t, y.reshape(N_EXPERTS + 1, b, cap, d), h2, ln3_g, ln3_b, cap)
```

```python
import functools
import math

import jax
import jax.numpy as jnp
from jax import lax
from jax.experimental import pallas as pl
from jax.experimental.pallas import tpu as pltpu

F32 = jnp.float32
BF16 = jnp.bfloat16

D_MODEL = 1024
MLA_HEADS = 8
MLA_NOPE_DIM = 64
MLA_ROPE_DIM = 32
MLA_QK_DIM = MLA_NOPE_DIM + MLA_ROPE_DIM
MLA_V_DIM = 64
MLA_Q_RANK = 256
MLA_KV_RANK = 128
MLA_WIDTH = MLA_HEADS * MLA_V_DIM
CONV_CH = D_MODEL - MLA_WIDTH
CONV_WIDTH = 31
CONV_PAD = (CONV_WIDTH - 1) // 2
ROPE_BASE = 10000.0
MEM_HEADS = 4
MEM_HEAD_DIM = D_MODEL // MEM_HEADS
MEM_Q_SCALE = 1.0 / math.sqrt(MEM_HEAD_DIM)
assert math.frexp(MEM_Q_SCALE)[0] == 0.5
N_EXPERTS = 16
EXPERT_FF = 2048
CAPACITY_FACTOR = 2
NORM_EPS = 1e-5
DEPTH = 1
DEEPNORM_ALPHA = (2.0 * DEPTH) ** 0.25
SPLIT_Q = MLA_Q_RANK
SPLIT_KV = SPLIT_Q + MLA_KV_RANK
SPLIT_KR = SPLIT_KV + MLA_ROPE_DIM

HEAD_PAD = 128
QK_PAD = MLA_HEADS * HEAD_PAD
OFF_CQ = 0
OFF_CKV = OFF_CQ + MLA_Q_RANK
OFF_KR = OFF_CKV + MLA_KV_RANK
OFF_A = OFF_KR + HEAD_PAD
OFF_G = OFF_A + CONV_CH
IN_EXT = OFF_G + CONV_CH

SUBLANES = 8
LANES = 128
BF16_ROWS = 16
ROUTE_CHUNK = 256
ROUTE_WIN = 64
ROUTE_INNER = 4
HALO = 16
VMEM_LIMIT = 56 * 1024 * 1024
FRONT_ROWS = 1024
ATTN_ROWS = 1024
MID_ROWS = 1024
MEM_LEN = 256


def _cparams(sem):
    return pltpu.CompilerParams(dimension_semantics=sem, vmem_limit_bytes=VMEM_LIMIT)


def _layer_norm(v, g, b):
    mu = jnp.mean(v, axis=-1, keepdims=True)
    d = v - mu
    var = jnp.mean(d * d, axis=-1, keepdims=True)
    return d * lax.rsqrt(var + NORM_EPS) * g + b


def _rms_norm(v, g):
    return v * lax.rsqrt(jnp.mean(v * v, axis=-1, keepdims=True) + NORM_EPS) * g


def _sigmoid(v):
    return 1.0 / (1.0 + jnp.exp(-v))


def _dot(a, b):
    return jnp.dot(a, b, preferred_element_type=F32)


def _dot_nt(a, b, precision=None):
    return lax.dot_general(a, b, (((1,), (1,)), ((), ())), preferred_element_type=F32, precision=precision)


def _front_kernel(x_ref, pos_ref, win_ref, qg_ref, wq_ref, kvg_ref, wkv_ref, invf_ref, vone_ref,
                  q_out, k_out, v_out, u_out):
    x = x_ref[0].astype(BF16)
    hc = _dot(x, win_ref[...])
    cqn = _rms_norm(hc[:, OFF_CQ:OFF_CKV], qg_ref[...])
    qq = _dot(cqn.astype(BF16), wq_ref[...])
    ckvn = _rms_norm(hc[:, OFF_CKV:OFF_KR], kvg_ref[...])
    kv = _dot(ckvn.astype(BF16), wkv_ref[...])
    tm = x_ref.shape[1]
    ang = invf_ref[...] * pos_ref[0]
    cos_h, sin_h = jnp.cos(ang), jnp.sin(ang)
    ones = lambda n: jnp.ones((n, tm), F32)
    zeros = lambda n: jnp.zeros((n, tm), F32)
    cos = jnp.concatenate([ones(MLA_NOPE_DIM), cos_h, cos_h, ones(HEAD_PAD - MLA_QK_DIM)], axis=0).T
    sin = jnp.concatenate([zeros(MLA_NOPE_DIM), sin_h, sin_h, zeros(HEAD_PAD - MLA_QK_DIM)], axis=0).T
    kr = hc[:, OFF_KR:OFF_A]
    kr_partner = pltpu.roll(kr, HEAD_PAD - MLA_ROPE_DIM, axis=1)
    rope_lanes = lax.broadcasted_iota(jnp.int32, (1, HEAD_PAD), 1) < MLA_QK_DIM
    krot = jnp.where(rope_lanes, kr * cos + kr_partner * sin, 0.0)
    scale = math.log2(math.e) / math.sqrt(MLA_QK_DIM)
    for h in range(MLA_HEADS):
        lo, hi = h * HEAD_PAD, (h + 1) * HEAD_PAD
        qh = (qq[:, lo:hi] * cos + pltpu.roll(qq[:, lo:hi], HEAD_PAD - MLA_ROPE_DIM, axis=1) * sin) * scale
        q_out[0, :, lo:hi] = qh.astype(BF16)
        k_out[0, :, lo:hi] = (kv[:, lo:hi] + krot).astype(BF16)
    v_out[0] = (kv[:, QK_PAD:] + vone_ref[...]).astype(BF16)
    u_out[0] = hc[:, OFF_A:OFF_G] * _sigmoid(hc[:, OFF_G:IN_EXT])


def _front(x, posf, w_in_ext, qg, wq_ext, kvg, wkv_ext, invf, vone, tm):
    b, s, d = x.shape
    const = lambda bi, i: (0, 0)
    return pl.pallas_call(
        _front_kernel,
        grid=(b, s // tm),
        in_specs=[
            pl.BlockSpec((1, tm, d), lambda bi, i: (bi, i, 0)),
            pl.BlockSpec((1, 1, tm), lambda bi, i: (bi, 0, i)),
            pl.BlockSpec(w_in_ext.shape, const),
            pl.BlockSpec(qg.shape, const),
            pl.BlockSpec(wq_ext.shape, const),
            pl.BlockSpec(kvg.shape, const),
            pl.BlockSpec(wkv_ext.shape, const),
            pl.BlockSpec(invf.shape, const),
            pl.BlockSpec(vone.shape, const),
        ],
        out_specs=[
            pl.BlockSpec((1, tm, QK_PAD), lambda bi, i: (bi, i, 0)),
            pl.BlockSpec((1, tm, QK_PAD), lambda bi, i: (bi, i, 0)),
            pl.BlockSpec((1, tm, QK_PAD), lambda bi, i: (bi, i, 0)),
            pl.BlockSpec((1, tm, CONV_CH), lambda bi, i: (bi, i, 0)),
        ],
        out_shape=[
            jax.ShapeDtypeStruct((b, s, QK_PAD), BF16),
            jax.ShapeDtypeStruct((b, s, QK_PAD), BF16),
            jax.ShapeDtypeStruct((b, s, QK_PAD), BF16),
            jax.ShapeDtypeStruct((b, s, CONV_CH), F32),
        ],
        compiler_params=_cparams(("parallel", "parallel")),
        name="front",
    )(x, posf, w_in_ext, qg, wq_ext, kvg, wkv_ext, invf, vone)


def _attn_kernel(q_ref, k_ref, v_ref, o_ref):
    for h in range(MLA_HEADS):
        lo, hi = h * HEAD_PAD, (h + 1) * HEAD_PAD
        sc = _dot_nt(q_ref[0, :, lo:hi], k_ref[0, :, lo:hi])
        p = jnp.exp2(sc - jnp.max(sc, axis=-1, keepdims=True))
        pv = _dot(p.astype(BF16), v_ref[0, :, lo:hi])
        o = pv[:, :MLA_V_DIM] / pv[:, MLA_V_DIM:MLA_V_DIM + 1]
        o_ref[0, :, h * MLA_V_DIM:(h + 1) * MLA_V_DIM] = o.astype(BF16)


def _attention(q, k, v, tq):
    b, s, _ = q.shape
    return pl.pallas_call(
        _attn_kernel,
        grid=(b, s // tq),
        in_specs=[
            pl.BlockSpec((1, tq, QK_PAD), lambda bi, i: (bi, i, 0)),
            pl.BlockSpec((1, s, QK_PAD), lambda bi, i: (bi, 0, 0)),
            pl.BlockSpec((1, s, QK_PAD), lambda bi, i: (bi, 0, 0)),
        ],
        out_specs=pl.BlockSpec((1, tq, MLA_WIDTH), lambda bi, i: (bi, i, 0)),
        out_shape=jax.ShapeDtypeStruct((b, s, MLA_WIDTH), BF16),
        compiler_params=_cparams(("parallel", "parallel")),
        name="attn",
    )(q, k, v)


CONV_SUB = 64
CONV_TILE = 512


def _mid_kernel(ucur_ref, uprev_ref, unext_ref, attn_ref, x_ref, cw_ref, cb_ref, cg_ref, cbeta_ref, wo_ref,
                g1_ref, b1_ref, wq_ref, mem_ref, wk_ref, wv_ref, xwo_ref, g2_ref, b2_ref, wr_ref,
                h2_out, h2b_out, aff_out, win_ref, shift_ref, u_scr, o_scr, kx_scr, vx_scr):
    tm = ucur_ref.shape[1]
    i = pl.program_id(1)
    last = pl.num_programs(1) - 1

    @pl.when(i == 0)
    def _():
        m = mem_ref[0].astype(BF16)
        kx_scr[...] = _dot(m, wk_ref[...]).astype(BF16)
        vx_scr[...] = _dot(m, wv_ref[...]).astype(BF16)

    win_ref[0:HALO, :] = jnp.where(i > 0, uprev_ref[0], 0.0)
    win_ref[HALO:HALO + tm, :] = ucur_ref[0]
    win_ref[HALO + tm:2 * HALO + tm, :] = jnp.where(i < last, unext_ref[0], 0.0)
    span = CONV_TILE + 2 * HALO - SUBLANES
    for c in range(tm // CONV_TILE):
        base = c * CONV_TILE
        shift_ref[0] = win_ref[base:base + CONV_TILE + 2 * HALO, :]
        for j in range(1, SUBLANES):
            shift_ref[j, 0:span, :] = win_ref[base + j:base + j + span, :]
        for r in range(CONV_TILE // CONV_SUB):
            acc = jnp.broadcast_to(cb_ref[...], (CONV_SUB, CONV_CH))
            for t in range(CONV_WIDTH):
                off = HALO - CONV_PAD + t
                row = r * CONV_SUB + (off // SUBLANES) * SUBLANES
                acc = acc + shift_ref[off % SUBLANES, row:row + CONV_SUB, :] * cw_ref[t:t + 1, :]
            y = _layer_norm(acc, cg_ref[...], cbeta_ref[...])
            u_scr[base + r * CONV_SUB:base + (r + 1) * CONV_SUB, :] = (y * _sigmoid(y)).astype(BF16)
    mix = _dot(attn_ref[0], wo_ref[0:MLA_WIDTH, :]) + _dot(u_scr[...], wo_ref[MLA_WIDTH:, :])
    h1 = _layer_norm(DEEPNORM_ALPHA * x_ref[0] + mix, g1_ref[...], b1_ref[...])
    q = _dot(h1.astype(BF16), wq_ref[...]).astype(BF16)
    for h in range(MEM_HEADS):
        lo, hi = h * MEM_HEAD_DIM, (h + 1) * MEM_HEAD_DIM
        sc = _dot_nt(q[:, lo:hi], kx_scr[:, lo:hi])
        p = jnp.exp(sc - jnp.max(sc, axis=-1, keepdims=True))
        l = jnp.sum(p, axis=-1, keepdims=True)
        o_scr[:, lo:hi] = (_dot(p.astype(BF16), vx_scr[:, lo:hi]) / l).astype(BF16)
    xa = _dot(o_scr[...], xwo_ref[...])
    h2 = _layer_norm(DEEPNORM_ALPHA * h1 + xa, g2_ref[...], b2_ref[...])
    h2_out[0] = h2
    h2_hi = h2.astype(BF16)
    h2b_out[0] = h2_hi
    h2_lo = (h2 - h2_hi.astype(F32)).astype(BF16)
    hi_terms = _dot(h2_hi, wr_ref[...])
    logits = hi_terms[:, :LANES] + hi_terms[:, LANES:] + _dot(h2_lo, wr_ref[:, :LANES])
    lt = logits.T[0:aff_out.shape[1], :]
    ex = jnp.exp(lt - jnp.max(lt, axis=0, keepdims=True))
    aff_out[0] = ex / jnp.sum(ex, axis=0, keepdims=True)


def _mid(u_pre, attn, x, conv_w, conv_b, conv_g, conv_beta, w_o, g1, b1, wq, mem, wk, wv, xwo, g2, b2, wr_split, tm):
    b, s, d = x.shape
    m = mem.shape[1]
    nh = tm // HALO
    const = lambda bi, i: (0, 0)
    tile = lambda w: pl.BlockSpec((1, tm, w), lambda bi, i: (bi, i, 0))
    whole = lambda arr: pl.BlockSpec(arr.shape, const)
    once = lambda arr: pl.BlockSpec(arr.shape, const, pipeline_mode=pl.Buffered(1))
    return pl.pallas_call(
        _mid_kernel,
        grid=(b, s // tm),
        in_specs=[
            tile(CONV_CH),
            pl.BlockSpec((1, HALO, CONV_CH), lambda bi, i: (bi, jnp.maximum(i * nh - 1, 0), 0)),
            pl.BlockSpec((1, HALO, CONV_CH), lambda bi, i: (bi, jnp.minimum((i + 1) * nh, s // HALO - 1), 0)),
            tile(MLA_WIDTH),
            tile(d),
            whole(conv_w), whole(conv_b), whole(conv_g), whole(conv_beta), once(w_o), whole(g1), whole(b1),
            once(wq),
            pl.BlockSpec((1, m, d), lambda bi, i: (bi, 0, 0)),
            once(wk), once(wv),
            once(xwo), whole(g2), whole(b2), whole(wr_split),
        ],
        out_specs=[
            tile(d),
            tile(d),
            pl.BlockSpec((1, N_EXPERTS, tm), lambda bi, i: (bi, 0, i)),
        ],
        out_shape=[
            jax.ShapeDtypeStruct((b, s, d), F32),
            jax.ShapeDtypeStruct((b, s, d), BF16),
            jax.ShapeDtypeStruct((b, N_EXPERTS, s), F32),
        ],
        scratch_shapes=[
            pltpu.VMEM((tm + 2 * HALO, CONV_CH), F32),
            pltpu.VMEM((SUBLANES, CONV_TILE + 2 * HALO, CONV_CH), F32),
            pltpu.VMEM((tm, CONV_CH), BF16),
            pltpu.VMEM((tm, d), BF16),
            pltpu.VMEM((m, d), BF16),
            pltpu.VMEM((m, d), BF16),
        ],
        compiler_params=_cparams(("parallel", "arbitrary")),
        name="mid",
    )(u_pre, u_pre, u_pre, attn, x, conv_w, conv_b, conv_g, conv_beta, w_o, g1, b1, wq, mem, wk, wv, xwo, g2, b2,
      wr_split)


def _split_router(w_router):
    hi = w_router.astype(BF16)
    lo = (w_router - hi.astype(F32)).astype(BF16)
    pad = jnp.zeros((w_router.shape[0], LANES - w_router.shape[1]), BF16)
    return jnp.concatenate([hi, pad, lo, pad], axis=1)


BISECT_STEPS_PER_CHECK = 4
BISECT_MAX_CHECKS = 320


def _topk_kernel(aff_ref, slot_out, slot_t_out, off_out, *, cap):
    aff = aff_ref[...]
    rows, s = aff.shape
    capf = jnp.float32(cap)

    def not_done(carry):
        return jnp.logical_and(carry[2] > 0, carry[3] < BISECT_MAX_CHECKS)

    def halve(_, bounds):
        lo, hi = bounds
        mid = 0.5 * (lo + hi)
        take = jnp.sum(jnp.where(aff >= mid, 1.0, 0.0), axis=1, keepdims=True) >= capf
        return jnp.where(take, mid, lo), jnp.where(take, hi, mid)

    def bisect(carry):
        lo, hi = lax.fori_loop(0, BISECT_STEPS_PER_CHECK, halve, carry[:2])
        smallest_in = jnp.min(jnp.where(aff >= lo, aff, jnp.inf), axis=1, keepdims=True)
        largest_in = jnp.max(jnp.where(aff < hi, aff, -jnp.inf), axis=1, keepdims=True)
        open_rows = jnp.sum(jnp.where(smallest_in == largest_in, 0.0, 1.0))
        return lo, hi, open_rows.astype(jnp.int32), carry[3] + 1

    _, hi, _, _ = lax.while_loop(not_done, bisect, (jnp.zeros((rows, 1), F32), jnp.full((rows, 1), 2.0, F32),
                                                    jnp.int32(1), jnp.int32(0)))
    th = jnp.max(jnp.where(aff < hi, aff, -jnp.inf), axis=1, keepdims=True)
    gt = aff > th
    eq = aff == th
    n_gt = jnp.sum(jnp.where(gt, 1.0, 0.0), axis=1, keepdims=True)
    tri = jnp.where(lax.broadcasted_iota(jnp.int32, (s, s), 0) < lax.broadcasted_iota(jnp.int32, (s, s), 1),
                    1.0, 0.0).astype(BF16)
    tie_rank = _dot(jnp.where(eq, 1.0, 0.0).astype(BF16), tri)
    sel = jnp.logical_or(gt, jnp.logical_and(eq, tie_rank < (capf - n_gt)))
    pos = _dot(jnp.where(sel, 1.0, 0.0).astype(BF16), tri)
    slot = jnp.where(sel, pos, -1.0)
    slot_out[...] = slot
    slot_t_out[...] = slot.T
    lanes = off_out.shape[1]
    before = jnp.where(lax.broadcasted_iota(jnp.int32, (s, lanes), 0)
                       < ROUTE_CHUNK * lax.broadcasted_iota(jnp.int32, (s, lanes), 1), 1.0, 0.0).astype(BF16)
    off_out[...] = _dot(jnp.where(sel, 1.0, 0.0).astype(BF16), before)


def _topk(aff2d, cap):
    rows, s = aff2d.shape
    return pl.pallas_call(
        functools.partial(_topk_kernel, cap=cap),
        out_shape=[
            jax.ShapeDtypeStruct((rows, s), F32),
            jax.ShapeDtypeStruct((s, rows), F32),
            jax.ShapeDtypeStruct((rows, LANES), F32),
        ],
        compiler_params=pltpu.CompilerParams(vmem_limit_bytes=VMEM_LIMIT),
        name="topk",
    )(aff2d)


def _route_tables(off_tab, n_b, n_e, cap):
    n_j = off_tab.shape[1] - 1
    off = off_tab.astype(jnp.int32)
    base = jnp.minimum((off[:, :n_j] // BF16_ROWS) * BF16_ROWS, cap - ROUTE_WIN)
    over = jnp.any((off[:, 1:] - base > ROUTE_WIN).reshape(n_b, n_e, n_j), axis=1)
    return base, over.astype(jnp.int32)


def _dispatch_kernel(base_ref, over_ref, h2b_ref, slot_ref, aff_ref, xs_out, gate_out, p_scr, gate_scr, *, cap):
    bi = pl.program_id(0)
    jo = pl.program_id(1)
    n_j = pl.num_programs(1) * ROUTE_INNER
    n_e, _, t = slot_ref.shape

    @pl.when(jo == 0)
    def _():
        xs_out[...] = jnp.zeros(xs_out.shape, xs_out.dtype)
        gate_scr[...] = jnp.zeros(gate_scr.shape, gate_scr.dtype)

    def chunk(ji, carry):
        j = jo * ROUTE_INNER + ji
        h = h2b_ref[0, pl.ds(pl.multiple_of(ji * t, t), t), :]
        slot_row = lambda e: slot_ref[e, pl.ds(j, 1), :]
        aff_row = lambda e: aff_ref[e, pl.ds(j, 1), :]

        @pl.when(over_ref[bi * n_j + j] == 0)
        def _():
            w_iota = lax.broadcasted_iota(jnp.int32, (ROUTE_WIN, t), 0).astype(F32)
            bases = []
            for e in range(n_e):
                base = pl.multiple_of(base_ref[(bi * n_e + e) * n_j + j], BF16_ROWS)
                hit = slot_row(e) == w_iota + base.astype(F32)
                p_scr[e * ROUTE_WIN:(e + 1) * ROUTE_WIN, :] = jnp.where(hit, 1.0, 0.0).astype(BF16)
                gate_scr[e, pl.ds(base, ROUTE_WIN), :] += jnp.sum(jnp.where(hit, aff_row(e), 0.0),
                                                                  axis=1, keepdims=True)
                bases.append(base)
            picked = _dot(p_scr[...], h)
            for e in range(n_e):
                rows = pl.ds(bases[e], ROUTE_WIN)
                xs_out[e, 0, rows, :] += picked[e * ROUTE_WIN:(e + 1) * ROUTE_WIN, :].astype(BF16)

        @pl.when(over_ref[bi * n_j + j] != 0)
        def _():
            c_iota = lax.broadcasted_iota(jnp.int32, (cap, t), 0).astype(F32)
            for e in range(n_e):
                hit = slot_row(e) == c_iota
                xs_out[e, 0] += _dot(jnp.where(hit, 1.0, 0.0).astype(BF16), h).astype(BF16)
                gate_scr[e] += jnp.sum(jnp.where(hit, aff_row(e), 0.0), axis=1, keepdims=True)

        return carry

    lax.fori_loop(0, ROUTE_INNER, chunk, 0)

    @pl.when(jo == pl.num_programs(1) - 1)
    def _():
        eye = lax.broadcasted_iota(jnp.int32, (cap, cap), 0) == lax.broadcasted_iota(jnp.int32, (cap, cap), 1)
        for e in range(n_e):
            gate_out[e, 0] = jnp.sum(jnp.where(eye, gate_scr[e], 0.0), axis=0, keepdims=True)


def _dispatch(base_flat, over_flat, h2b, slot3d, aff3d, n_e, cap):
    b, s, d = h2b.shape
    t = ROUTE_CHUNK
    rows = ROUTE_INNER * t
    grid_spec = pltpu.PrefetchScalarGridSpec(
        num_scalar_prefetch=2,
        grid=(b, s // rows),
        in_specs=[
            pl.BlockSpec((1, rows, d), lambda bi, jo, *_: (bi, jo, 0)),
            pl.BlockSpec((n_e, s // t, t), lambda bi, jo, *_: (bi, 0, 0)),
            pl.BlockSpec((n_e, s // t, t), lambda bi, jo, *_: (bi, 0, 0)),
        ],
        out_specs=[
            pl.BlockSpec((n_e, 1, cap, d), lambda bi, jo, *_: (0, bi, 0, 0)),
            pl.BlockSpec((n_e, 1, 1, cap), lambda bi, jo, *_: (0, bi, 0, 0)),
        ],
        scratch_shapes=[pltpu.VMEM((n_e * ROUTE_WIN, t), BF16), pltpu.VMEM((n_e, cap, 1), F32)],
    )
    return pl.pallas_call(
        functools.partial(_dispatch_kernel, cap=cap),
        grid_spec=grid_spec,
        out_shape=[
            jax.ShapeDtypeStruct((n_e, b, cap, d), BF16),
            jax.ShapeDtypeStruct((n_e, b, 1, cap), F32),
        ],
        compiler_params=_cparams(("parallel", "arbitrary")),
        name="dispatch",
    )(base_flat, over_flat, h2b, slot3d, aff3d)


FFN_ROWS = 512
FFN_CHUNK = 512


def _experts_kernel(xs_ref, gate_ref, wg_ref, wu_ref, wd_ref, y_out, wg_s, wu_s, wd_s, hid_s):
    e = pl.program_id(0)
    f = pl.program_id(1)
    n_chunks = wg_s.shape[1]

    def stage():
        slot = e % 2
        wg_s[slot, f] = wg_ref[0].astype(BF16)
        wu_s[slot, f] = wu_ref[0].astype(BF16)
        wd_s[slot, f] = wd_ref[0].astype(BF16)

    @pl.when(e == 0)
    def _():
        stage()
        y_out[0] = jnp.zeros(y_out.shape[1:], y_out.dtype)

    @pl.when(e > 0)
    def _():
        stage()
        slot = (e - 1) % 2
        xs = xs_ref[0]
        for c in range(n_chunks):
            g = _dot(xs, wg_s[slot, c])
            u = _dot(xs, wu_s[slot, c])
            hid_s[:, c * FFN_CHUNK:(c + 1) * FFN_CHUNK] = (g * _sigmoid(g) * u).astype(BF16)
        y = _dot(hid_s[:, 0:FFN_CHUNK], wd_s[slot, 0])
        for c in range(1, n_chunks):
            y = y + _dot(hid_s[:, c * FFN_CHUNK:(c + 1) * FFN_CHUNK], wd_s[slot, c])
        rows = xs.shape[0]
        eye = lax.broadcasted_iota(jnp.int32, (rows, rows), 0) == lax.broadcasted_iota(jnp.int32, (rows, rows), 1)
        gate_col = jnp.sum(jnp.where(eye, gate_ref[0], 0.0), axis=1, keepdims=True)
        y_out[0] = (y * gate_col).astype(BF16)


def _experts(xs, gate, w_gate, w_up, w_down):
    e, n, d = xs.shape
    ff = w_gate.shape[2]
    n_chunks = ff // FFN_CHUNK
    assert n // FFN_ROWS == n_chunks
    prev = lambda ei, fi: (jnp.maximum(ei - 1, 0), fi, 0)
    cur = lambda ei: jnp.minimum(ei, e - 1)
    return pl.pallas_call(
        _experts_kernel,
        grid=(e + 1, n_chunks),
        in_specs=[
            pl.BlockSpec((1, FFN_ROWS, d), prev),
            pl.BlockSpec((1, 1, FFN_ROWS), lambda ei, fi: (jnp.maximum(ei - 1, 0), 0, fi)),
            pl.BlockSpec((1, d, FFN_CHUNK), lambda ei, fi: (cur(ei), 0, fi)),
            pl.BlockSpec((1, d, FFN_CHUNK), lambda ei, fi: (cur(ei), 0, fi)),
            pl.BlockSpec((1, FFN_CHUNK, d), lambda ei, fi: (cur(ei), fi, 0)),
        ],
        out_specs=pl.BlockSpec((1, FFN_ROWS, d), lambda ei, fi: (jnp.where(ei == 0, e, ei - 1), fi, 0)),
        out_shape=jax.ShapeDtypeStruct((e + 1, n, d), BF16),
        scratch_shapes=[
            pltpu.VMEM((2, n_chunks, d, FFN_CHUNK), BF16),
            pltpu.VMEM((2, n_chunks, d, FFN_CHUNK), BF16),
            pltpu.VMEM((2, n_chunks, FFN_CHUNK, d), BF16),
            pltpu.VMEM((FFN_ROWS, ff), BF16),
        ],
        compiler_params=_cparams(("arbitrary", "arbitrary")),
        name="experts",
    )(xs, gate, w_gate, w_up, w_down)


def _combine_kernel(base_ref, over_ref, slot_t_ref, base_t_ref, y_ref, h2_ref, g3_ref, b3_ref, out_ref, ywin_scr,
                    *, cap):
    bi = pl.program_id(0)
    jo = pl.program_id(1)
    n_j = pl.num_programs(1) * ROUTE_INNER
    n_e = y_ref.shape[0]
    rows = slot_t_ref.shape[1]
    t = ROUTE_CHUNK

    def spread_cols(per_expert):
        width = n_e * per_expert
        col = lax.broadcasted_iota(jnp.int32, (rows, width), 1) // per_expert
        return jnp.where(lax.broadcasted_iota(jnp.int32, (rows, width), 0) == bi * n_e + col, 1.0, 0.0).astype(BF16)

    def lane_in_group(per_expert):
        return (lax.broadcasted_iota(jnp.int32, (1, n_e * per_expert), 1) % per_expert).astype(F32)

    win_spread = spread_cols(ROUTE_WIN)
    win_lane = lane_in_group(ROUTE_WIN)

    def chunk(ji, carry):
        j = jo * ROUTE_INNER + ji
        tok = pl.ds(pl.multiple_of(ji * t, t), t)

        def finish(ff):
            out_ref[0, tok, :] = _layer_norm(DEEPNORM_ALPHA * h2_ref[0, tok, :] + ff, g3_ref[...], b3_ref[...])

        slot_bf = slot_t_ref[tok, :].astype(BF16)

        @pl.when(over_ref[bi * n_j + j] == 0)
        def _():
            slot_wide = _dot(slot_bf, win_spread)
            base_wide = _dot(jnp.broadcast_to(base_t_ref[ji], (SUBLANES, rows)).astype(BF16), win_spread)[0:1, :]
            onehot = jnp.where(slot_wide - base_wide == win_lane, 1.0, 0.0).astype(BF16)
            for e in range(n_e):
                base = pl.multiple_of(base_ref[(bi * n_e + e) * n_j + j], BF16_ROWS)
                ywin_scr[e * ROUTE_WIN:(e + 1) * ROUTE_WIN, :] = y_ref[e, 0, pl.ds(base, ROUTE_WIN), :]
            finish(_dot(onehot, ywin_scr[...]))

        @pl.when(over_ref[bi * n_j + j] != 0)
        def _():
            slot_wide = _dot(slot_bf, spread_cols(cap))
            onehot = jnp.where(slot_wide == lane_in_group(cap), 1.0, 0.0).astype(BF16)
            finish(_dot(onehot, y_ref[:, 0].reshape(n_e * cap, y_ref.shape[3])))

        return carry

    lax.fori_loop(0, ROUTE_INNER, chunk, 0)


def _combine(base_flat, over_flat, slot_t, base_t, y4, h2, g3, b3, cap):
    b, s, d = h2.shape
    e = y4.shape[0] - 1
    rows = ROUTE_INNER * ROUTE_CHUNK
    const = lambda bi, jo, *_: (0, 0)
    grid_spec = pltpu.PrefetchScalarGridSpec(
        num_scalar_prefetch=2,
        grid=(b, s // rows),
        in_specs=[
            pl.BlockSpec((rows, slot_t.shape[1]), lambda bi, jo, *_: (jo, 0)),
            pl.BlockSpec((ROUTE_INNER, 1, base_t.shape[2]), lambda bi, jo, *_: (jo, 0, 0)),
            pl.BlockSpec((e, 1, cap, d), lambda bi, jo, *_: (0, bi, 0, 0)),
            pl.BlockSpec((1, rows, d), lambda bi, jo, *_: (bi, jo, 0)),
            pl.BlockSpec(g3.shape, const),
            pl.BlockSpec(b3.shape, const),
        ],
        out_specs=pl.BlockSpec((1, rows, d), lambda bi, jo, *_: (bi, jo, 0)),
        scratch_shapes=[pltpu.VMEM((e * ROUTE_WIN, d), BF16)],
    )
    return pl.pallas_call(
        functools.partial(_combine_kernel, cap=cap),
        grid_spec=grid_spec,
        out_shape=jax.ShapeDtypeStruct((b, s, d), F32),
        compiler_params=_cparams(("parallel", "parallel")),
        name="combine",
    )(base_flat, over_flat, slot_t, base_t, y4, h2, g3, b3)


def _extend_weights(w_in, w_uq, w_uk, w_uv):
    half = MLA_ROPE_DIM // 2
    d = w_in.shape[0]
    w_t = w_in.T.astype(BF16)
    kr = w_t[SPLIT_KV:SPLIT_KR]
    z = lambda n: jnp.zeros((n, d), BF16)
    assert HEAD_PAD - MLA_QK_DIM == MLA_ROPE_DIM
    kr_tile = jnp.concatenate([z(MLA_NOPE_DIM), kr, -kr[half:], kr[:half]], axis=0)
    w_in_ext = jnp.concatenate([w_t[:SPLIT_KV], kr_tile, w_t[SPLIT_KR:]], axis=0).T

    wq = w_uq.reshape(MLA_Q_RANK, MLA_HEADS, MLA_QK_DIM)
    wq_ext = jnp.concatenate([wq, -wq[:, :, MLA_NOPE_DIM + half:], wq[:, :, MLA_NOPE_DIM:MLA_NOPE_DIM + half]],
                             axis=2).reshape(MLA_Q_RANK, QK_PAD)

    wk = w_uk.reshape(MLA_KV_RANK, MLA_HEADS, MLA_NOPE_DIM)
    wk_full = jnp.concatenate([wk, jnp.zeros((MLA_KV_RANK, MLA_HEADS, HEAD_PAD - MLA_NOPE_DIM), w_uk.dtype)], axis=2)
    wv = w_uv.reshape(MLA_KV_RANK, MLA_HEADS, MLA_V_DIM)
    wv_full = jnp.concatenate([wv, jnp.zeros((MLA_KV_RANK, MLA_HEADS, HEAD_PAD - MLA_V_DIM), w_uv.dtype)], axis=2)
    wkv_ext = jnp.concatenate([wk_full.reshape(MLA_KV_RANK, QK_PAD), wv_full.reshape(MLA_KV_RANK, QK_PAD)], axis=1)
    return w_in_ext.astype(BF16), wq_ext.astype(BF16), wkv_ext.astype(BF16)


def _rope_freq_column():
    inv_freq = ROPE_BASE ** (-jnp.arange(0, MLA_ROPE_DIM, 2, dtype=F32) / MLA_ROPE_DIM)
    return inv_freq[:, None]


def _value_one_lanes():
    one_hot = (jnp.arange(HEAD_PAD) == MLA_V_DIM).astype(F32)
    return jnp.tile(one_hot, MLA_HEADS)[None, :]


def kernel(x, mem, positions, w_in, q_norm_g, w_uq, kv_norm_g, w_uk, w_uv, conv_w, conv_b, conv_ln_g, conv_ln_b,
           w_o, ln1_g, ln1_b, xa_w_q, xa_w_k, xa_w_v, xa_w_o, ln2_g, ln2_b, w_router, w_gate, w_up, w_down,
           ln3_g, ln3_b):
    assert w_in.shape[0] == DEPTH == 1
    b, s, d = x.shape
    cap = CAPACITY_FACTOR * s // N_EXPERTS
    assert s % max(FRONT_ROWS, ATTN_ROWS, MID_ROWS, ROUTE_INNER * ROUTE_CHUNK) == 0 and cap >= ROUTE_WIN
    assert mem.shape == (b, MEM_LEN, d) and d == D_MODEL

    w_in_ext, wq_ext, wkv_ext = _extend_weights(w_in[0], w_uq[0], w_uk[0], w_uv[0])
    posf = positions.astype(F32)[:, None, :]
    q, k, v, u_pre = _front(x, posf, w_in_ext, q_norm_g, wq_ext, kv_norm_g, wkv_ext, _rope_freq_column(),
                             _value_one_lanes(), FRONT_ROWS)
    attn = _attention(q, k, v, ATTN_ROWS)
    h2, h2b, aff = _mid(u_pre, attn, x, conv_w[0], conv_b, conv_ln_g, conv_ln_b, w_o[0].astype(BF16), ln1_g, ln1_b,
                        (xa_w_q[0] * MEM_Q_SCALE).astype(BF16), mem, xa_w_k[0].astype(BF16), xa_w_v[0].astype(BF16),
                        xa_w_o[0].astype(BF16), ln2_g, ln2_b, _split_router(w_router[0]), MID_ROWS)

    aff2d = aff.reshape(b * N_EXPERTS, s)
    slot, slot_t, off_tab = _topk(aff2d, cap)
    n_j = s // ROUTE_CHUNK
    base, over = _route_tables(off_tab[:, :n_j + 1], b, N_EXPERTS, cap)
    base_flat, over_flat = base.reshape(-1), over.reshape(-1)
    by_chunk = lambda a: a.reshape(b * N_EXPERTS, n_j, ROUTE_CHUNK)
    xs, gate = _dispatch(base_flat, over_flat, h2b, by_chunk(slot), by_chunk(aff2d), N_EXPERTS, cap)
    y = _experts(xs.reshape(N_EXPERTS, b * cap, d), gate.reshape(N_EXPERTS, 1, b * cap),
                 w_gate[0], w_up[0], w_down[0])
    base_t = base.T.astype(F32).reshape(n_j, 1, b * N_EXPERTS)
    return _combine(base_flat, over_flat, slot_t, base_t, y.reshape(N_EXPERTS + 1, b, cap, d), h2, ln3_g, ln3_b, cap)
```

```python
import functools
import math

import jax
import jax.numpy as jnp
from jax import lax
from jax.experimental import pallas as pl
from jax.experimental.pallas import tpu as pltpu

F32 = jnp.float32
BF16 = jnp.bfloat16

D_MODEL = 1024
MLA_HEADS = 8
MLA_NOPE_DIM = 64
MLA_ROPE_DIM = 32
MLA_QK_DIM = MLA_NOPE_DIM + MLA_ROPE_DIM
MLA_V_DIM = 64
MLA_Q_RANK = 256
MLA_KV_RANK = 128
MLA_WIDTH = MLA_HEADS * MLA_V_DIM
CONV_CH = D_MODEL - MLA_WIDTH
CONV_WIDTH = 31
CONV_PAD = (CONV_WIDTH - 1) // 2
ROPE_BASE = 10000.0
MEM_HEADS = 4
MEM_HEAD_DIM = D_MODEL // MEM_HEADS
MEM_Q_SCALE = 1.0 / math.sqrt(MEM_HEAD_DIM)
assert math.frexp(MEM_Q_SCALE)[0] == 0.5
N_EXPERTS = 16
EXPERT_FF = 2048
CAPACITY_FACTOR = 2
NORM_EPS = 1e-5
DEPTH = 1
DEEPNORM_ALPHA = (2.0 * DEPTH) ** 0.25
SPLIT_Q = MLA_Q_RANK
SPLIT_KV = SPLIT_Q + MLA_KV_RANK
SPLIT_KR = SPLIT_KV + MLA_ROPE_DIM

HEAD_PAD = 128
QK_PAD = MLA_HEADS * HEAD_PAD
OFF_CQ = 0
OFF_CKV = OFF_CQ + MLA_Q_RANK
OFF_KR = OFF_CKV + MLA_KV_RANK
OFF_A = OFF_KR + HEAD_PAD
OFF_G = OFF_A + CONV_CH
IN_EXT = OFF_G + CONV_CH

SUBLANES = 8
LANES = 128
BF16_ROWS = 16
ROUTE_CHUNK = 256
ROUTE_WIN = 64
ROUTE_INNER = 4
HALO = 16
VMEM_LIMIT = 56 * 1024 * 1024
FRONT_ROWS = 1024
ATTN_ROWS = 1024
MID_ROWS = 1024
MEM_LEN = 256


def _cparams(sem):
    return pltpu.CompilerParams(dimension_semantics=sem, vmem_limit_bytes=VMEM_LIMIT)


def _layer_norm(v, g, b):
    mu = jnp.mean(v, axis=-1, keepdims=True)
    d = v - mu
    var = jnp.mean(d * d, axis=-1, keepdims=True)
    return d * lax.rsqrt(var + NORM_EPS) * g + b


def _rms_norm(v, g):
    return v * lax.rsqrt(jnp.mean(v * v, axis=-1, keepdims=True) + NORM_EPS) * g


def _sigmoid(v):
    return 1.0 / (1.0 + jnp.exp(-v))


def _dot(a, b):
    return jnp.dot(a, b, preferred_element_type=F32)


def _dot_nt(a, b, precision=None):
    return lax.dot_general(a, b, (((1,), (1,)), ((), ())), preferred_element_type=F32, precision=precision)


def _front_kernel(x_ref, pos_ref, win_ref, qg_ref, wq_ref, kvg_ref, wkv_ref, invf_ref, vone_ref,
                  q_out, k_out, v_out, u_out):
    x = x_ref[0].astype(BF16)
    hc = _dot(x, win_ref[...])
    cqn = _rms_norm(hc[:, OFF_CQ:OFF_CKV], qg_ref[...])
    qq = _dot(cqn.astype(BF16), wq_ref[...])
    ckvn = _rms_norm(hc[:, OFF_CKV:OFF_KR], kvg_ref[...])
    kv = _dot(ckvn.astype(BF16), wkv_ref[...])
    tm = x_ref.shape[1]
    ang = invf_ref[...] * pos_ref[0]
    cos_h, sin_h = jnp.cos(ang), jnp.sin(ang)
    ones = lambda n: jnp.ones((n, tm), F32)
    zeros = lambda n: jnp.zeros((n, tm), F32)
    cos = jnp.concatenate([ones(MLA_NOPE_DIM), cos_h, cos_h, ones(HEAD_PAD - MLA_QK_DIM)], axis=0).T
    sin = jnp.concatenate([zeros(MLA_NOPE_DIM), sin_h, sin_h, zeros(HEAD_PAD - MLA_QK_DIM)], axis=0).T
    kr = hc[:, OFF_KR:OFF_A]
    kr_partner = pltpu.roll(kr, HEAD_PAD - MLA_ROPE_DIM, axis=1)
    rope_lanes = lax.broadcasted_iota(jnp.int32, (1, HEAD_PAD), 1) < MLA_QK_DIM
    krot = jnp.where(rope_lanes, kr * cos + kr_partner * sin, 0.0)
    scale = math.log2(math.e) / math.sqrt(MLA_QK_DIM)
    for h in range(MLA_HEADS):
        lo, hi = h * HEAD_PAD, (h + 1) * HEAD_PAD
        qh = (qq[:, lo:hi] * cos + pltpu.roll(qq[:, lo:hi], HEAD_PAD - MLA_ROPE_DIM, axis=1) * sin) * scale
        q_out[0, :, lo:hi] = qh.astype(BF16)
        k_out[0, :, lo:hi] = (kv[:, lo:hi] + krot).astype(BF16)
    v_out[0] = (kv[:, QK_PAD:] + vone_ref[...]).astype(BF16)
    u_out[0] = hc[:, OFF_A:OFF_G] * _sigmoid(hc[:, OFF_G:IN_EXT])


def _front(x, posf, w_in_ext, qg, wq_ext, kvg, wkv_ext, invf, vone, tm):
    b, s, d = x.shape
    const = lambda bi, i: (0, 0)
    return pl.pallas_call(
        _front_kernel,
        grid=(b, s // tm),
        in_specs=[
            pl.BlockSpec((1, tm, d), lambda bi, i: (bi, i, 0)),
            pl.BlockSpec((1, 1, tm), lambda bi, i: (bi, 0, i)),
            pl.BlockSpec(w_in_ext.shape, const),
            pl.BlockSpec(qg.shape, const),
            pl.BlockSpec(wq_ext.shape, const),
            pl.BlockSpec(kvg.shape, const),
            pl.BlockSpec(wkv_ext.shape, const),
            pl.BlockSpec(invf.shape, const),
            pl.BlockSpec(vone.shape, const),
        ],
        out_specs=[
            pl.BlockSpec((1, tm, QK_PAD), lambda bi, i: (bi, i, 0)),
            pl.BlockSpec((1, tm, QK_PAD), lambda bi, i: (bi, i, 0)),
            pl.BlockSpec((1, tm, QK_PAD), lambda bi, i: (bi, i, 0)),
            pl.BlockSpec((1, tm, CONV_CH), lambda bi, i: (bi, i, 0)),
        ],
        out_shape=[
            jax.ShapeDtypeStruct((b, s, QK_PAD), BF16),
            jax.ShapeDtypeStruct((b, s, QK_PAD), BF16),
            jax.ShapeDtypeStruct((b, s, QK_PAD), BF16),
            jax.ShapeDtypeStruct((b, s, CONV_CH), F32),
        ],
        compiler_params=_cparams(("parallel", "parallel")),
        name="front",
    )(x, posf, w_in_ext, qg, wq_ext, kvg, wkv_ext, invf, vone)


def _attn_kernel(q_ref, k_ref, v_ref, o_ref):
    for h in range(MLA_HEADS):
        lo, hi = h * HEAD_PAD, (h + 1) * HEAD_PAD
        sc = _dot_nt(q_ref[0, :, lo:hi], k_ref[0, :, lo:hi])
        p = jnp.exp2(sc - jnp.max(sc, axis=-1, keepdims=True))
        pv = _dot(p.astype(BF16), v_ref[0, :, lo:hi])
        o = pv[:, :MLA_V_DIM] / pv[:, MLA_V_DIM:MLA_V_DIM + 1]
        o_ref[0, :, h * MLA_V_DIM:(h + 1) * MLA_V_DIM] = o.astype(BF16)


def _attention(q, k, v, tq):
    b, s, _ = q.shape
    return pl.pallas_call(
        _attn_kernel,
        grid=(b, s // tq),
        in_specs=[
            pl.BlockSpec((1, tq, QK_PAD), lambda bi, i: (bi, i, 0)),
            pl.BlockSpec((1, s, QK_PAD), lambda bi, i: (bi, 0, 0)),
            pl.BlockSpec((1, s, QK_PAD), lambda bi, i: (bi, 0, 0)),
        ],
        out_specs=pl.BlockSpec((1, tq, MLA_WIDTH), lambda bi, i: (bi, i, 0)),
        out_shape=jax.ShapeDtypeStruct((b, s, MLA_WIDTH), BF16),
        compiler_params=_cparams(("parallel", "parallel")),
        name="attn",
    )(q, k, v)


def _mem_kv_kernel(mem_ref, wk_ref, wv_ref, k_out, v_out, wk_s, wv_s):
    @pl.when(pl.program_id(0) == 0)
    def _():
        wk_s[...] = wk_ref[...].astype(BF16)
        wv_s[...] = wv_ref[...].astype(BF16)

    m = mem_ref[0].astype(BF16)
    k_out[0] = _dot(m, wk_s[...]).astype(BF16)
    v_out[0] = _dot(m, wv_s[...]).astype(BF16)


def _mem_kv(mem, wk, wv):
    b, m, d = mem.shape
    const = lambda bi: (0, 0)
    return pl.pallas_call(
        _mem_kv_kernel,
        grid=(b,),
        in_specs=[
            pl.BlockSpec((1, m, d), lambda bi: (bi, 0, 0)),
            pl.BlockSpec(wk.shape, const, pipeline_mode=pl.Buffered(1)),
            pl.BlockSpec(wv.shape, const, pipeline_mode=pl.Buffered(1)),
        ],
        out_specs=[pl.BlockSpec((1, m, d), lambda bi: (bi, 0, 0))] * 2,
        out_shape=[jax.ShapeDtypeStruct((b, m, d), BF16)] * 2,
        scratch_shapes=[pltpu.VMEM(wk.shape, BF16), pltpu.VMEM(wv.shape, BF16)],
        compiler_params=_cparams(("arbitrary",)),
        name="mem_kv",
    )(mem, wk, wv)


CONV_SUB = 64
CONV_TILE = 512


def _mid_kernel(ucur_ref, uprev_ref, unext_ref, attn_ref, x_ref, cw_ref, cb_ref, cg_ref, cbeta_ref, wo_ref,
                g1_ref, b1_ref, wq_ref, kx_ref, vx_ref, xwo_ref, g2_ref, b2_ref, wr_ref,
                h2_out, h2b_out, aff_out, win_ref, shift_ref, u_scr, o_scr):
    tm = ucur_ref.shape[1]
    i = pl.program_id(1)
    last = pl.num_programs(1) - 1
    win_ref[0:HALO, :] = jnp.where(i > 0, uprev_ref[0], 0.0)
    win_ref[HALO:HALO + tm, :] = ucur_ref[0]
    win_ref[HALO + tm:2 * HALO + tm, :] = jnp.where(i < last, unext_ref[0], 0.0)
    span = CONV_TILE + 2 * HALO - SUBLANES
    for c in range(tm // CONV_TILE):
        base = c * CONV_TILE
        shift_ref[0] = win_ref[base:base + CONV_TILE + 2 * HALO, :]
        for j in range(1, SUBLANES):
            shift_ref[j, 0:span, :] = win_ref[base + j:base + j + span, :]
        for r in range(CONV_TILE // CONV_SUB):
            acc = jnp.broadcast_to(cb_ref[...], (CONV_SUB, CONV_CH))
            for t in range(CONV_WIDTH):
                off = HALO - CONV_PAD + t
                row = r * CONV_SUB + (off // SUBLANES) * SUBLANES
                acc = acc + shift_ref[off % SUBLANES, row:row + CONV_SUB, :] * cw_ref[t:t + 1, :]
            y = _layer_norm(acc, cg_ref[...], cbeta_ref[...])
            u_scr[base + r * CONV_SUB:base + (r + 1) * CONV_SUB, :] = (y * _sigmoid(y)).astype(BF16)
    mix = _dot(attn_ref[0], wo_ref[0:MLA_WIDTH, :]) + _dot(u_scr[...], wo_ref[MLA_WIDTH:, :])
    h1 = _layer_norm(DEEPNORM_ALPHA * x_ref[0] + mix, g1_ref[...], b1_ref[...])
    q = _dot(h1.astype(BF16), wq_ref[...]).astype(BF16)
    for h in range(MEM_HEADS):
        lo, hi = h * MEM_HEAD_DIM, (h + 1) * MEM_HEAD_DIM
        sc = _dot_nt(q[:, lo:hi], kx_ref[0, :, lo:hi])
        p = jnp.exp(sc - jnp.max(sc, axis=-1, keepdims=True))
        l = jnp.sum(p, axis=-1, keepdims=True)
        o_scr[:, lo:hi] = (_dot(p.astype(BF16), vx_ref[0, :, lo:hi]) / l).astype(BF16)
    xa = _dot(o_scr[...], xwo_ref[...])
    h2 = _layer_norm(DEEPNORM_ALPHA * h1 + xa, g2_ref[...], b2_ref[...])
    h2_out[0] = h2
    h2_hi = h2.astype(BF16)
    h2b_out[0] = h2_hi
    h2_lo = (h2 - h2_hi.astype(F32)).astype(BF16)
    hi_terms = _dot(h2_hi, wr_ref[...])
    logits = hi_terms[:, :LANES] + hi_terms[:, LANES:] + _dot(h2_lo, wr_ref[:, :LANES])
    lt = logits.T[0:aff_out.shape[1], :]
    ex = jnp.exp(lt - jnp.max(lt, axis=0, keepdims=True))
    aff_out[0] = ex / jnp.sum(ex, axis=0, keepdims=True)


def _mid(u_pre, attn, x, conv_w, conv_b, conv_g, conv_beta, w_o, g1, b1, wq, kx, vx, xwo, g2, b2, wr_split, tm):
    b, s, d = x.shape
    m = kx.shape[1]
    nh = tm // HALO
    const = lambda bi, i: (0, 0)
    tile = lambda w: pl.BlockSpec((1, tm, w), lambda bi, i: (bi, i, 0))
    whole = lambda arr: pl.BlockSpec(arr.shape, const)
    once = lambda arr: pl.BlockSpec(arr.shape, const, pipeline_mode=pl.Buffered(1))
    return pl.pallas_call(
        _mid_kernel,
        grid=(b, s // tm),
        in_specs=[
            tile(CONV_CH),
            pl.BlockSpec((1, HALO, CONV_CH), lambda bi, i: (bi, jnp.maximum(i * nh - 1, 0), 0)),
            pl.BlockSpec((1, HALO, CONV_CH), lambda bi, i: (bi, jnp.minimum((i + 1) * nh, s // HALO - 1), 0)),
            tile(MLA_WIDTH),
            tile(d),
            whole(conv_w), whole(conv_b), whole(conv_g), whole(conv_beta), once(w_o), whole(g1), whole(b1),
            once(wq),
            pl.BlockSpec((1, m, d), lambda bi, i: (bi, 0, 0)),
            pl.BlockSpec((1, m, d), lambda bi, i: (bi, 0, 0)),
            once(xwo), whole(g2), whole(b2), whole(wr_split),
        ],
        out_specs=[
            tile(d),
            tile(d),
            pl.BlockSpec((1, N_EXPERTS, tm), lambda bi, i: (bi, 0, i)),
        ],
        out_shape=[
            jax.ShapeDtypeStruct((b, s, d), F32),
            jax.ShapeDtypeStruct((b, s, d), BF16),
            jax.ShapeDtypeStruct((b, N_EXPERTS, s), F32),
        ],
        scratch_shapes=[
            pltpu.VMEM((tm + 2 * HALO, CONV_CH), F32),
            pltpu.VMEM((SUBLANES, CONV_TILE + 2 * HALO, CONV_CH), F32),
            pltpu.VMEM((tm, CONV_CH), BF16),
            pltpu.VMEM((tm, d), BF16),
        ],
        compiler_params=_cparams(("parallel", "parallel")),
        name="mid",
    )(u_pre, u_pre, u_pre, attn, x, conv_w, conv_b, conv_g, conv_beta, w_o, g1, b1, wq, kx, vx, xwo, g2, b2, wr_split)


def _split_router(w_router):
    hi = w_router.astype(BF16)
    lo = (w_router - hi.astype(F32)).astype(BF16)
    pad = jnp.zeros((w_router.shape[0], LANES - w_router.shape[1]), BF16)
    return jnp.concatenate([hi, pad, lo, pad], axis=1)


BISECT_STEPS_PER_CHECK = 4
BISECT_MAX_CHECKS = 320


def _topk_kernel(aff_ref, slot_out, slot_t_out, off_out, *, cap):
    aff = aff_ref[...]
    rows, s = aff.shape
    capf = jnp.float32(cap)

    def not_done(carry):
        return jnp.logical_and(carry[2] > 0, carry[3] < BISECT_MAX_CHECKS)

    def halve(_, bounds):
        lo, hi = bounds
        mid = 0.5 * (lo + hi)
        take = jnp.sum(jnp.where(aff >= mid, 1.0, 0.0), axis=1, keepdims=True) >= capf
        return jnp.where(take, mid, lo), jnp.where(take, hi, mid)

    def bisect(carry):
        lo, hi = lax.fori_loop(0, BISECT_STEPS_PER_CHECK, halve, carry[:2])
        smallest_in = jnp.min(jnp.where(aff >= lo, aff, jnp.inf), axis=1, keepdims=True)
        largest_in = jnp.max(jnp.where(aff < hi, aff, -jnp.inf), axis=1, keepdims=True)
        open_rows = jnp.sum(jnp.where(smallest_in == largest_in, 0.0, 1.0))
        return lo, hi, open_rows.astype(jnp.int32), carry[3] + 1

    _, hi, _, _ = lax.while_loop(not_done, bisect, (jnp.zeros((rows, 1), F32), jnp.full((rows, 1), 2.0, F32),
                                                    jnp.int32(1), jnp.int32(0)))
    th = jnp.max(jnp.where(aff < hi, aff, -jnp.inf), axis=1, keepdims=True)
    gt = aff > th
    eq = aff == th
    n_gt = jnp.sum(jnp.where(gt, 1.0, 0.0), axis=1, keepdims=True)
    tri = jnp.where(lax.broadcasted_iota(jnp.int32, (s, s), 0) < lax.broadcasted_iota(jnp.int32, (s, s), 1),
                    1.0, 0.0).astype(BF16)
    tie_rank = _dot(jnp.where(eq, 1.0, 0.0).astype(BF16), tri)
    sel = jnp.logical_or(gt, jnp.logical_and(eq, tie_rank < (capf - n_gt)))
    pos = _dot(jnp.where(sel, 1.0, 0.0).astype(BF16), tri)
    slot = jnp.where(sel, pos, -1.0)
    slot_out[...] = slot
    slot_t_out[...] = slot.T
    lanes = off_out.shape[1]
    before = jnp.where(lax.broadcasted_iota(jnp.int32, (s, lanes), 0)
                       < ROUTE_CHUNK * lax.broadcasted_iota(jnp.int32, (s, lanes), 1), 1.0, 0.0).astype(BF16)
    off_out[...] = _dot(jnp.where(sel, 1.0, 0.0).astype(BF16), before)


def _topk(aff2d, cap):
    rows, s = aff2d.shape
    return pl.pallas_call(
        functools.partial(_topk_kernel, cap=cap),
        out_shape=[
            jax.ShapeDtypeStruct((rows, s), F32),
            jax.ShapeDtypeStruct((s, rows), F32),
            jax.ShapeDtypeStruct((rows, LANES), F32),
        ],
        compiler_params=pltpu.CompilerParams(vmem_limit_bytes=VMEM_LIMIT),
        name="topk",
    )(aff2d)


def _route_tables(off_tab, n_b, n_e, cap):
    n_j = off_tab.shape[1] - 1
    off = off_tab.astype(jnp.int32)
    base = jnp.minimum((off[:, :n_j] // BF16_ROWS) * BF16_ROWS, cap - ROUTE_WIN)
    over = jnp.any((off[:, 1:] - base > ROUTE_WIN).reshape(n_b, n_e, n_j), axis=1)
    return base, over.astype(jnp.int32)


def _dispatch_kernel(base_ref, over_ref, h2b_ref, slot_ref, aff_ref, xs_out, gate_out, p_scr, gate_scr, *, cap):
    bi = pl.program_id(0)
    jo = pl.program_id(1)
    n_j = pl.num_programs(1) * ROUTE_INNER
    n_e, _, t = slot_ref.shape

    @pl.when(jo == 0)
    def _():
        xs_out[...] = jnp.zeros(xs_out.shape, xs_out.dtype)
        gate_scr[...] = jnp.zeros(gate_scr.shape, gate_scr.dtype)

    def chunk(ji, carry):
        j = jo * ROUTE_INNER + ji
        h = h2b_ref[0, pl.ds(pl.multiple_of(ji * t, t), t), :]
        slot_row = lambda e: slot_ref[e, pl.ds(j, 1), :]
        aff_row = lambda e: aff_ref[e, pl.ds(j, 1), :]

        @pl.when(over_ref[bi * n_j + j] == 0)
        def _():
            w_iota = lax.broadcasted_iota(jnp.int32, (ROUTE_WIN, t), 0).astype(F32)
            bases = []
            for e in range(n_e):
                base = pl.multiple_of(base_ref[(bi * n_e + e) * n_j + j], BF16_ROWS)
                hit = slot_row(e) == w_iota + base.astype(F32)
                p_scr[e * ROUTE_WIN:(e + 1) * ROUTE_WIN, :] = jnp.where(hit, 1.0, 0.0).astype(BF16)
                gate_scr[e, pl.ds(base, ROUTE_WIN), :] += jnp.sum(jnp.where(hit, aff_row(e), 0.0),
                                                                  axis=1, keepdims=True)
                bases.append(base)
            picked = _dot(p_scr[...], h)
            for e in range(n_e):
                rows = pl.ds(bases[e], ROUTE_WIN)
                xs_out[e, 0, rows, :] += picked[e * ROUTE_WIN:(e + 1) * ROUTE_WIN, :].astype(BF16)

        @pl.when(over_ref[bi * n_j + j] != 0)
        def _():
            c_iota = lax.broadcasted_iota(jnp.int32, (cap, t), 0).astype(F32)
            for e in range(n_e):
                hit = slot_row(e) == c_iota
                xs_out[e, 0] += _dot(jnp.where(hit, 1.0, 0.0).astype(BF16), h).astype(BF16)
                gate_scr[e] += jnp.sum(jnp.where(hit, aff_row(e), 0.0), axis=1, keepdims=True)

        return carry

    lax.fori_loop(0, ROUTE_INNER, chunk, 0)

    @pl.when(jo == pl.num_programs(1) - 1)
    def _():
        eye = lax.broadcasted_iota(jnp.int32, (cap, cap), 0) == lax.broadcasted_iota(jnp.int32, (cap, cap), 1)
        for e in range(n_e):
            gate_out[e, 0] = jnp.sum(jnp.where(eye, gate_scr[e], 0.0), axis=0, keepdims=True)


def _dispatch(base_flat, over_flat, h2b, slot3d, aff3d, n_e, cap):
    b, s, d = h2b.shape
    t = ROUTE_CHUNK
    rows = ROUTE_INNER * t
    grid_spec = pltpu.PrefetchScalarGridSpec(
        num_scalar_prefetch=2,
        grid=(b, s // rows),
        in_specs=[
            pl.BlockSpec((1, rows, d), lambda bi, jo, *_: (bi, jo, 0)),
            pl.BlockSpec((n_e, s // t, t), lambda bi, jo, *_: (bi, 0, 0)),
            pl.BlockSpec((n_e, s // t, t), lambda bi, jo, *_: (bi, 0, 0)),
        ],
        out_specs=[
            pl.BlockSpec((n_e, 1, cap, d), lambda bi, jo, *_: (0, bi, 0, 0)),
            pl.BlockSpec((n_e, 1, 1, cap), lambda bi, jo, *_: (0, bi, 0, 0)),
        ],
        scratch_shapes=[pltpu.VMEM((n_e * ROUTE_WIN, t), BF16), pltpu.VMEM((n_e, cap, 1), F32)],
    )
    return pl.pallas_call(
        functools.partial(_dispatch_kernel, cap=cap),
        grid_spec=grid_spec,
        out_shape=[
            jax.ShapeDtypeStruct((n_e, b, cap, d), BF16),
            jax.ShapeDtypeStruct((n_e, b, 1, cap), F32),
        ],
        compiler_params=_cparams(("parallel", "arbitrary")),
        name="dispatch",
    )(base_flat, over_flat, h2b, slot3d, aff3d)


FFN_ROWS = 512
FFN_CHUNK = 512


def _experts_kernel(xs_ref, gate_ref, wg_ref, wu_ref, wd_ref, y_out, wg_s, wu_s, wd_s, hid_s):
    e = pl.program_id(0)
    f = pl.program_id(1)
    n_chunks = wg_s.shape[1]

    def stage():
        slot = e % 2
        wg_s[slot, f] = wg_ref[0].astype(BF16)
        wu_s[slot, f] = wu_ref[0].astype(BF16)
        wd_s[slot, f] = wd_ref[0].astype(BF16)

    @pl.when(e == 0)
    def _():
        stage()
        y_out[0] = jnp.zeros(y_out.shape[1:], y_out.dtype)

    @pl.when(e > 0)
    def _():
        stage()
        slot = (e - 1) % 2
        xs = xs_ref[0]
        for c in range(n_chunks):
            g = _dot(xs, wg_s[slot, c])
            u = _dot(xs, wu_s[slot, c])
            hid_s[:, c * FFN_CHUNK:(c + 1) * FFN_CHUNK] = (g * _sigmoid(g) * u).astype(BF16)
        y = _dot(hid_s[:, 0:FFN_CHUNK], wd_s[slot, 0])
        for c in range(1, n_chunks):
            y = y + _dot(hid_s[:, c * FFN_CHUNK:(c + 1) * FFN_CHUNK], wd_s[slot, c])
        rows = xs.shape[0]
        eye = lax.broadcasted_iota(jnp.int32, (rows, rows), 0) == lax.broadcasted_iota(jnp.int32, (rows, rows), 1)
        gate_col = jnp.sum(jnp.where(eye, gate_ref[0], 0.0), axis=1, keepdims=True)
        y_out[0] = (y * gate_col).astype(BF16)


def _experts(xs, gate, w_gate, w_up, w_down):
    e, n, d = xs.shape
    ff = w_gate.shape[2]
    n_chunks = ff // FFN_CHUNK
    assert n // FFN_ROWS == n_chunks
    prev = lambda ei, fi: (jnp.maximum(ei - 1, 0), fi, 0)
    cur = lambda ei: jnp.minimum(ei, e - 1)
    return pl.pallas_call(
        _experts_kernel,
        grid=(e + 1, n_chunks),
        in_specs=[
            pl.BlockSpec((1, FFN_ROWS, d), prev),
            pl.BlockSpec((1, 1, FFN_ROWS), lambda ei, fi: (jnp.maximum(ei - 1, 0), 0, fi)),
            pl.BlockSpec((1, d, FFN_CHUNK), lambda ei, fi: (cur(ei), 0, fi)),
            pl.BlockSpec((1, d, FFN_CHUNK), lambda ei, fi: (cur(ei), 0, fi)),
            pl.BlockSpec((1, FFN_CHUNK, d), lambda ei, fi: (cur(ei), fi, 0)),
        ],
        out_specs=pl.BlockSpec((1, FFN_ROWS, d), lambda ei, fi: (jnp.where(ei == 0, e, ei - 1), fi, 0)),
        out_shape=jax.ShapeDtypeStruct((e + 1, n, d), BF16),
        scratch_shapes=[
            pltpu.VMEM((2, n_chunks, d, FFN_CHUNK), BF16),
            pltpu.VMEM((2, n_chunks, d, FFN_CHUNK), BF16),
            pltpu.VMEM((2, n_chunks, FFN_CHUNK, d), BF16),
            pltpu.VMEM((FFN_ROWS, ff), BF16),
        ],
        compiler_params=_cparams(("arbitrary", "arbitrary")),
        name="experts",
    )(xs, gate, w_gate, w_up, w_down)


def _combine_kernel(base_ref, over_ref, slot_t_ref, base_t_ref, y_ref, h2_ref, g3_ref, b3_ref, out_ref, ywin_scr,
                    *, cap):
    bi = pl.program_id(0)
    jo = pl.program_id(1)
    n_j = pl.num_programs(1) * ROUTE_INNER
    n_e = y_ref.shape[0]
    rows = slot_t_ref.shape[1]
    t = ROUTE_CHUNK

    def spread_cols(per_expert):
        width = n_e * per_expert
        col = lax.broadcasted_iota(jnp.int32, (rows, width), 1) // per_expert
        return jnp.where(lax.broadcasted_iota(jnp.int32, (rows, width), 0) == bi * n_e + col, 1.0, 0.0).astype(BF16)

    def lane_in_group(per_expert):
        return (lax.broadcasted_iota(jnp.int32, (1, n_e * per_expert), 1) % per_expert).astype(F32)

    win_spread = spread_cols(ROUTE_WIN)
    win_lane = lane_in_group(ROUTE_WIN)

    def chunk(ji, carry):
        j = jo * ROUTE_INNER + ji
        tok = pl.ds(pl.multiple_of(ji * t, t), t)

        def finish(ff):
            out_ref[0, tok, :] = _layer_norm(DEEPNORM_ALPHA * h2_ref[0, tok, :] + ff, g3_ref[...], b3_ref[...])

        slot_bf = slot_t_ref[tok, :].astype(BF16)

        @pl.when(over_ref[bi * n_j + j] == 0)
        def _():
            slot_wide = _dot(slot_bf, win_spread)
            base_wide = _dot(jnp.broadcast_to(base_t_ref[ji], (SUBLANES, rows)).astype(BF16), win_spread)[0:1, :]
            onehot = jnp.where(slot_wide - base_wide == win_lane, 1.0, 0.0).astype(BF16)
            for e in range(n_e):
                base = pl.multiple_of(base_ref[(bi * n_e + e) * n_j + j], BF16_ROWS)
                ywin_scr[e * ROUTE_WIN:(e + 1) * ROUTE_WIN, :] = y_ref[e, 0, pl.ds(base, ROUTE_WIN), :]
            finish(_dot(onehot, ywin_scr[...]))

        @pl.when(over_ref[bi * n_j + j] != 0)
        def _():
            slot_wide = _dot(slot_bf, spread_cols(cap))
            onehot = jnp.where(slot_wide == lane_in_group(cap), 1.0, 0.0).astype(BF16)
            finish(_dot(onehot, y_ref[:, 0].reshape(n_e * cap, y_ref.shape[3])))

        return carry

    lax.fori_loop(0, ROUTE_INNER, chunk, 0)


def _combine(base_flat, over_flat, slot_t, base_t, y4, h2, g3, b3, cap):
    b, s, d = h2.shape
    e = y4.shape[0] - 1
    rows = ROUTE_INNER * ROUTE_CHUNK
    const = lambda bi, jo, *_: (0, 0)
    grid_spec = pltpu.PrefetchScalarGridSpec(
        num_scalar_prefetch=2,
        grid=(b, s // rows),
        in_specs=[
            pl.BlockSpec((rows, slot_t.shape[1]), lambda bi, jo, *_: (jo, 0)),
            pl.BlockSpec((ROUTE_INNER, 1, base_t.shape[2]), lambda bi, jo, *_: (jo, 0, 0)),
            pl.BlockSpec((e, 1, cap, d), lambda bi, jo, *_: (0, bi, 0, 0)),
            pl.BlockSpec((1, rows, d), lambda bi, jo, *_: (bi, jo, 0)),
            pl.BlockSpec(g3.shape, const),
            pl.BlockSpec(b3.shape, const),
        ],
        out_specs=pl.BlockSpec((1, rows, d), lambda bi, jo, *_: (bi, jo, 0)),
        scratch_shapes=[pltpu.VMEM((e * ROUTE_WIN, d), BF16)],
    )
    return pl.pallas_call(
        functools.partial(_combine_kernel, cap=cap),
        grid_spec=grid_spec,
        out_shape=jax.ShapeDtypeStruct((b, s, d), F32),
        compiler_params=_cparams(("parallel", "parallel")),
        name="combine",
    )(base_flat, over_flat, slot_t, base_t, y4, h2, g3, b3)


def _extend_weights(w_in, w_uq, w_uk, w_uv):
    half = MLA_ROPE_DIM // 2
    d = w_in.shape[0]
    w_t = w_in.T.astype(BF16)
    kr = w_t[SPLIT_KV:SPLIT_KR]
    z = lambda n: jnp.zeros((n, d), BF16)
    assert HEAD_PAD - MLA_QK_DIM == MLA_ROPE_DIM
    kr_tile = jnp.concatenate([z(MLA_NOPE_DIM), kr, -kr[half:], kr[:half]], axis=0)
    w_in_ext = jnp.concatenate([w_t[:SPLIT_KV], kr_tile, w_t[SPLIT_KR:]], axis=0).T

    wq = w_uq.reshape(MLA_Q_RANK, MLA_HEADS, MLA_QK_DIM)
    wq_ext = jnp.concatenate([wq, -wq[:, :, MLA_NOPE_DIM + half:], wq[:, :, MLA_NOPE_DIM:MLA_NOPE_DIM + half]],
                             axis=2).reshape(MLA_Q_RANK, QK_PAD)

    wk = w_uk.reshape(MLA_KV_RANK, MLA_HEADS, MLA_NOPE_DIM)
    wk_full = jnp.concatenate([wk, jnp.zeros((MLA_KV_RANK, MLA_HEADS, HEAD_PAD - MLA_NOPE_DIM), w_uk.dtype)], axis=2)
    wv = w_uv.reshape(MLA_KV_RANK, MLA_HEADS, MLA_V_DIM)
    wv_full = jnp.concatenate([wv, jnp.zeros((MLA_KV_RANK, MLA_HEADS, HEAD_PAD - MLA_V_DIM), w_uv.dtype)], axis=2)
    wkv_ext = jnp.concatenate([wk_full.reshape(MLA_KV_RANK, QK_PAD), wv_full.reshape(MLA_KV_RANK, QK_PAD)], axis=1)
    return w_in_ext.astype(BF16), wq_ext.astype(BF16), wkv_ext.astype(BF16)


def _rope_freq_column():
    inv_freq = ROPE_BASE ** (-jnp.arange(0, MLA_ROPE_DIM, 2, dtype=F32) / MLA_ROPE_DIM)
    return inv_freq[:, None]


def _value_one_lanes():
    one_hot = (jnp.arange(HEAD_PAD) == MLA_V_DIM).astype(F32)
    return jnp.tile(one_hot, MLA_HEADS)[None, :]


def kernel(x, mem, positions, w_in, q_norm_g, w_uq, kv_norm_g, w_uk, w_uv, conv_w, conv_b, conv_ln_g, conv_ln_b,
           w_o, ln1_g, ln1_b, xa_w_q, xa_w_k, xa_w_v, xa_w_o, ln2_g, ln2_b, w_router, w_gate, w_up, w_down,
           ln3_g, ln3_b):
    assert w_in.shape[0] == DEPTH == 1
    b, s, d = x.shape
    cap = CAPACITY_FACTOR * s // N_EXPERTS
    assert s % max(FRONT_ROWS, ATTN_ROWS, MID_ROWS, ROUTE_INNER * ROUTE_CHUNK) == 0 and cap >= ROUTE_WIN
    assert mem.shape == (b, MEM_LEN, d) and d == D_MODEL

    w_in_ext, wq_ext, wkv_ext = _extend_weights(w_in[0], w_uq[0], w_uk[0], w_uv[0])
    posf = positions.astype(F32)[:, None, :]
    q, k, v, u_pre = _front(x, posf, w_in_ext, q_norm_g, wq_ext, kv_norm_g, wkv_ext, _rope_freq_column(),
                             _value_one_lanes(), FRONT_ROWS)
    attn = _attention(q, k, v, ATTN_ROWS)
    kx, vx = _mem_kv(mem, xa_w_k[0], xa_w_v[0])
    h2, h2b, aff = _mid(u_pre, attn, x, conv_w[0], conv_b, conv_ln_g, conv_ln_b, w_o[0].astype(BF16), ln1_g, ln1_b,
                        (xa_w_q[0] * MEM_Q_SCALE).astype(BF16), kx, vx, xa_w_o[0].astype(BF16), ln2_g, ln2_b,
                        _split_router(w_router[0]), MID_ROWS)

    aff2d = aff.reshape(b * N_EXPERTS, s)
    slot, slot_t, off_tab = _topk(aff2d, cap)
    n_j = s // ROUTE_CHUNK
    base, over = _route_tables(off_tab[:, :n_j + 1], b, N_EXPERTS, cap)
    base_flat, over_flat = base.reshape(-1), over.reshape(-1)
    by_chunk = lambda a: a.reshape(b * N_EXPERTS, n_j, ROUTE_CHUNK)
    xs, gate = _dispatch(base_flat, over_flat, h2b, by_chunk(slot), by_chunk(aff2d), N_EXPERTS, cap)
    y = _experts(xs.reshape(N_EXPERTS, b * cap, d), gate.reshape(N_EXPERTS, 1, b * cap),
                 w_gate[0], w_up[0], w_down[0])
    base_t = base.T.astype(F32).reshape(n_j, 1, b * N_EXPERTS)
    return _combine(base_flat, over_flat, slot_t, base_t, y.reshape(N_EXPERTS + 1, b, cap, d), h2, ln3_g, ln3_b, cap)
```
